```python
import jax, jax.numpy as jnp
from jax import lax
import numpy as np

D_MODEL = 1024
BATCH = 2
SEQ = 16384
DEPTH = 2

CTX_LEN = 256
GRID_W = 64
A_HEADS = 4
A_DK = 128
A_DV = 128
A_KW = A_HEADS * A_DK
A_WIDTH = A_HEADS * A_DV
B_WIDTH = D_MODEL - A_WIDTH
B_GROUPS = 4
CONV_W = 3
C_GROUPS = 4
C_GROUP = 128
C_WIDTH = C_GROUPS * C_GROUP
D_WIDTH = D_MODEL - C_WIDTH
D_GROUPS = 4
D_GDIM = D_WIDTH // D_GROUPS
D_CHUNK = 128
SCAN_CHUNK = 32
EVEN_IN = 3 * A_KW + 2 * A_WIDTH + 3 * B_WIDTH
ODD_IN = C_WIDTH + 2 * D_WIDTH
N_EXPERTS = 32
TOP_K = 4
D_EXPERT = 1024
SWIGLU_ALPHA = 1.702
SWIGLU_LIMIT = 7.0
MOE_BLOCK = 128
N_EVEN = (DEPTH + 1) // 2
N_ODD = DEPTH // 2
DN_ALPHA = (2 * DEPTH) ** 0.25
DN_BETA = (8 * DEPTH) ** -0.25
EPS = 1e-5

kernel_name = "hybrid_hgrn2_conv_fourier_gmlp_moe_dit"


def _split(a, sizes):
    return jnp.split(a, [int(s) for s in np.cumsum(sizes)[:-1]], axis=-1)


def _layer_norm(x, w, b):
    xf = x.astype(jnp.float32)
    mu = jnp.mean(xf, axis=-1, keepdims=True)
    var = jnp.mean(jnp.square(xf - mu), axis=-1, keepdims=True)
    return ((xf - mu) * lax.rsqrt(var + EPS) * w + b).astype(x.dtype)


def _group_rms(x, w, groups):
    shp = x.shape
    xf = x.astype(jnp.float32).reshape(shp[:-1] + (groups, shp[-1] // groups))
    xf = xf * lax.rsqrt(jnp.mean(jnp.square(xf), axis=-1, keepdims=True) + EPS)
    return (xf.reshape(shp) * w).astype(x.dtype)


def _group_ln(x, w, b, groups):
    shp = x.shape
    xf = x.astype(jnp.float32).reshape(shp[:-1] + (groups, shp[-1] // groups))
    mu = jnp.mean(xf, axis=-1, keepdims=True)
    var = jnp.mean(jnp.square(xf - mu), axis=-1, keepdims=True)
    xf = (xf - mu) * lax.rsqrt(var + EPS)
    return (xf.reshape(shp) * w + b).astype(x.dtype)


def _heads(a):
    bn, t, _ = a.shape
    return a.reshape(bn, t, A_HEADS, -1).transpose(0, 2, 1, 3)


def _chunk_scan(q, log_f, k, v, s0):
    bn, h, t, dk = q.shape
    dv = v.shape[-1]
    n = t // SCAN_CHUNK
    rs = lambda a: a.reshape(bn, h, n, SCAN_CHUNK, a.shape[-1])
    q, log_f, k, v = rs(q), rs(log_f.astype(jnp.float32)), rs(k), rs(v)
    b = jnp.cumsum(log_f, axis=3)
    b_last = b[..., -1:, :]
    q_dec = q * jnp.exp(b)
    k_inv = k * jnp.exp(-b)
    k_end = k * jnp.exp(b_last - b)
    mask = jnp.tril(jnp.ones((SCAN_CHUNK, SCAN_CHUNK), dtype=bool))
    scores = jnp.where(mask, jnp.einsum('bhncd,bhnsd->bhncs', q_dec, k_inv), 0.0)
    o_intra = jnp.einsum('bhncs,bhnsv->bhncv', scores, v)
    chunk_decay = jnp.exp(b_last[..., 0, :])

    def step(s, xs):
        qd, ke, vv, dec = xs
        o = jnp.einsum('bhcd,bhdv->bhcv', qd, s)
        s = dec[..., None] * s + jnp.einsum('bhcd,bhcv->bhdv', ke, vv)
        return s, o

    mv = lambda a: jnp.moveaxis(a, 2, 0)
    s_fin, o_inter = lax.scan(step, s0, (mv(q_dec), mv(k_end), mv(v), mv(chunk_decay)))
    o = o_intra + jnp.moveaxis(o_inter, 0, 2)
    return o.reshape(bn, h, t, dv), s_fin


def _hgrn2_inputs(q, ff, fb, i, lb_f, lb_b):
    q = _heads(jax.nn.silu(q))
    v = _heads(i)

    def gate(z, lb):
        lbh = lb.reshape(A_HEADS, 1, A_DK)
        f = lbh + (1.0 - lbh) * jax.nn.sigmoid(_heads(z).astype(jnp.float32))
        return jnp.log(f), 1.0 - f

    return q, gate(ff, lb_f), gate(fb, lb_b), v


def _hgrn2_out(o, g, norm_w, dtype):
    bn, h, t, dv = o.shape
    o = o.transpose(0, 2, 1, 3)
    o = o * lax.rsqrt(jnp.mean(jnp.square(o), axis=-1, keepdims=True) + EPS)
    return (o.reshape(bn, t, h * dv) * norm_w * jax.nn.silu(g.astype(jnp.float32))).astype(dtype)


def _short_conv(u, gate_b, gate_c, conv_w):
    z = gate_c * u
    zc = lax.conv_general_dilated(z, conv_w[:, None, :], window_strides=(1,), padding=((1, 1),),
                                  dimension_numbers=('NWC', 'WIO', 'NWC'),
                                  feature_group_count=z.shape[-1])
    return gate_b * zc


def _mixer_recurrent_conv(h, hc, w_in, lb_f, lb_b, norm_w, conv_w, w_out, ctx_out):
    sizes = [A_KW, A_KW, A_KW, A_WIDTH, A_WIDTH, B_WIDTH, B_WIDTH, B_WIDTH]
    q, ff, fb, i, g, u, gb, gc = _split(h @ w_in, sizes)
    qc_, ffc, fbc, ic, gcx, uc, gbc, gcc = _split(hc @ w_in, sizes)
    bn = h.shape[0]
    flip = lambda a: jnp.flip(a, axis=2)
    zeros = jnp.zeros((bn, A_HEADS, A_DK, A_DV), jnp.float32)
    qk, (lfc_f, kc_f), (lfc_b, kc_b), vc = _hgrn2_inputs(qc_, ffc, fbc, ic, lb_f, lb_b)
    oc_f, sc_f = _chunk_scan(qk, lfc_f, kc_f, vc, zeros)
    oc_b, sc_b = _chunk_scan(flip(qk), flip(lfc_b), flip(kc_b), flip(vc), zeros)
    ql, (lf_f, k_f), (lf_b, k_b), vl = _hgrn2_inputs(q, ff, fb, i, lb_f, lb_b)
    o_f, _ = _chunk_scan(ql, lf_f, k_f, vl, sc_f)
    o_b, _ = _chunk_scan(flip(ql), flip(lf_b), flip(k_b), flip(vl), sc_b)
    a_lat = _hgrn2_out(o_f + flip(o_b), g, norm_w, h.dtype)
    b_lat = _short_conv(u, gb, gc, conv_w)
    y = jnp.concatenate([a_lat, b_lat], axis=-1) @ w_out
    if not ctx_out:
        return y, None
    a_ctx = _hgrn2_out(oc_f + flip(oc_b), gcx, norm_w, hc.dtype)
    b_ctx = _short_conv(uc, gbc, gcc, conv_w)
    y_ctx = jnp.concatenate([a_ctx, b_ctx], axis=-1) @ w_out
    return y, y_ctx


def _mixer_fourier_chunk(h, on_grid, w_in, c_norm_w, d_norm_w, d_norm_b, sp_w, sp_b, w_out):
    bn, t, _ = h.shape
    zc, u, v = _split(h @ w_in, [C_WIDTH, D_WIDTH, D_WIDTH])
    zf = _group_rms(zc, c_norm_w, C_GROUPS).astype(jnp.float32)
    if on_grid:
        rows = t // GRID_W
        zf = zf.reshape(bn, rows, GRID_W, C_GROUPS, C_GROUP)
        axes = (1, 2, 4)
    else:
        zf = zf.reshape(bn, t, C_GROUPS, C_GROUP)
        axes = (1, 3)
    c_out = jnp.fft.fftn(zf, axes=axes, norm='ortho').real.reshape(bn, t, C_WIDTH).astype(h.dtype)
    u = jax.nn.gelu(u, approximate=False)
    v = _group_ln(jax.nn.gelu(v, approximate=False), d_norm_w, d_norm_b, D_GROUPS)
    vc = v.reshape(bn, t // D_CHUNK, D_CHUNK, D_GROUPS, D_GDIM)
    sv = jnp.einsum('gpq,bnqgc->bnpgc', sp_w, vc) + sp_b.T[None, None, :, :, None]
    d_out = u * sv.reshape(bn, t, D_WIDTH)
    return jnp.concatenate([c_out, d_out], axis=-1) @ w_out


def _moe(h, router_w, router_b, w1, b1, w2, b2):
    n, d = h.shape
    m = n * TOP_K
    logits = (h @ router_w).astype(jnp.float32) + router_b.astype(jnp.float32)
    top_v, top_i = lax.top_k(logits, TOP_K)
    gates = jax.nn.softmax(top_v, axis=-1)
    flat_e = top_i.reshape(-1).astype(jnp.int32)
    order = jnp.argsort(flat_e).astype(jnp.int32)
    sorted_e = flat_e[order]
    counts = jnp.bincount(flat_e, length=N_EXPERTS).astype(jnp.int32)
    starts = jnp.cumsum(counts) - counts
    padded = (counts + MOE_BLOCK - 1) // MOE_BLOCK * MOE_BLOCK
    pad_ends = jnp.cumsum(padded)
    pad_starts = pad_ends - padded
    dest = pad_starts[sorted_e] + (jnp.arange(m, dtype=jnp.int32) - starts[sorted_e])
    n_blocks = (m + N_EXPERTS * (MOE_BLOCK - 1)) // MOE_BLOCK + 1
    p = n_blocks * MOE_BLOCK
    slot = jnp.full((p,), m, jnp.int32).at[dest].set(order)
    tok = jnp.where(slot < m, slot // TOP_K, n)
    slot_gate = jnp.concatenate([gates.reshape(-1), jnp.zeros((1,), jnp.float32)])[slot]
    block_starts = jnp.arange(n_blocks, dtype=jnp.int32) * MOE_BLOCK
    block_expert = jnp.minimum(jnp.sum(block_starts[:, None] >= pad_ends[None, :], axis=1),
                               N_EXPERTS - 1).astype(jnp.int32)
    h_pad = jnp.concatenate([h, jnp.zeros((1, d), h.dtype)], axis=0)

    def expert_block(args):
        tok_b, e = args
        a = h_pad[tok_b] @ w1[e] + b1[e]
        glu = jnp.minimum(a[:, 0::2], SWIGLU_LIMIT)
        lin = jnp.clip(a[:, 1::2], -SWIGLU_LIMIT, SWIGLU_LIMIT)
        act = glu * jax.nn.sigmoid(SWIGLU_ALPHA * glu) * (lin + 1.0)
        return act @ w2[e] + b2[e]

    y_blocks = lax.map(expert_block, (tok.reshape(n_blocks, MOE_BLOCK), block_expert))
    y = jnp.zeros((n + 1, d), h.dtype).at[tok].add(
        y_blocks.reshape(p, d) * slot_gate[:, None].astype(h.dtype))
    return y[:n]


def setup_inputs(seed: int = 0) -> dict:
    key = jax.random.key(seed)
    ks = jax.random.split(key, 32)
    nrm = lambda k, shape, s: jax.random.normal(k, shape, jnp.float32) * s
    D = D_MODEL
    return {
        "x": nrm(ks[0], (BATCH, SEQ, D), 1.0),
        "c": nrm(ks[1], (BATCH, D), 1.0),
        "ctx": nrm(ks[2], (BATCH, CTX_LEN, D), 1.0),
        "c_ctx": nrm(ks[3], (D,), 1.0),
        "lower_bounds_fwd": nrm(ks[4], (DEPTH + 1, A_KW), 0.1),
        "lower_bounds_bwd": nrm(ks[5], (DEPTH + 1, A_KW), 0.1),
        "ada_w": nrm(ks[6], (DEPTH, D, 6 * D), 0.5 * D ** -0.5),
        "ada_b": nrm(ks[7], (DEPTH, 6 * D), 0.02),
        "w_in_even": nrm(ks[8], (N_EVEN, D, EVEN_IN), D ** -0.5),
        "a_norm_w": 1.0 + nrm(ks[9], (N_EVEN, A_WIDTH), 0.02),
        "conv_w": nrm(ks[10], (N_EVEN, CONV_W, B_WIDTH), CONV_W ** -0.5),
        "w_in_odd": nrm(ks[11], (N_ODD, D, ODD_IN), D ** -0.5),
        "c_norm_w": 1.0 + nrm(ks[12], (N_ODD, C_WIDTH), 0.02),
        "d_norm_w": 1.0 + nrm(ks[13], (N_ODD, D_WIDTH), 0.02),
        "d_norm_b": nrm(ks[14], (N_ODD, D_WIDTH), 0.02),
        "spatial_w": nrm(ks[15], (N_ODD, D_GROUPS, D_CHUNK, D_CHUNK), D_CHUNK ** -0.5),
        "spatial_b": 1.0 + nrm(ks[16], (N_ODD, D_GROUPS, D_CHUNK), 0.02),
        "w_out": nrm(ks[17], (DEPTH, D, D), DN_BETA * D ** -0.5),
        "ln_mix_w": 1.0 + nrm(ks[18], (DEPTH, D), 0.02),
        "ln_mix_b": nrm(ks[19], (DEPTH, D), 0.02),
        "ln_ffn_w": 1.0 + nrm(ks[20], (DEPTH, D), 0.02),
        "ln_ffn_b": nrm(ks[21], (DEPTH, D), 0.02),
        "router_w": nrm(ks[22], (DEPTH, D, N_EXPERTS), D ** -0.5),
        "router_b": nrm(ks[23], (DEPTH, N_EXPERTS), 0.01),
        "moe_w1": nrm(ks[24], (DEPTH, N_EXPERTS, D, 2 * D_EXPERT), D ** -0.5),
        "moe_b1": nrm(ks[25], (DEPTH, N_EXPERTS, 2 * D_EXPERT), 0.02),
        "moe_w2": nrm(ks[26], (DEPTH, N_EXPERTS, D_EXPERT, D), DN_BETA * D_EXPERT ** -0.5),
        "moe_b2": nrm(ks[27], (DEPTH, N_EXPERTS, D), 0.02),
    }


def reference(x, c, ctx, c_ctx, lower_bounds_fwd, lower_bounds_bwd, ada_w, ada_b, w_in_even,
              a_norm_w, conv_w, w_in_odd, c_norm_w, d_norm_w, d_norm_b, spatial_w, spatial_b,
              w_out, ln_mix_w, ln_mix_b, ln_ffn_w, ln_ffn_b, router_w, router_b, moe_w1, moe_b1,
              moe_w2, moe_b2):
    bn, t, d = x.shape
    lb_f_all = jnp.cumsum(jax.nn.softmax(lower_bounds_fwd.astype(jnp.float32), axis=0), axis=0)
    lb_b_all = jnp.cumsum(jax.nn.softmax(lower_bounds_bwd.astype(jnp.float32), axis=0), axis=0)
    s_lat = jax.nn.silu(c)
    s_ctx = jax.nn.silu(c_ctx)
    h_ctx = ctx
    for l in range(DEPTH):
        last = l == DEPTH - 1
        sh1, sc1, g1, sh2, sc2, g2 = jnp.split((s_lat @ ada_w[l] + ada_b[l])[:, None, :], 6, axis=-1)
        csh1, csc1, cg1, csh2, csc2, cg2 = jnp.split(s_ctx @ ada_w[l] + ada_b[l], 6, axis=-1)
        hx = x * (1.0 + sc1) + sh1
        hc = h_ctx * (1.0 + csc1) + csh1
        if l % 2 == 0:
            e = l // 2
            y, y_ctx = _mixer_recurrent_conv(hx, hc, w_in_even[e], lb_f_all[l], lb_b_all[l],
                                             a_norm_w[e], conv_w[e], w_out[l], not last)
        else:
            o = l // 2
            y = _mixer_fourier_chunk(hx, True, w_in_odd[o], c_norm_w[o], d_norm_w[o], d_norm_b[o],
                                     spatial_w[o], spatial_b[o], w_out[l])
            y_ctx = None if last else _mixer_fourier_chunk(
                hc, False, w_in_odd[o], c_norm_w[o], d_norm_w[o], d_norm_b[o],
                spatial_w[o], spatial_b[o], w_out[l])
        x = _layer_norm(DN_ALPHA * x + g1 * y, ln_mix_w[l], ln_mix_b[l])
        hx2 = (x * (1.0 + sc2) + sh2).reshape(-1, d)
        if last:
            tokens = hx2
        else:
            h_ctx = _layer_norm(DN_ALPHA * h_ctx + cg1 * y_ctx, ln_mix_w[l], ln_mix_b[l])
            hc2 = (h_ctx * (1.0 + csc2) + csh2).reshape(-1, d)
            tokens = jnp.concatenate([hx2, hc2], axis=0)
        yt = _moe(tokens, router_w[l], router_b[l], moe_w1[l], moe_b1[l], moe_w2[l], moe_b2[l])
        x = _layer_norm(DN_ALPHA * x + g2 * yt[:bn * t].reshape(bn, t, d), ln_ffn_w[l], ln_ffn_b[l])
        if not last:
            y2c = yt[bn * t:].reshape(h_ctx.shape)
            h_ctx = _layer_norm(DN_ALPHA * h_ctx + cg2 * y2c, ln_ffn_w[l], ln_ffn_b[l])
    return x
```

```python
import functools
import math

import numpy as np
import jax
import jax.numpy as jnp
from jax import lax
from jax.experimental import pallas as pl
from jax.experimental.pallas import tpu as pltpu

F32 = jnp.float32
BF16 = jnp.bfloat16

D_MODEL = 1024
DEPTH = 2
GRID_W = 64
A_HEADS = 4
A_DK = 128
A_DV = 128
A_KW = A_HEADS * A_DK
A_WIDTH = A_HEADS * A_DV
B_WIDTH = D_MODEL - A_WIDTH
CONV_W = 3
C_GROUPS = 4
C_GROUP = 128
C_WIDTH = C_GROUPS * C_GROUP
D_WIDTH = D_MODEL - C_WIDTH
D_GROUPS = 4
D_GDIM = D_WIDTH // D_GROUPS
D_CHUNK = 128
SCAN_CHUNK = 32
EVEN_IN = 3 * A_KW + 2 * A_WIDTH + 3 * B_WIDTH
ODD_IN = C_WIDTH + 2 * D_WIDTH
N_EXPERTS = 32
TOP_K = 4
D_EXPERT = 1024
SWIGLU_ALPHA = 1.702
SWIGLU_LIMIT = 7.0
DN_ALPHA = (2 * DEPTH) ** 0.25
EPS = 1e-5

MOD_ROWS = 8
SCAN_ROWS = 256
ROW_TILE = 256
MOE_ROWS = 256
ROUTE_TILE = 512
COMBINE_TILE = 128
HALO = 16

_NT = (((1,), (1,)), ((), ()))
_TN = (((0,), (0,)), ((), ()))


def _sigmoid(a):
    return 1.0 / (1.0 + jnp.exp(-a))


def _silu(a):
    return a * _sigmoid(a)


def _gelu(a):
    return 0.5 * a * (1.0 + lax.erf(a * (1.0 / math.sqrt(2.0))))


def _params(*sem):
    return pltpu.CompilerParams(dimension_semantics=sem, vmem_limit_bytes=56 * 1024 * 1024)


def _ada_kernel(c_ref, w_ref, b_ref, o_ref):
    s = _silu(c_ref[...])
    o_ref[0] = jnp.dot(s, w_ref[0], preferred_element_type=F32, precision=lax.Precision.HIGHEST) + b_ref[0]


def _ada(cond, ada_w, ada_b):
    d = cond.shape[1]
    n6 = ada_w.shape[2]
    tn = 1536
    return pl.pallas_call(
        _ada_kernel,
        grid=(DEPTH, n6 // tn),
        in_specs=[pl.BlockSpec((MOD_ROWS, d), lambda l, n: (0, 0)),
                  pl.BlockSpec((1, d, tn), lambda l, n: (l, 0, n)),
                  pl.BlockSpec((1, 1, tn), lambda l, n: (l, 0, n))],
        out_specs=pl.BlockSpec((1, MOD_ROWS, tn), lambda l, n: (l, 0, n)),
        out_shape=jax.ShapeDtypeStruct((DEPTH, MOD_ROWS, n6), F32),
        compiler_params=_params("parallel", "parallel"),
        name="ada_mod",
    )(cond, ada_w, ada_b.reshape(DEPTH, 1, n6))


def _inproj_kernel(x_ref, sc_ref, sh_ref, w_ref, o_ref, *, tn):
    h = (x_ref[0] * (1.0 + sc_ref[0]) + sh_ref[0]).astype(BF16)
    for n in range(o_ref.shape[2] // tn):
        o_ref[0, :, n * tn:(n + 1) * tn] = jnp.dot(
            h, w_ref[:, n * tn:(n + 1) * tn], preferred_element_type=F32).astype(o_ref.dtype)


def _inproj(x, sc, sh, w, mod_row, bm):
    bn, t, d = x.shape
    nout = w.shape[1]
    mod_spec = pl.BlockSpec((1, 1, d), lambda b, i: (mod_row(b), 0, 0))
    return pl.pallas_call(
        functools.partial(_inproj_kernel, tn=512),
        grid=(bn, t // bm),
        in_specs=[pl.BlockSpec((1, bm, d), lambda b, i: (b, i, 0)), mod_spec, mod_spec,
                  pl.BlockSpec((d, nout), lambda b, i: (0, 0))],
        out_specs=pl.BlockSpec((1, bm, nout), lambda b, i: (b, i, 0)),
        out_shape=jax.ShapeDtypeStruct((bn, t, nout), BF16),
        compiler_params=_params("parallel", "parallel"),
        name="in_proj",
    )(x, sc, sh, w)


def _scan_kernel(*refs, reverse, finalize):
    if finalize:
        (q_ref, z_ref, v_ref, lb_ref, s0_ref, g_ref, op_ref, nw_ref, o_ref, sfin_ref, st_ref, oacc_ref) = refs
    else:
        (q_ref, z_ref, v_ref, lb_ref, s0_ref, o_ref, sfin_ref, st_ref, oacc_ref) = refs
    rows = q_ref.shape[1]
    nchunk = rows // SCAN_CHUNK

    @pl.when(pl.program_id(2) == 0)
    def _():
        st_ref[...] = s0_ref[0, 0]

    q = _silu(q_ref[0].astype(F32))
    lb = lb_ref[...]
    f = lb + (1.0 - lb) * _sigmoid(z_ref[0].astype(F32))
    logf = jnp.log(f)
    k = 1.0 - f
    v = v_ref[0]

    ri = lax.broadcasted_iota(jnp.int32, (rows, rows), 0)
    ci = lax.broadcasted_iota(jnp.int32, (rows, rows), 1)
    same = (ri // SCAN_CHUNK) == (ci // SCAN_CHUNK)
    tri = jnp.where(same & ((ci >= ri) if reverse else (ci <= ri)), 1.0, 0.0).astype(BF16)
    hi = logf.astype(BF16)
    lo = (logf - hi.astype(F32)).astype(BF16)
    b = jnp.dot(tri, hi, preferred_element_type=F32) + jnp.dot(tri, lo, preferred_element_type=F32)

    q_dec = (q * jnp.exp(b)).astype(BF16)
    k_inv = (k * jnp.exp(-b)).astype(BF16)
    cr = lax.broadcasted_iota(jnp.int32, (SCAN_CHUNK, SCAN_CHUNK), 0)
    cc = lax.broadcasted_iota(jnp.int32, (SCAN_CHUNK, SCAN_CHUNK), 1)
    causal = (cc >= cr) if reverse else (cc <= cr)

    st = st_ref[...]
    order = range(nchunk - 1, -1, -1) if reverse else range(nchunk)
    for c in order:
        lo_r, hi_r = c * SCAN_CHUNK, (c + 1) * SCAN_CHUNK
        tot_r = lo_r if reverse else hi_r - 1
        b_tot = b[tot_r:tot_r + 1, :]
        qd = q_dec[lo_r:hi_r]
        ke = (k[lo_r:hi_r] * jnp.exp(b_tot - b[lo_r:hi_r])).astype(BF16)
        vc = v[lo_r:hi_r]
        scores = lax.dot_general(qd, k_inv[lo_r:hi_r], _NT, preferred_element_type=F32)
        scores = jnp.where(causal, scores, 0.0).astype(BF16)
        o_c = (jnp.dot(scores, vc, preferred_element_type=F32)
               + lax.dot_general(qd, st.astype(BF16), _NT, preferred_element_type=F32))
        oacc_ref[lo_r:hi_r, :] = o_c
        st = st * jnp.exp(b_tot) + lax.dot_general(vc, ke, _TN, preferred_element_type=F32)
    st_ref[...] = st
    sfin_ref[0, 0] = st

    o = oacc_ref[...]
    if finalize:
        o = o + op_ref[0]
        o = o * lax.rsqrt(jnp.mean(o * o, axis=-1, keepdims=True) + EPS)
        o = o * nw_ref[...] * _silu(g_ref[0].astype(F32))
    o_ref[0] = o.astype(o_ref.dtype)


def _scan(zin, lb, s0, *, reverse, o_prev=None, norm_w=None):
    bn, t, _ = zin.shape
    finalize = o_prev is not None
    rows = min(SCAN_ROWS, t)
    nb = t // rows
    pos = (lambda n: nb - 1 - n) if reverse else (lambda n: n)
    hw = A_DK

    def col(section):
        return pl.BlockSpec((1, rows, hw), lambda b, h, n: (b, pos(n), section * A_HEADS + h))

    in_specs = [col(0), col(2 if reverse else 1), col(3),
                pl.BlockSpec((1, hw), lambda b, h, n: (0, h)),
                pl.BlockSpec((1, 1, A_DV, A_DK), lambda b, h, n: (b, h, 0, 0))]
    args = [zin, zin, zin, lb.reshape(1, A_KW), s0]
    if finalize:
        in_specs += [col(4),
                     pl.BlockSpec((1, rows, A_DV), lambda b, h, n: (b, pos(n), h)),
                     pl.BlockSpec((1, A_DV), lambda b, h, n: (0, h))]
        args += [zin, o_prev, norm_w.reshape(1, A_WIDTH)]
    return pl.pallas_call(
        functools.partial(_scan_kernel, reverse=reverse, finalize=finalize),
        grid=(bn, A_HEADS, nb),
        in_specs=in_specs,
        out_specs=[pl.BlockSpec((1, rows, A_DV), lambda b, h, n: (b, pos(n), h)),
                   pl.BlockSpec((1, 1, A_DV, A_DK), lambda b, h, n: (b, h, 0, 0))],
        out_shape=[jax.ShapeDtypeStruct((bn, t, A_WIDTH), BF16 if finalize else F32),
                   jax.ShapeDtypeStruct((bn, A_HEADS, A_DV, A_DK), F32)],
        scratch_shapes=[pltpu.VMEM((A_DV, A_DK), F32), pltpu.VMEM((rows, A_DV), F32)],
        compiler_params=_params("parallel", "parallel", "arbitrary"),
        name="hgrn2_scan_bwd" if reverse else "hgrn2_scan_fwd",
    )(*args)


def _post_mix(acc, x_ref, g1_ref, lnw_ref, lnb_ref, sc2_ref, sh2_ref, rw_ref, rb_ref,
              x1_ref, h2_ref, ti_ref, tg_ref):
    r = DN_ALPHA * x_ref[0] + g1_ref[0] * acc
    mu = jnp.mean(r, axis=-1, keepdims=True)
    rc = r - mu
    var = jnp.mean(rc * rc, axis=-1, keepdims=True)
    x1 = rc * lax.rsqrt(var + EPS) * lnw_ref[...] + lnb_ref[...]
    x1_ref[0] = x1
    h2 = x1 * (1.0 + sc2_ref[0]) + sh2_ref[0]
    h2_ref[...] = h2
    logits = lax.dot_general(rw_ref[...], h2.astype(BF16), _NT, preferred_element_type=F32) + rb_ref[...]
    iota = lax.broadcasted_iota(jnp.int32, logits.shape, 0)
    vals, idxs = [], []
    cur = logits
    for _ in range(TOP_K):
        m = jnp.max(cur, axis=0, keepdims=True)
        ik = jnp.min(jnp.where(cur == m, iota, N_EXPERTS), axis=0, keepdims=True)
        vals.append(m)
        idxs.append(ik)
        cur = jnp.where(iota == ik, -jnp.inf, cur)
    tv = jnp.concatenate(vals, axis=0)
    e = jnp.exp(tv - tv[0:1])
    tg_ref[...] = e / jnp.sum(e, axis=0, keepdims=True)
    ti_ref[...] = jnp.concatenate(idxs, axis=0)


def _outproj_even_kernel(a_ref, u_ref, gb_ref, gc_ref, up_ref, gcp_ref, un_ref, gcn_ref, cw_ref, wo_ref,
                         x_ref, g1_ref, lnw_ref, lnb_ref, sc2_ref, sh2_ref, rw_ref, rb_ref, alias_ref,
                         x1_ref, h2_ref, ti_ref, tg_ref):
    del alias_ref
    i = pl.program_id(1)
    bm = u_ref.shape[1]
    z = gc_ref[0].astype(F32) * u_ref[0].astype(F32)
    z_before = gcp_ref[0, HALO - 1:HALO, :].astype(F32) * up_ref[0, HALO - 1:HALO, :].astype(F32)
    z_after = gcn_ref[0, 0:1, :].astype(F32) * un_ref[0, 0:1, :].astype(F32)
    z_before = jnp.where(i == 0, 0.0, z_before)
    z_after = jnp.where(i == pl.num_programs(1) - 1, 0.0, z_after)
    row = lax.broadcasted_iota(jnp.int32, z.shape, 0)
    z_prev = jnp.where(row == 0, z_before, pltpu.roll(z, 1, axis=0))
    z_next = jnp.where(row == bm - 1, z_after, pltpu.roll(z, bm - 1, axis=0))
    zc = cw_ref[0:1, :] * z_prev + cw_ref[1:2, :] * z + cw_ref[2:3, :] * z_next
    b_mix = (gb_ref[0].astype(F32) * zc).astype(BF16)
    acc = (jnp.dot(a_ref[0], wo_ref[0:A_WIDTH, :], preferred_element_type=F32)
           + jnp.dot(b_mix, wo_ref[A_WIDTH:, :], preferred_element_type=F32))
    _post_mix(acc, x_ref, g1_ref, lnw_ref, lnb_ref, sc2_ref, sh2_ref, rw_ref, rb_ref,
              x1_ref, h2_ref, ti_ref, tg_ref)


def _outproj_odd_kernel(c_ref, u_ref, v_ref, dw_ref, db_ref, spw_ref, spb_ref, wo_ref,
                        x_ref, g1_ref, lnw_ref, lnb_ref, sc2_ref, sh2_ref, rw_ref, rb_ref, alias_ref,
                        x1_ref, h2_ref, ti_ref, tg_ref):
    del alias_ref
    bm = u_ref.shape[1]
    u = _gelu(u_ref[0].astype(F32))
    v = _gelu(v_ref[0].astype(F32))
    cols = []
    for g in range(D_GROUPS):
        lo_c, hi_c = g * D_GDIM, (g + 1) * D_GDIM
        vg = v[:, lo_c:hi_c]
        mu = jnp.mean(vg, axis=-1, keepdims=True)
        vc = vg - mu
        var = jnp.mean(vc * vc, axis=-1, keepdims=True)
        vn = (vc * lax.rsqrt(var + EPS) * dw_ref[:, lo_c:hi_c] + db_ref[:, lo_c:hi_c]).astype(BF16)
        parts = []
        for ch in range(bm // D_CHUNK):
            sv = jnp.dot(spw_ref[g], vn[ch * D_CHUNK:(ch + 1) * D_CHUNK], preferred_element_type=F32)
            parts.append(sv + spb_ref[:, g:g + 1])
        cols.append(u[:, lo_c:hi_c] * jnp.concatenate(parts, axis=0))
    d_mix = jnp.concatenate(cols, axis=1).astype(BF16)
    acc = (jnp.dot(c_ref[0].astype(BF16), wo_ref[0:C_WIDTH, :], preferred_element_type=F32)
           + jnp.dot(d_mix, wo_ref[C_WIDTH:, :], preferred_element_type=F32))
    _post_mix(acc, x_ref, g1_ref, lnw_ref, lnb_ref, sc2_ref, sh2_ref, rw_ref, rb_ref,
              x1_ref, h2_ref, ti_ref, tg_ref)


def _outproj(kind, mixer_args, mixer_specs, wo, x, mods, mod_row, lnw, lnb, rw_t, rb, tok_state, tok_off, bm):
    bn, t, d = x.shape
    nt = t // bm
    g1, sc2, sh2 = mods
    h2_all, ti_all, tg_all = tok_state
    ntok = h2_all.shape[0]
    off = tok_off // bm
    mod_spec = pl.BlockSpec((1, 1, d), lambda b, i: (mod_row(b), 0, 0))
    vec_spec = pl.BlockSpec((1, d), lambda b, i: (0, 0))
    in_specs = list(mixer_specs) + [
        pl.BlockSpec((d, d), lambda b, i: (0, 0)),
        pl.BlockSpec((1, bm, d), lambda b, i: (b, i, 0)),
        mod_spec, vec_spec, vec_spec, mod_spec, mod_spec,
        pl.BlockSpec((N_EXPERTS, d), lambda b, i: (0, 0)),
        pl.BlockSpec((N_EXPERTS, 1), lambda b, i: (0, 0)),
        pl.BlockSpec(memory_space=pl.ANY), pl.BlockSpec(memory_space=pl.ANY), pl.BlockSpec(memory_space=pl.ANY)]
    n_in = len(in_specs)
    kernel = _outproj_even_kernel if kind == "even" else _outproj_odd_kernel

    def body(*refs):
        ins, outs = refs[:n_in], refs[n_in:]
        kernel(*ins[:n_in - 3], ins[n_in - 3:], *outs)

    return pl.pallas_call(
        body,
        grid=(bn, nt),
        in_specs=in_specs,
        out_specs=[pl.BlockSpec((1, bm, d), lambda b, i: (b, i, 0)),
                   pl.BlockSpec((bm, d), lambda b, i: (off + b * nt + i, 0)),
                   pl.BlockSpec((TOP_K, bm), lambda b, i: (0, off + b * nt + i)),
                   pl.BlockSpec((TOP_K, bm), lambda b, i: (0, off + b * nt + i))],
        out_shape=[jax.ShapeDtypeStruct((bn, t, d), F32),
                   jax.ShapeDtypeStruct((ntok, d), F32),
                   jax.ShapeDtypeStruct((TOP_K, ntok), jnp.int32),
                   jax.ShapeDtypeStruct((TOP_K, ntok), F32)],
        input_output_aliases={n_in - 3: 1, n_in - 2: 2, n_in - 1: 3},
        compiler_params=_params("parallel", "parallel"),
        name="out_proj_" + kind,
    )(*mixer_args, wo, x, g1, lnw.reshape(1, d), lnb.reshape(1, d), sc2, sh2, rw_t, rb.reshape(N_EXPERTS, 1),
      h2_all, ti_all, tg_all)


def _outproj_even(a_mix, zin, conv_w, **kw):
    bm = kw["bm"]
    t = zin.shape[1]
    hb = bm // HALO
    last = t // HALO - 1
    cur = lambda sec: pl.BlockSpec((1, bm, B_WIDTH), lambda b, i: (b, i, sec))
    before = lambda sec: pl.BlockSpec((1, HALO, B_WIDTH), lambda b, i: (b, jnp.maximum(i * hb - 1, 0), sec))
    after = lambda sec: pl.BlockSpec((1, HALO, B_WIDTH), lambda b, i: (b, jnp.minimum((i + 1) * hb, last), sec))
    specs = [pl.BlockSpec((1, bm, A_WIDTH), lambda b, i: (b, i, 0)), cur(5), cur(6), cur(7),
             before(5), before(7), after(5), after(7),
             pl.BlockSpec((CONV_W, B_WIDTH), lambda b, i: (0, 0))]
    return _outproj("even", [a_mix, zin, zin, zin, zin, zin, zin, zin, conv_w], specs, **kw)


def _outproj_odd(c_mix, zin, d_norm_w, d_norm_b, sp_w, sp_b, **kw):
    bm = kw["bm"]
    cur = lambda sec: pl.BlockSpec((1, bm, D_WIDTH), lambda b, i: (b, i, sec))
    specs = [pl.BlockSpec((1, bm, C_WIDTH), lambda b, i: (b, i, 0)), cur(1), cur(2),
             pl.BlockSpec((1, D_WIDTH), lambda b, i: (0, 0)), pl.BlockSpec((1, D_WIDTH), lambda b, i: (0, 0)),
             pl.BlockSpec((D_GROUPS, D_CHUNK, D_CHUNK), lambda b, i: (0, 0, 0)),
             pl.BlockSpec((D_CHUNK, D_GROUPS), lambda b, i: (0, 0))]
    return _outproj("odd", [c_mix, zin, zin, d_norm_w.reshape(1, D_WIDTH), d_norm_b.reshape(1, D_WIDTH),
                            sp_w.astype(BF16), sp_b.T], specs, **kw)


def _dft_mats(n):
    ang = 2.0 * np.pi * np.outer(np.arange(n), np.arange(n)) / n
    return np.cos(ang), np.sin(ang)


def _fourier_cw_kernel(z_ref, nw_ref, cs_ref, kc_ref, ks_ref, p_ref, q_ref, ps_ref, qs_ref):
    tile = z_ref.shape[1]
    half = 256
    grows = tile // GRID_W
    for hs in range(tile // half):
        z = z_ref[0, hs * half:(hs + 1) * half, :].astype(F32)
        a_parts, b_parts = [], []
        for g in range(C_GROUPS):
            zg = z[:, g * C_GROUP:(g + 1) * C_GROUP]
            zg = zg * lax.rsqrt(jnp.mean(zg * zg, axis=-1, keepdims=True) + EPS) * nw_ref[:, g * C_GROUP:(g + 1) * C_GROUP]
            ab = jnp.dot(zg.astype(BF16), cs_ref[...], preferred_element_type=F32)
            a_parts.append(ab[:, :C_GROUP])
            b_parts.append(ab[:, C_GROUP:])
        ab = jnp.concatenate(a_parts + b_parts, axis=1).astype(BF16)
        m1 = jnp.dot(kc_ref[...], ab, preferred_element_type=F32)
        m2 = jnp.dot(ks_ref[...], ab, preferred_element_type=F32)
        pv = m1[:, :C_WIDTH] - m2[:, C_WIDTH:]
        qv = m2[:, :C_WIDTH] + m1[:, C_WIDTH:]
        for g in range(C_GROUPS):
            ps_ref[g, hs * half:(hs + 1) * half, :] = pv[:, g * C_GROUP:(g + 1) * C_GROUP]
            qs_ref[g, hs * half:(hs + 1) * half, :] = qv[:, g * C_GROUP:(g + 1) * C_GROUP]
    for k2 in range(GRID_W):
        for g in range(C_GROUPS):
            p_ref[0, k2, :, g * C_GROUP:(g + 1) * C_GROUP] = ps_ref[g, pl.ds(k2, grows, stride=GRID_W), :]
            q_ref[0, k2, :, g * C_GROUP:(g + 1) * C_GROUP] = qs_ref[g, pl.ds(k2, grows, stride=GRID_W), :]


def _fourier_r_kernel(p_ref, q_ref, c_ref, s_ref, o_ref, ys_ref):
    nj = p_ref.shape[1]
    rows = p_ref.shape[2]
    for j in range(nj):
        y = (jnp.dot(c_ref[...], p_ref[0, j].astype(BF16), preferred_element_type=F32)
             + jnp.dot(s_ref[...], q_ref[0, j].astype(BF16), preferred_element_type=F32))
        for g in range(C_GROUPS):
            ys_ref[g, j * rows:(j + 1) * rows, :] = y[:, g * C_GROUP:(g + 1) * C_GROUP]

    def put(k1, carry):
        for g in range(C_GROUPS):
            o_ref[0, k1, :, g * C_GROUP:(g + 1) * C_GROUP] = ys_ref[g, pl.ds(k1, nj, stride=rows), :]
        return carry

    lax.fori_loop(0, rows, put, 0)


def _fourier_mix(zin, c_norm_w):
    bn, t, _ = zin.shape
    rows = t // GRID_W
    tile = 512
    grows = tile // GRID_W
    c3, s3 = _dft_mats(C_GROUP)
    c2, s2 = _dft_mats(GRID_W)
    c1, s1 = _dft_mats(rows)
    scale = 1.0 / math.sqrt(rows * GRID_W * C_GROUP)
    cs3 = jnp.asarray(np.concatenate([c3, s3], axis=1), BF16)
    eye = np.eye(256 // GRID_W)
    kc2 = jnp.asarray(np.kron(eye, c2), BF16)
    ks2 = jnp.asarray(np.kron(eye, s2), BF16)
    c1s = jnp.asarray(c1 * scale, BF16)
    s1s = jnp.asarray(-s1 * scale, BF16)
    pq_shape = jax.ShapeDtypeStruct((bn, GRID_W, rows, C_WIDTH), F32)
    pq_spec = pl.BlockSpec((1, GRID_W, grows, C_WIDTH), lambda b, i: (b, 0, i, 0))
    p, q = pl.pallas_call(
        _fourier_cw_kernel,
        grid=(bn, t // tile),
        in_specs=[pl.BlockSpec((1, tile, C_WIDTH), lambda b, i: (b, i, 0)),
                  pl.BlockSpec((1, C_WIDTH), lambda b, i: (0, 0)),
                  pl.BlockSpec((C_GROUP, 2 * C_GROUP), lambda b, i: (0, 0)),
                  pl.BlockSpec((256, 256), lambda b, i: (0, 0)),
                  pl.BlockSpec((256, 256), lambda b, i: (0, 0))],
        out_specs=[pq_spec, pq_spec],
        out_shape=[pq_shape, pq_shape],
        scratch_shapes=[pltpu.VMEM((C_GROUPS, tile, C_GROUP), F32), pltpu.VMEM((C_GROUPS, tile, C_GROUP), F32)],
        compiler_params=_params("parallel", "parallel"),
        name="fourier_chan_col",
    )(zin, c_norm_w.reshape(1, C_WIDTH), cs3, kc2, ks2)
    nj = 8
    in_spec = pl.BlockSpec((1, nj, rows, C_WIDTH), lambda b, j: (b, j, 0, 0))
    mat_spec = pl.BlockSpec((rows, rows), lambda b, j: (0, 0))
    y = pl.pallas_call(
        _fourier_r_kernel,
        grid=(bn, GRID_W // nj),
        in_specs=[in_spec, in_spec, mat_spec, mat_spec],
        out_specs=pl.BlockSpec((1, rows, nj, C_WIDTH), lambda b, j: (b, 0, j, 0)),
        out_shape=jax.ShapeDtypeStruct((bn, rows, GRID_W, C_WIDTH), F32),
        scratch_shapes=[pltpu.VMEM((C_GROUPS, nj * rows, C_GROUP), F32)],
        compiler_params=_params("parallel", "parallel"),
        name="fourier_rows",
    )(p, q, c1s, s1s)
    return y.reshape(bn, t, C_WIDTH)


def _rank_kernel(ti_ref, rank_ref, cnt_ref, run_ref):
    @pl.when(pl.program_id(0) == 0)
    def _():
        run_ref[...] = jnp.zeros_like(run_ref)

    ti = ti_ref[...]
    tb = ti.shape[1]
    eio = lax.broadcasted_iota(jnp.int32, (N_EXPERTS, tb), 0)
    hot = [eio == ti[k:k + 1, :] for k in range(TOP_K)]
    occ = sum(h.astype(F32) for h in hot)
    si = lax.broadcasted_iota(jnp.int32, (tb, tb), 0)
    ti_ = lax.broadcasted_iota(jnp.int32, (tb, tb), 1)
    before = jnp.where(si < ti_, 1.0, 0.0).astype(BF16)
    seen = jnp.dot(occ.astype(BF16), before, preferred_element_type=F32) + run_ref[:, 0:1]
    rank_ref[...] = jnp.concatenate(
        [jnp.sum(jnp.where(h, seen, 0.0), axis=0, keepdims=True) for h in hot], axis=0).astype(jnp.int32)
    run_ref[...] = run_ref[...] + jnp.sum(occ, axis=1, keepdims=True)
    cnt_ref[...] = run_ref[...].astype(jnp.int32)


def _ranks(top_i):
    ntok = top_i.shape[1]
    tb = ROUTE_TILE if ntok % ROUTE_TILE == 0 else 256
    rank, cnt = pl.pallas_call(
        _rank_kernel,
        grid=(ntok // tb,),
        in_specs=[pl.BlockSpec((TOP_K, tb), lambda i: (0, i))],
        out_specs=[pl.BlockSpec((TOP_K, tb), lambda i: (0, i)),
                   pl.BlockSpec((N_EXPERTS, 128), lambda i: (0, 0))],
        out_shape=[jax.ShapeDtypeStruct((TOP_K, ntok), jnp.int32),
                   jax.ShapeDtypeStruct((N_EXPERTS, 128), jnp.int32)],
        scratch_shapes=[pltpu.VMEM((N_EXPERTS, 128), F32)],
        compiler_params=_params("arbitrary"),
        name="route_rank",
    )(top_i)
    return rank, cnt[:, 0]


def _expert_kernel(be_ref, tok_ref, tokn_ref, h_hbm, w1g_ref, w1l_ref, w2_ref, b1g_ref, b1l_ref, b2_ref,
                   y_ref, xbuf, sem, *, nb):
    del be_ref
    j = pl.program_id(0)
    rows = xbuf.shape[1]
    slot = j % 2

    def row_copy(tok, buf, r):
        return pltpu.make_async_copy(h_hbm.at[pl.ds(tok, 1), :], xbuf.at[buf, pl.ds(r, 1), :], sem.at[buf])

    def gather(idx_ref, buf):
        def issue(r, carry):
            row_copy(idx_ref[0, 0, r], buf, r).start()
            return carry
        lax.fori_loop(0, rows, issue, 0)

    @pl.when(j == 0)
    def _():
        gather(tok_ref, 0)

    @pl.when(j + 1 < nb)
    def _():
        gather(tokn_ref, 1 - slot)

    def drain(r, carry):
        row_copy(0, slot, r).wait()
        return carry
    lax.fori_loop(0, rows, drain, 0)

    x = xbuf[slot].astype(BF16)
    glu = jnp.minimum(jnp.dot(x, w1g_ref[0], preferred_element_type=F32) + b1g_ref[0], SWIGLU_LIMIT)
    lin = jnp.clip(jnp.dot(x, w1l_ref[0], preferred_element_type=F32) + b1l_ref[0], -SWIGLU_LIMIT, SWIGLU_LIMIT)
    act = glu * _sigmoid(SWIGLU_ALPHA * glu) * (lin + 1.0)
    y_ref[...] = jnp.dot(act.astype(BF16), w2_ref[0], preferred_element_type=F32) + b2_ref[0]


def _experts(h_all, slot_tok, block_expert, w1g, w1l, w2, b1g, b1l, b2):
    ntok, d = h_all.shape
    p = slot_tok.shape[0]
    nb = p // MOE_ROWS
    f = w1g.shape[2]
    tok3 = slot_tok.reshape(nb, 1, MOE_ROWS)
    wspec = lambda k, n: pl.BlockSpec((1, k, n), lambda j, be: (be[j], 0, 0))
    grid_spec = pltpu.PrefetchScalarGridSpec(
        num_scalar_prefetch=1,
        grid=(nb,),
        in_specs=[pl.BlockSpec((1, 1, MOE_ROWS), lambda j, be: (j, 0, 0), memory_space=pltpu.SMEM),
                  pl.BlockSpec((1, 1, MOE_ROWS), lambda j, be: (jnp.minimum(j + 1, nb - 1), 0, 0),
                               memory_space=pltpu.SMEM),
                  pl.BlockSpec(memory_space=pl.ANY),
                  wspec(d, f), wspec(d, f), wspec(f, d), wspec(1, f), wspec(1, f), wspec(1, d)],
        out_specs=pl.BlockSpec((MOE_ROWS, d), lambda j, be: (j, 0)),
        scratch_shapes=[pltpu.VMEM((2, MOE_ROWS, d), F32), pltpu.SemaphoreType.DMA((2,))])
    return pl.pallas_call(
        functools.partial(_expert_kernel, nb=nb),
        grid_spec=grid_spec,
        out_shape=jax.ShapeDtypeStruct((p, d), F32),
        compiler_params=_params("arbitrary"),
        name="moe_experts",
    )(block_expert, tok3, tok3, h_all, w1g, w1l, w2, b1g, b1l, b2)


def _combine_kernel(dst_ref, dstn_ref, y_hbm, x1_ref, gate_ref, g2_ref, lnw_ref, lnb_ref, o_ref, ybuf, sem, *, nt):
    i = pl.program_id(0)
    tb = x1_ref.shape[0]
    slot = i % 2

    def row_copy(src, buf, k, r):
        return pltpu.make_async_copy(y_hbm.at[pl.ds(src, 1), :], ybuf.at[buf, k, pl.ds(r, 1), :], sem.at[buf])

    def gather(idx_ref, buf):
        def issue(r, carry):
            for k in range(TOP_K):
                row_copy(idx_ref[0, k, r], buf, k, r).start()
            return carry
        lax.fori_loop(0, tb, issue, 0)

    @pl.when(i == 0)
    def _():
        gather(dst_ref, 0)

    @pl.when(i + 1 < nt)
    def _():
        gather(dstn_ref, 1 - slot)

    def drain(r, carry):
        for k in range(TOP_K):
            row_copy(0, slot, k, r).wait()
        return carry
    lax.fori_loop(0, tb, drain, 0)

    y = sum(gate_ref[:, k:k + 1] * ybuf[slot, k] for k in range(TOP_K))
    r = DN_ALPHA * x1_ref[...] + g2_ref[0] * y
    mu = jnp.mean(r, axis=-1, keepdims=True)
    rc = r - mu
    var = jnp.mean(rc * rc, axis=-1, keepdims=True)
    o_ref[...] = rc * lax.rsqrt(var + EPS) * lnw_ref[...] + lnb_ref[...]


def _combine(x1, y_sorted, dest, gates_t, tok_off, g2, mod_row, lnw, lnb):
    bn, t, d = x1.shape
    n = bn * t
    tb = COMBINE_TILE
    nt = n // tb
    off = tok_off // tb
    per_b = t // tb
    dest3 = dest.reshape(TOP_K, -1, tb).transpose(1, 0, 2)
    out = pl.pallas_call(
        functools.partial(_combine_kernel, nt=nt),
        grid=(nt,),
        in_specs=[pl.BlockSpec((1, TOP_K, tb), lambda i: (off + i, 0, 0), memory_space=pltpu.SMEM),
                  pl.BlockSpec((1, TOP_K, tb), lambda i: (off + jnp.minimum(i + 1, nt - 1), 0, 0),
                               memory_space=pltpu.SMEM),
                  pl.BlockSpec(memory_space=pl.ANY),
                  pl.BlockSpec((tb, d), lambda i: (i, 0)),
                  pl.BlockSpec((tb, TOP_K), lambda i: (off + i, 0)),
                  pl.BlockSpec((1, 1, d), lambda i: (mod_row(i // per_b), 0, 0)),
                  pl.BlockSpec((1, d), lambda i: (0, 0)),
                  pl.BlockSpec((1, d), lambda i: (0, 0))],
        out_specs=pl.BlockSpec((tb, d), lambda i: (i, 0)),
        out_shape=jax.ShapeDtypeStruct((n, d), F32),
        scratch_shapes=[pltpu.VMEM((2, TOP_K, tb, d), F32), pltpu.SemaphoreType.DMA((2,))],
        compiler_params=_params("arbitrary"),
        name="moe_combine",
    )(dest3, dest3, y_sorted, x1.reshape(n, d), gates_t, g2, lnw.reshape(1, d), lnb.reshape(1, d))
    return out.reshape(bn, t, d)


def _moe_dispatch(top_i):
    ntok = top_i.shape[1]
    m = ntok * TOP_K
    rank, counts = _ranks(top_i)
    padded = (counts + MOE_ROWS - 1) // MOE_ROWS * MOE_ROWS
    pad_ends = jnp.cumsum(padded)
    pad_starts = pad_ends - padded
    dest = pad_starts[top_i] + rank
    nb = (m + N_EXPERTS * (MOE_ROWS - 1) + MOE_ROWS - 1) // MOE_ROWS
    p = nb * MOE_ROWS
    tok_ids = jnp.broadcast_to(jnp.arange(ntok, dtype=jnp.int32)[None, :], (TOP_K, ntok))
    slot_tok = jnp.zeros((p,), jnp.int32).at[dest.reshape(-1)].set(tok_ids.reshape(-1))
    block_starts = jnp.arange(nb, dtype=jnp.int32) * MOE_ROWS
    block_expert = jnp.minimum(jnp.sum(block_starts[:, None] >= pad_ends[None, :], axis=1),
                               N_EXPERTS - 1).astype(jnp.int32)
    return dest, slot_tok, block_expert


def kernel(x, c, ctx, c_ctx, lower_bounds_fwd, lower_bounds_bwd, ada_w, ada_b, w_in_even, a_norm_w, conv_w,
           w_in_odd, c_norm_w, d_norm_w, d_norm_b, spatial_w, spatial_b, w_out, ln_mix_w, ln_mix_b,
           ln_ffn_w, ln_ffn_b, router_w, router_b, moe_w1, moe_b1, moe_w2, moe_b2):
    bn, t, d = x.shape
    tc = ctx.shape[1]
    n_lat = bn * t
    n_ctx = bn * tc
    ctx_row = bn
    lat_row = lambda b: b
    ctx_mod_row = lambda b: ctx_row

    lb_f_all = jnp.cumsum(jax.nn.softmax(lower_bounds_fwd.astype(F32), axis=0), axis=0)
    lb_b_all = jnp.cumsum(jax.nn.softmax(lower_bounds_bwd.astype(F32), axis=0), axis=0)

    cond = jnp.zeros((MOD_ROWS, d), F32).at[:bn].set(c).at[ctx_row].set(c_ctx)
    mods = _ada(cond, ada_w, ada_b).reshape(DEPTH, MOD_ROWS, 6, 1, d)

    h_ctx = ctx
    for l in range(DEPTH):
        last = l == DEPTH - 1
        sh1, sc1, g1, sh2, sc2, g2 = (mods[l, :, j] for j in range(6))
        wo = w_out[l].astype(BF16)
        rw_t = router_w[l].T.astype(BF16)
        ntok = n_lat if last else n_lat + n_ctx
        tok_state = (jnp.zeros((ntok, d), F32), jnp.zeros((TOP_K, ntok), jnp.int32), jnp.zeros((TOP_K, ntok), F32))
        post = dict(wo=wo, lnw=ln_mix_w[l], lnb=ln_mix_b[l], rw_t=rw_t, rb=router_b[l], bm=ROW_TILE)
        if l % 2 == 0:
            e = l // 2
            w_in = w_in_even[e].astype(BF16)
            zin_c = _inproj(h_ctx, sc1, sh1, w_in, ctx_mod_row, bm=min(ROW_TILE, tc))
            zin = _inproj(x, sc1, sh1, w_in, lat_row, bm=ROW_TILE)
            zeros = jnp.zeros((bn, A_HEADS, A_DV, A_DK), F32)
            oc_f, sc_f = _scan(zin_c, lb_f_all[l], zeros, reverse=False)
            a_ctx, sc_b = _scan(zin_c, lb_b_all[l], zeros, reverse=True, o_prev=oc_f, norm_w=a_norm_w[e])
            o_f, _ = _scan(zin, lb_f_all[l], sc_f, reverse=False)
            a_lat, _ = _scan(zin, lb_b_all[l], sc_b, reverse=True, o_prev=o_f, norm_w=a_norm_w[e])
            x1, *tok_state = _outproj_even(a_lat, zin, conv_w[e], x=x, mods=(g1, sc2, sh2), mod_row=lat_row,
                                           tok_state=tok_state, tok_off=0, **post)
            if not last:
                hc1, *tok_state = _outproj_even(a_ctx, zin_c, conv_w[e], x=h_ctx, mods=(g1, sc2, sh2),
                                                mod_row=ctx_mod_row, tok_state=tok_state, tok_off=n_lat, **post)
        else:
            o = l // 2
            w_in = w_in_odd[o].astype(BF16)
            odd = dict(d_norm_w=d_norm_w[o], d_norm_b=d_norm_b[o], sp_w=spatial_w[o], sp_b=spatial_b[o])
            zin = _inproj(x, sc1, sh1, w_in, lat_row, bm=ROW_TILE)
            c_mix = _fourier_mix(zin, c_norm_w[o])
            x1, *tok_state = _outproj_odd(c_mix, zin, x=x, mods=(g1, sc2, sh2), mod_row=lat_row,
                                          tok_state=tok_state, tok_off=0, **odd, **post)
            if not last:
                raise NotImplementedError("an odd layer that is not the last needs the context Fourier mixer")
        h2_all, top_i, top_g = tok_state
        dest, slot_tok, block_expert = _moe_dispatch(top_i)
        w1 = moe_w1[l]
        y_sorted = _experts(
            h2_all, slot_tok, block_expert,
            w1[:, :, 0::2].astype(BF16), w1[:, :, 1::2].astype(BF16), moe_w2[l].astype(BF16),
            moe_b1[l][:, None, 0::2], moe_b1[l][:, None, 1::2], moe_b2[l][:, None, :])
        gates_t = top_g.T
        x = _combine(x1, y_sorted, dest, gates_t, 0, g2, lat_row, ln_ffn_w[l], ln_ffn_b[l])
        if not last:
            h_ctx = _combine(hc1, y_sorted, dest, gates_t, n_lat, g2, ctx_mod_row, ln_ffn_w[l], ln_ffn_b[l])
    return x
```

```python
import functools
import math

import numpy as np
import jax
import jax.numpy as jnp
from jax import lax
from jax.experimental import pallas as pl
from jax.experimental.pallas import tpu as pltpu

F32 = jnp.float32
BF16 = jnp.bfloat16

D_MODEL = 1024
DEPTH = 2
GRID_W = 64
A_HEADS = 4
A_DK = 128
A_DV = 128
A_KW = A_HEADS * A_DK
A_WIDTH = A_HEADS * A_DV
B_WIDTH = D_MODEL - A_WIDTH
CONV_W = 3
C_GROUPS = 4
C_GROUP = 128
C_WIDTH = C_GROUPS * C_GROUP
D_WIDTH = D_MODEL - C_WIDTH
D_GROUPS = 4
D_GDIM = D_WIDTH // D_GROUPS
D_CHUNK = 128
SCAN_CHUNK = 32
EVEN_IN = 3 * A_KW + 2 * A_WIDTH + 3 * B_WIDTH
ODD_IN = C_WIDTH + 2 * D_WIDTH
N_EXPERTS = 32
TOP_K = 4
D_EXPERT = 1024
SWIGLU_ALPHA = 1.702
SWIGLU_LIMIT = 7.0
DN_ALPHA = (2 * DEPTH) ** 0.25
EPS = 1e-5

MOD_ROWS = 8
SCAN_ROWS = 256
ROW_TILE = 256
MOE_ROWS = 256
ROUTE_TILE = 512
COMBINE_TILE = 128
HALO = 16
LANES = 128
ROW_TILES = D_MODEL // LANES

_NT = (((1,), (1,)), ((), ()))
_TN = (((0,), (0,)), ((), ()))


def _sigmoid(a):
    return 1.0 / (1.0 + jnp.exp(-a))


def _silu(a):
    return a * _sigmoid(a)


def _gelu(a):
    return 0.5 * a * (1.0 + lax.erf(a * (1.0 / math.sqrt(2.0))))


def _store_row_tiles(ref, val):
    rows = val.shape[0]
    for c in range(ROW_TILES):
        ref[pl.ds(c, rows, stride=ROW_TILES), :] = val[:, c * LANES:(c + 1) * LANES]


def _load_row_tiles(ref, rows):
    return jnp.concatenate([ref[pl.ds(c, rows, stride=ROW_TILES), :] for c in range(ROW_TILES)], axis=1)


def _params(*sem):
    return pltpu.CompilerParams(dimension_semantics=sem, vmem_limit_bytes=56 * 1024 * 1024)


def _ada_kernel(c_ref, w_ref, b_ref, o_ref):
    s = _silu(c_ref[...])
    o_ref[0] = jnp.dot(s, w_ref[0], preferred_element_type=F32, precision=lax.Precision.HIGHEST) + b_ref[0]


def _ada(cond, ada_w, ada_b):
    d = cond.shape[1]
    n6 = ada_w.shape[2]
    tn = 1536
    return pl.pallas_call(
        _ada_kernel,
        grid=(DEPTH, n6 // tn),
        in_specs=[pl.BlockSpec((MOD_ROWS, d), lambda l, n: (0, 0)),
                  pl.BlockSpec((1, d, tn), lambda l, n: (l, 0, n)),
                  pl.BlockSpec((1, 1, tn), lambda l, n: (l, 0, n))],
        out_specs=pl.BlockSpec((1, MOD_ROWS, tn), lambda l, n: (l, 0, n)),
        out_shape=jax.ShapeDtypeStruct((DEPTH, MOD_ROWS, n6), F32),
        compiler_params=_params("parallel", "parallel"),
        name="ada_mod",
    )(cond, ada_w, ada_b.reshape(DEPTH, 1, n6))


def _inproj_kernel(x_ref, sc_ref, sh_ref, w_ref, o_ref, *, tn):
    h = (x_ref[0] * (1.0 + sc_ref[0]) + sh_ref[0]).astype(BF16)
    for n in range(o_ref.shape[2] // tn):
        o_ref[0, :, n * tn:(n + 1) * tn] = jnp.dot(
            h, w_ref[:, n * tn:(n + 1) * tn], preferred_element_type=F32).astype(o_ref.dtype)


def _inproj(x, sc, sh, w, mod_row, bm):
    bn, t, d = x.shape
    nout = w.shape[1]
    mod_spec = pl.BlockSpec((1, 1, d), lambda b, i: (mod_row(b), 0, 0))
    return pl.pallas_call(
        functools.partial(_inproj_kernel, tn=512),
        grid=(bn, t // bm),
        in_specs=[pl.BlockSpec((1, bm, d), lambda b, i: (b, i, 0)), mod_spec, mod_spec,
                  pl.BlockSpec((d, nout), lambda b, i: (0, 0))],
        out_specs=pl.BlockSpec((1, bm, nout), lambda b, i: (b, i, 0)),
        out_shape=jax.ShapeDtypeStruct((bn, t, nout), BF16),
        compiler_params=_params("parallel", "parallel"),
        name="in_proj",
    )(x, sc, sh, w)


def _scan_kernel(*refs, reverse, finalize):
    if finalize:
        (q_ref, z_ref, v_ref, lb_ref, s0_ref, g_ref, op_ref, nw_ref, o_ref, sfin_ref, st_ref, oacc_ref) = refs
    else:
        (q_ref, z_ref, v_ref, lb_ref, s0_ref, o_ref, sfin_ref, st_ref, oacc_ref) = refs
    rows = q_ref.shape[1]
    nchunk = rows // SCAN_CHUNK

    @pl.when(pl.program_id(2) == 0)
    def _():
        st_ref[...] = s0_ref[0, 0]

    q = _silu(q_ref[0].astype(F32))
    lb = lb_ref[...]
    f = lb + (1.0 - lb) * _sigmoid(z_ref[0].astype(F32))
    logf = jnp.log(f)
    k = 1.0 - f
    v = v_ref[0]

    ri = lax.broadcasted_iota(jnp.int32, (rows, rows), 0)
    ci = lax.broadcasted_iota(jnp.int32, (rows, rows), 1)
    same = (ri // SCAN_CHUNK) == (ci // SCAN_CHUNK)
    tri = jnp.where(same & ((ci >= ri) if reverse else (ci <= ri)), 1.0, 0.0).astype(BF16)
    hi = logf.astype(BF16)
    lo = (logf - hi.astype(F32)).astype(BF16)
    b = jnp.dot(tri, hi, preferred_element_type=F32) + jnp.dot(tri, lo, preferred_element_type=F32)

    q_dec = (q * jnp.exp(b)).astype(BF16)
    k_inv = (k * jnp.exp(-b)).astype(BF16)
    cr = lax.broadcasted_iota(jnp.int32, (SCAN_CHUNK, SCAN_CHUNK), 0)
    cc = lax.broadcasted_iota(jnp.int32, (SCAN_CHUNK, SCAN_CHUNK), 1)
    causal = (cc >= cr) if reverse else (cc <= cr)

    st = st_ref[...]
    order = range(nchunk - 1, -1, -1) if reverse else range(nchunk)
    for c in order:
        lo_r, hi_r = c * SCAN_CHUNK, (c + 1) * SCAN_CHUNK
        tot_r = lo_r if reverse else hi_r - 1
        b_tot = b[tot_r:tot_r + 1, :]
        qd = q_dec[lo_r:hi_r]
        ke = (k[lo_r:hi_r] * jnp.exp(b_tot - b[lo_r:hi_r])).astype(BF16)
        vc = v[lo_r:hi_r]
        scores = lax.dot_general(qd, k_inv[lo_r:hi_r], _NT, preferred_element_type=F32)
        scores = jnp.where(causal, scores, 0.0).astype(BF16)
        o_c = (jnp.dot(scores, vc, preferred_element_type=F32)
               + lax.dot_general(qd, st.astype(BF16), _NT, preferred_element_type=F32))
        oacc_ref[lo_r:hi_r, :] = o_c
        st = st * jnp.exp(b_tot) + lax.dot_general(vc, ke, _TN, preferred_element_type=F32)
    st_ref[...] = st
    sfin_ref[0, 0] = st

    o = oacc_ref[...]
    if finalize:
        o = o + op_ref[0]
        o = o * lax.rsqrt(jnp.mean(o * o, axis=-1, keepdims=True) + EPS)
        o = o * nw_ref[...] * _silu(g_ref[0].astype(F32))
    o_ref[0] = o.astype(o_ref.dtype)


def _scan(zin, lb, s0, *, reverse, o_prev=None, norm_w=None):
    bn, t, _ = zin.shape
    finalize = o_prev is not None
    rows = min(SCAN_ROWS, t)
    nb = t // rows
    pos = (lambda n: nb - 1 - n) if reverse else (lambda n: n)
    hw = A_DK

    def col(section):
        return pl.BlockSpec((1, rows, hw), lambda b, h, n: (b, pos(n), section * A_HEADS + h))

    in_specs = [col(0), col(2 if reverse else 1), col(3),
                pl.BlockSpec((1, hw), lambda b, h, n: (0, h)),
                pl.BlockSpec((1, 1, A_DV, A_DK), lambda b, h, n: (b, h, 0, 0))]
    args = [zin, zin, zin, lb.reshape(1, A_KW), s0]
    if finalize:
        in_specs += [col(4),
                     pl.BlockSpec((1, rows, A_DV), lambda b, h, n: (b, pos(n), h)),
                     pl.BlockSpec((1, A_DV), lambda b, h, n: (0, h))]
        args += [zin, o_prev, norm_w.reshape(1, A_WIDTH)]
    return pl.pallas_call(
        functools.partial(_scan_kernel, reverse=reverse, finalize=finalize),
        grid=(bn, A_HEADS, nb),
        in_specs=in_specs,
        out_specs=[pl.BlockSpec((1, rows, A_DV), lambda b, h, n: (b, pos(n), h)),
                   pl.BlockSpec((1, 1, A_DV, A_DK), lambda b, h, n: (b, h, 0, 0))],
        out_shape=[jax.ShapeDtypeStruct((bn, t, A_WIDTH), BF16 if finalize else F32),
                   jax.ShapeDtypeStruct((bn, A_HEADS, A_DV, A_DK), F32)],
        scratch_shapes=[pltpu.VMEM((A_DV, A_DK), F32), pltpu.VMEM((rows, A_DV), F32)],
        compiler_params=_params("parallel", "parallel", "arbitrary"),
        name="hgrn2_scan_bwd" if reverse else "hgrn2_scan_fwd",
    )(*args)


def _post_mix(acc, x_ref, g1_ref, lnw_ref, lnb_ref, sc2_ref, sh2_ref, rw_ref, rb_ref,
              x1_ref, h2_ref, ti_ref, tg_ref):
    r = DN_ALPHA * x_ref[0] + g1_ref[0] * acc
    mu = jnp.mean(r, axis=-1, keepdims=True)
    rc = r - mu
    var = jnp.mean(rc * rc, axis=-1, keepdims=True)
    x1 = rc * lax.rsqrt(var + EPS) * lnw_ref[...] + lnb_ref[...]
    x1_ref[0] = x1
    h2 = x1 * (1.0 + sc2_ref[0]) + sh2_ref[0]
    _store_row_tiles(h2_ref, h2)
    logits = lax.dot_general(rw_ref[...], h2.astype(BF16), _NT, preferred_element_type=F32) + rb_ref[...]
    iota = lax.broadcasted_iota(jnp.int32, logits.shape, 0)
    vals, idxs = [], []
    cur = logits
    for _ in range(TOP_K):
        m = jnp.max(cur, axis=0, keepdims=True)
        ik = jnp.min(jnp.where(cur == m, iota, N_EXPERTS), axis=0, keepdims=True)
        vals.append(m)
        idxs.append(ik)
        cur = jnp.where(iota == ik, -jnp.inf, cur)
    tv = jnp.concatenate(vals, axis=0)
    e = jnp.exp(tv - tv[0:1])
    tg_ref[...] = e / jnp.sum(e, axis=0, keepdims=True)
    ti_ref[...] = jnp.concatenate(idxs, axis=0)


def _outproj_even_kernel(a_ref, u_ref, gb_ref, gc_ref, up_ref, gcp_ref, un_ref, gcn_ref, cw_ref, wo_ref,
                         x_ref, g1_ref, lnw_ref, lnb_ref, sc2_ref, sh2_ref, rw_ref, rb_ref, alias_ref,
                         x1_ref, h2_ref, ti_ref, tg_ref):
    del alias_ref
    i = pl.program_id(1)
    bm = u_ref.shape[1]
    z = gc_ref[0].astype(F32) * u_ref[0].astype(F32)
    z_before = gcp_ref[0, HALO - 1:HALO, :].astype(F32) * up_ref[0, HALO - 1:HALO, :].astype(F32)
    z_after = gcn_ref[0, 0:1, :].astype(F32) * un_ref[0, 0:1, :].astype(F32)
    z_before = jnp.where(i == 0, 0.0, z_before)
    z_after = jnp.where(i == pl.num_programs(1) - 1, 0.0, z_after)
    row = lax.broadcasted_iota(jnp.int32, z.shape, 0)
    z_prev = jnp.where(row == 0, z_before, pltpu.roll(z, 1, axis=0))
    z_next = jnp.where(row == bm - 1, z_after, pltpu.roll(z, bm - 1, axis=0))
    zc = cw_ref[0:1, :] * z_prev + cw_ref[1:2, :] * z + cw_ref[2:3, :] * z_next
    b_mix = (gb_ref[0].astype(F32) * zc).astype(BF16)
    acc = (jnp.dot(a_ref[0], wo_ref[0:A_WIDTH, :], preferred_element_type=F32)
           + jnp.dot(b_mix, wo_ref[A_WIDTH:, :], preferred_element_type=F32))
    _post_mix(acc, x_ref, g1_ref, lnw_ref, lnb_ref, sc2_ref, sh2_ref, rw_ref, rb_ref,
              x1_ref, h2_ref, ti_ref, tg_ref)


def _outproj_odd_kernel(c_ref, u_ref, v_ref, dw_ref, db_ref, spw_ref, spb_ref, wo_ref,
                        x_ref, g1_ref, lnw_ref, lnb_ref, sc2_ref, sh2_ref, rw_ref, rb_ref, alias_ref,
                        x1_ref, h2_ref, ti_ref, tg_ref):
    del alias_ref
    bm = u_ref.shape[1]
    u = _gelu(u_ref[0].astype(F32))
    v = _gelu(v_ref[0].astype(F32))
    cols = []
    for g in range(D_GROUPS):
        lo_c, hi_c = g * D_GDIM, (g + 1) * D_GDIM
        vg = v[:, lo_c:hi_c]
        mu = jnp.mean(vg, axis=-1, keepdims=True)
        vc = vg - mu
        var = jnp.mean(vc * vc, axis=-1, keepdims=True)
        vn = (vc * lax.rsqrt(var + EPS) * dw_ref[:, lo_c:hi_c] + db_ref[:, lo_c:hi_c]).astype(BF16)
        parts = []
        for ch in range(bm // D_CHUNK):
            sv = jnp.dot(spw_ref[g], vn[ch * D_CHUNK:(ch + 1) * D_CHUNK], preferred_element_type=F32)
            parts.append(sv + spb_ref[:, g:g + 1])
        cols.append(u[:, lo_c:hi_c] * jnp.concatenate(parts, axis=0))
    d_mix = jnp.concatenate(cols, axis=1).astype(BF16)
    acc = (jnp.dot(c_ref[0].astype(BF16), wo_ref[0:C_WIDTH, :], preferred_element_type=F32)
           + jnp.dot(d_mix, wo_ref[C_WIDTH:, :], preferred_element_type=F32))
    _post_mix(acc, x_ref, g1_ref, lnw_ref, lnb_ref, sc2_ref, sh2_ref, rw_ref, rb_ref,
              x1_ref, h2_ref, ti_ref, tg_ref)


def _outproj(kind, mixer_args, mixer_specs, wo, x, mods, mod_row, lnw, lnb, rw_t, rb, tok_state, tok_off, bm):
    bn, t, d = x.shape
    nt = t // bm
    g1, sc2, sh2 = mods
    h2_all, ti_all, tg_all = tok_state
    ntok = h2_all.shape[0] // ROW_TILES
    off = tok_off // bm
    mod_spec = pl.BlockSpec((1, 1, d), lambda b, i: (mod_row(b), 0, 0))
    vec_spec = pl.BlockSpec((1, d), lambda b, i: (0, 0))
    in_specs = list(mixer_specs) + [
        pl.BlockSpec((d, d), lambda b, i: (0, 0)),
        pl.BlockSpec((1, bm, d), lambda b, i: (b, i, 0)),
        mod_spec, vec_spec, vec_spec, mod_spec, mod_spec,
        pl.BlockSpec((N_EXPERTS, d), lambda b, i: (0, 0)),
        pl.BlockSpec((N_EXPERTS, 1), lambda b, i: (0, 0)),
        pl.BlockSpec(memory_space=pl.ANY), pl.BlockSpec(memory_space=pl.ANY), pl.BlockSpec(memory_space=pl.ANY)]
    n_in = len(in_specs)
    kernel = _outproj_even_kernel if kind == "even" else _outproj_odd_kernel

    def body(*refs):
        ins, outs = refs[:n_in], refs[n_in:]
        kernel(*ins[:n_in - 3], ins[n_in - 3:], *outs)

    return pl.pallas_call(
        body,
        grid=(bn, nt),
        in_specs=in_specs,
        out_specs=[pl.BlockSpec((1, bm, d), lambda b, i: (b, i, 0)),
                   pl.BlockSpec((bm * ROW_TILES, LANES), lambda b, i: (off + b * nt + i, 0)),
                   pl.BlockSpec((TOP_K, bm), lambda b, i: (0, off + b * nt + i)),
                   pl.BlockSpec((TOP_K, bm), lambda b, i: (0, off + b * nt + i))],
        out_shape=[jax.ShapeDtypeStruct((bn, t, d), F32),
                   jax.ShapeDtypeStruct((ntok * ROW_TILES, LANES), F32),
                   jax.ShapeDtypeStruct((TOP_K, ntok), jnp.int32),
                   jax.ShapeDtypeStruct((TOP_K, ntok), F32)],
        input_output_aliases={n_in - 3: 1, n_in - 2: 2, n_in - 1: 3},
        compiler_params=_params("parallel", "parallel"),
        name="out_proj_" + kind,
    )(*mixer_args, wo, x, g1, lnw.reshape(1, d), lnb.reshape(1, d), sc2, sh2, rw_t, rb.reshape(N_EXPERTS, 1),
      h2_all, ti_all, tg_all)


def _outproj_even(a_mix, zin, conv_w, **kw):
    bm = kw["bm"]
    t = zin.shape[1]
    hb = bm // HALO
    last = t // HALO - 1
    cur = lambda sec: pl.BlockSpec((1, bm, B_WIDTH), lambda b, i: (b, i, sec))
    before = lambda sec: pl.BlockSpec((1, HALO, B_WIDTH), lambda b, i: (b, jnp.maximum(i * hb - 1, 0), sec))
    after = lambda sec: pl.BlockSpec((1, HALO, B_WIDTH), lambda b, i: (b, jnp.minimum((i + 1) * hb, last), sec))
    specs = [pl.BlockSpec((1, bm, A_WIDTH), lambda b, i: (b, i, 0)), cur(5), cur(6), cur(7),
             before(5), before(7), after(5), after(7),
             pl.BlockSpec((CONV_W, B_WIDTH), lambda b, i: (0, 0))]
    return _outproj("even", [a_mix, zin, zin, zin, zin, zin, zin, zin, conv_w], specs, **kw)


def _outproj_odd(c_mix, zin, d_norm_w, d_norm_b, sp_w, sp_b, **kw):
    bm = kw["bm"]
    cur = lambda sec: pl.BlockSpec((1, bm, D_WIDTH), lambda b, i: (b, i, sec))
    specs = [pl.BlockSpec((1, bm, C_WIDTH), lambda b, i: (b, i, 0)), cur(1), cur(2),
             pl.BlockSpec((1, D_WIDTH), lambda b, i: (0, 0)), pl.BlockSpec((1, D_WIDTH), lambda b, i: (0, 0)),
             pl.BlockSpec((D_GROUPS, D_CHUNK, D_CHUNK), lambda b, i: (0, 0, 0)),
             pl.BlockSpec((D_CHUNK, D_GROUPS), lambda b, i: (0, 0))]
    return _outproj("odd", [c_mix, zin, zin, d_norm_w.reshape(1, D_WIDTH), d_norm_b.reshape(1, D_WIDTH),
                            sp_w.astype(BF16), sp_b.T], specs, **kw)


def _dft_mats(n):
    ang = 2.0 * np.pi * np.outer(np.arange(n), np.arange(n)) / n
    return np.cos(ang), np.sin(ang)


def _fourier_cw_kernel(z_ref, nw_ref, cs_ref, kc_ref, ks_ref, p_ref, q_ref, ps_ref, qs_ref):
    tile = z_ref.shape[1]
    half = 256
    grows = tile // GRID_W
    for hs in range(tile // half):
        z = z_ref[0, hs * half:(hs + 1) * half, :].astype(F32)
        a_parts, b_parts = [], []
        for g in range(C_GROUPS):
            zg = z[:, g * C_GROUP:(g + 1) * C_GROUP]
            zg = zg * lax.rsqrt(jnp.mean(zg * zg, axis=-1, keepdims=True) + EPS) * nw_ref[:, g * C_GROUP:(g + 1) * C_GROUP]
            ab = jnp.dot(zg.astype(BF16), cs_ref[...], preferred_element_type=F32)
            a_parts.append(ab[:, :C_GROUP])
            b_parts.append(ab[:, C_GROUP:])
        ab = jnp.concatenate(a_parts + b_parts, axis=1).astype(BF16)
        m1 = jnp.dot(kc_ref[...], ab, preferred_element_type=F32)
        m2 = jnp.dot(ks_ref[...], ab, preferred_element_type=F32)
        pv = m1[:, :C_WIDTH] - m2[:, C_WIDTH:]
        qv = m2[:, :C_WIDTH] + m1[:, C_WIDTH:]
        for g in range(C_GROUPS):
            ps_ref[g, hs * half:(hs + 1) * half, :] = pv[:, g * C_GROUP:(g + 1) * C_GROUP]
            qs_ref[g, hs * half:(hs + 1) * half, :] = qv[:, g * C_GROUP:(g + 1) * C_GROUP]
    for k2 in range(GRID_W):
        for g in range(C_GROUPS):
            p_ref[0, k2, :, g * C_GROUP:(g + 1) * C_GROUP] = ps_ref[g, pl.ds(k2, grows, stride=GRID_W), :]
            q_ref[0, k2, :, g * C_GROUP:(g + 1) * C_GROUP] = qs_ref[g, pl.ds(k2, grows, stride=GRID_W), :]


def _fourier_r_kernel(p_ref, q_ref, c_ref, s_ref, o_ref, ys_ref):
    nj = p_ref.shape[1]
    rows = p_ref.shape[2]
    for j in range(nj):
        y = (jnp.dot(c_ref[...], p_ref[0, j].astype(BF16), preferred_element_type=F32)
             + jnp.dot(s_ref[...], q_ref[0, j].astype(BF16), preferred_element_type=F32))
        for g in range(C_GROUPS):
            ys_ref[g, j * rows:(j + 1) * rows, :] = y[:, g * C_GROUP:(g + 1) * C_GROUP]

    def put(k1, carry):
        for g in range(C_GROUPS):
            o_ref[0, k1, :, g * C_GROUP:(g + 1) * C_GROUP] = ys_ref[g, pl.ds(k1, nj, stride=rows), :]
        return carry

    lax.fori_loop(0, rows, put, 0)


def _fourier_mix(zin, c_norm_w):
    bn, t, _ = zin.shape
    rows = t // GRID_W
    tile = 512
    grows = tile // GRID_W
    c3, s3 = _dft_mats(C_GROUP)
    c2, s2 = _dft_mats(GRID_W)
    c1, s1 = _dft_mats(rows)
    scale = 1.0 / math.sqrt(rows * GRID_W * C_GROUP)
    cs3 = jnp.asarray(np.concatenate([c3, s3], axis=1), BF16)
    eye = np.eye(256 // GRID_W)
    kc2 = jnp.asarray(np.kron(eye, c2), BF16)
    ks2 = jnp.asarray(np.kron(eye, s2), BF16)
    c1s = jnp.asarray(c1 * scale, BF16)
    s1s = jnp.asarray(-s1 * scale, BF16)
    pq_shape = jax.ShapeDtypeStruct((bn, GRID_W, rows, C_WIDTH), F32)
    pq_spec = pl.BlockSpec((1, GRID_W, grows, C_WIDTH), lambda b, i: (b, 0, i, 0))
    p, q = pl.pallas_call(
        _fourier_cw_kernel,
        grid=(bn, t // tile),
        in_specs=[pl.BlockSpec((1, tile, C_WIDTH), lambda b, i: (b, i, 0)),
                  pl.BlockSpec((1, C_WIDTH), lambda b, i: (0, 0)),
                  pl.BlockSpec((C_GROUP, 2 * C_GROUP), lambda b, i: (0, 0)),
                  pl.BlockSpec((256, 256), lambda b, i: (0, 0)),
                  pl.BlockSpec((256, 256), lambda b, i: (0, 0))],
        out_specs=[pq_spec, pq_spec],
        out_shape=[pq_shape, pq_shape],
        scratch_shapes=[pltpu.VMEM((C_GROUPS, tile, C_GROUP), F32), pltpu.VMEM((C_GROUPS, tile, C_GROUP), F32)],
        compiler_params=_params("parallel", "parallel"),
        name="fourier_chan_col",
    )(zin, c_norm_w.reshape(1, C_WIDTH), cs3, kc2, ks2)
    nj = 8
    in_spec = pl.BlockSpec((1, nj, rows, C_WIDTH), lambda b, j: (b, j, 0, 0))
    mat_spec = pl.BlockSpec((rows, rows), lambda b, j: (0, 0))
    y = pl.pallas_call(
        _fourier_r_kernel,
        grid=(bn, GRID_W // nj),
        in_specs=[in_spec, in_spec, mat_spec, mat_spec],
        out_specs=pl.BlockSpec((1, rows, nj, C_WIDTH), lambda b, j: (b, 0, j, 0)),
        out_shape=jax.ShapeDtypeStruct((bn, rows, GRID_W, C_WIDTH), F32),
        scratch_shapes=[pltpu.VMEM((C_GROUPS, nj * rows, C_GROUP), F32)],
        compiler_params=_params("parallel", "parallel"),
        name="fourier_rows",
    )(p, q, c1s, s1s)
    return y.reshape(bn, t, C_WIDTH)


def _rank_kernel(ti_ref, rank_ref, cnt_ref, run_ref):
    @pl.when(pl.program_id(0) == 0)
    def _():
        run_ref[...] = jnp.zeros_like(run_ref)

    ti = ti_ref[...]
    tb = ti.shape[1]
    eio = lax.broadcasted_iota(jnp.int32, (N_EXPERTS, tb), 0)
    hot = [eio == ti[k:k + 1, :] for k in range(TOP_K)]
    occ = sum(h.astype(F32) for h in hot)
    si = lax.broadcasted_iota(jnp.int32, (tb, tb), 0)
    ti_ = lax.broadcasted_iota(jnp.int32, (tb, tb), 1)
    before = jnp.where(si < ti_, 1.0, 0.0).astype(BF16)
    seen = jnp.dot(occ.astype(BF16), before, preferred_element_type=F32) + run_ref[:, 0:1]
    rank_ref[...] = jnp.concatenate(
        [jnp.sum(jnp.where(h, seen, 0.0), axis=0, keepdims=True) for h in hot], axis=0).astype(jnp.int32)
    run_ref[...] = run_ref[...] + jnp.sum(occ, axis=1, keepdims=True)
    cnt_ref[...] = run_ref[...].astype(jnp.int32)


def _ranks(top_i):
    ntok = top_i.shape[1]
    tb = ROUTE_TILE if ntok % ROUTE_TILE == 0 else 256
    rank, cnt = pl.pallas_call(
        _rank_kernel,
        grid=(ntok // tb,),
        in_specs=[pl.BlockSpec((TOP_K, tb), lambda i: (0, i))],
        out_specs=[pl.BlockSpec((TOP_K, tb), lambda i: (0, i)),
                   pl.BlockSpec((N_EXPERTS, 128), lambda i: (0, 0))],
        out_shape=[jax.ShapeDtypeStruct((TOP_K, ntok), jnp.int32),
                   jax.ShapeDtypeStruct((N_EXPERTS, 128), jnp.int32)],
        scratch_shapes=[pltpu.VMEM((N_EXPERTS, 128), F32)],
        compiler_params=_params("arbitrary"),
        name="route_rank",
    )(top_i)
    return rank, cnt[:, 0]


def _w1_split_kernel(w_ref, g_ref, l_ref, s_ref):
    kdim, cols = w_ref.shape[1], w_ref.shape[2]
    half = cols // 2
    nk = kdim // LANES
    t = w_ref[0].T
    for kc in range(nk):
        s_ref[kc] = t[:, kc * LANES:(kc + 1) * LANES]
    even = jnp.concatenate([s_ref[kc, pl.ds(0, half, stride=2), :] for kc in range(nk)], axis=1)
    odd = jnp.concatenate([s_ref[kc, pl.ds(1, half, stride=2), :] for kc in range(nk)], axis=1)
    g_ref[0] = even.T.astype(BF16)
    l_ref[0] = odd.T.astype(BF16)


def _w1_split(w1):
    e, d, f2 = w1.shape
    cols = 256
    out = jax.ShapeDtypeStruct((e, d, f2 // 2), BF16)
    out_spec = pl.BlockSpec((1, d, cols // 2), lambda i, c: (i, 0, c))
    return pl.pallas_call(
        _w1_split_kernel,
        grid=(e, f2 // cols),
        in_specs=[pl.BlockSpec((1, d, cols), lambda i, c: (i, 0, c))],
        out_specs=[out_spec, out_spec],
        out_shape=[out, out],
        scratch_shapes=[pltpu.VMEM((d // LANES, cols, LANES), F32)],
        compiler_params=_params("parallel", "parallel"),
        name="moe_w1_split",
    )(w1)


def _expert_kernel(be_ref, tok_ref, tokn_ref, h_hbm, *refs, ns):
    del be_ref
    wa, wb = refs[0:6], refs[6:12]
    y_ref, xa, xb, sem = refs[12:]
    j = pl.program_id(0)
    rows = xa.shape[0] // ROW_TILES

    def row_copy(tok, buf, s, r):
        src = h_hbm.at[pl.ds(pl.multiple_of(tok * ROW_TILES, ROW_TILES), ROW_TILES), :]
        return pltpu.make_async_copy(src, buf.at[pl.ds(r * ROW_TILES, ROW_TILES), :], sem.at[s])

    def whole(buf, s):
        return pltpu.make_async_copy(h_hbm.at[pl.ds(0, rows * ROW_TILES), :], buf, sem.at[s])

    def gather(idx_ref, half, buf, s):
        for r in range(rows):
            row_copy(idx_ref[half, 0, r], buf, s, r).start(priority=r % 2)

    def ffn(x_ref, w1g_ref, w1l_ref, w2_ref, b1g_ref, b1l_ref, b2_ref):
        x = _load_row_tiles(x_ref, rows).astype(BF16)
        glu = jnp.minimum(jnp.dot(x, w1g_ref[0], preferred_element_type=F32) + b1g_ref[0], SWIGLU_LIMIT)
        lin = jnp.clip(jnp.dot(x, w1l_ref[0], preferred_element_type=F32) + b1l_ref[0],
                       -SWIGLU_LIMIT, SWIGLU_LIMIT)
        act = glu * _sigmoid(SWIGLU_ALPHA * glu) * (lin + 1.0)
        return jnp.dot(act.astype(BF16), w2_ref[0], preferred_element_type=F32) + b2_ref[0]

    @pl.when(j == 0)
    def _():
        gather(tok_ref, 0, xa, 0)

    whole(xa, 0).wait()
    gather(tok_ref, 1, xb, 1)
    _store_row_tiles(y_ref.at[pl.ds(0, rows * ROW_TILES), :], ffn(xa, *wa))
    whole(xb, 1).wait()
    gather(tokn_ref, 0, xa, 0)
    _store_row_tiles(y_ref.at[pl.ds(rows * ROW_TILES, rows * ROW_TILES), :], ffn(xb, *wb))

    @pl.when(j == ns - 1)
    def _():
        whole(xa, 0).wait()


def _experts(h_all, slot_tok, block_expert, w1g, w1l, w2, b1g, b1l, b2):
    d = w1g.shape[1]
    p = slot_tok.shape[0]
    ns = p // (2 * MOE_ROWS)
    f = w1g.shape[2]
    buf_rows = MOE_ROWS * ROW_TILES
    tok3 = slot_tok.reshape(2 * ns, 1, MOE_ROWS)
    weights = (w1g, w1l, w2, b1g, b1l, b2)

    def wspecs(half):
        spec = lambda k, n: pl.BlockSpec((1, k, n), lambda j, be: (be[2 * j + half], 0, 0))
        return [spec(d, f), spec(d, f), spec(f, d), spec(1, f), spec(1, f), spec(1, d)]

    grid_spec = pltpu.PrefetchScalarGridSpec(
        num_scalar_prefetch=1,
        grid=(ns,),
        in_specs=[pl.BlockSpec((2, 1, MOE_ROWS), lambda j, be: (j, 0, 0), memory_space=pltpu.SMEM),
                  pl.BlockSpec((2, 1, MOE_ROWS), lambda j, be: (jnp.minimum(j + 1, ns - 1), 0, 0),
                               memory_space=pltpu.SMEM),
                  pl.BlockSpec(memory_space=pl.ANY)] + wspecs(0) + wspecs(1),
        out_specs=pl.BlockSpec((2 * buf_rows, LANES), lambda j, be: (j, 0)),
        scratch_shapes=[pltpu.VMEM((buf_rows, LANES), F32), pltpu.VMEM((buf_rows, LANES), F32),
                        pltpu.SemaphoreType.DMA((2,))])
    return pl.pallas_call(
        functools.partial(_expert_kernel, ns=ns),
        grid_spec=grid_spec,
        out_shape=jax.ShapeDtypeStruct((p * ROW_TILES, LANES), F32),
        compiler_params=_params("arbitrary"),
        name="moe_experts",
    )(block_expert, tok3, tok3, h_all, *weights, *weights)


def _combine_kernel(dst_ref, dstn_ref, y_hbm, x1_ref, gate_ref, g2_ref, lnw_ref, lnb_ref, o_ref, ybuf, sem, *, nt):
    i = pl.program_id(0)
    tb = x1_ref.shape[0]
    slot = i % 2

    def row_copy(src, buf, k, r):
        rows = y_hbm.at[pl.ds(pl.multiple_of(src * ROW_TILES, ROW_TILES), ROW_TILES), :]
        return pltpu.make_async_copy(rows, ybuf.at[buf, k, pl.ds(r * ROW_TILES, ROW_TILES), :], sem.at[buf])

    def gather(idx_ref, buf):
        for r in range(tb):
            for k in range(TOP_K):
                row_copy(idx_ref[0, k, r], buf, k, r).start(priority=k % 2)

    @pl.when(i == 0)
    def _():
        gather(dst_ref, 0)

    @pl.when(i + 1 < nt)
    def _():
        gather(dstn_ref, 1 - slot)

    for k in range(TOP_K):
        pltpu.make_async_copy(y_hbm.at[pl.ds(0, tb * ROW_TILES), :], ybuf.at[slot, k], sem.at[slot]).wait()

    y = sum(gate_ref[:, k:k + 1] * _load_row_tiles(ybuf.at[slot, k], tb) for k in range(TOP_K))
    r = DN_ALPHA * x1_ref[...] + g2_ref[0] * y
    mu = jnp.mean(r, axis=-1, keepdims=True)
    rc = r - mu
    var = jnp.mean(rc * rc, axis=-1, keepdims=True)
    o_ref[...] = rc * lax.rsqrt(var + EPS) * lnw_ref[...] + lnb_ref[...]


def _combine(x1, y_sorted, dest3, gates_t, tok_off, g2, mod_row, lnw, lnb):
    bn, t, d = x1.shape
    n = bn * t
    tb = COMBINE_TILE
    nt = n // tb
    off = tok_off // tb
    per_b = t // tb
    out = pl.pallas_call(
        functools.partial(_combine_kernel, nt=nt),
        grid=(nt,),
        in_specs=[pl.BlockSpec((1, TOP_K, tb), lambda i: (off + i, 0, 0), memory_space=pltpu.SMEM),
                  pl.BlockSpec((1, TOP_K, tb), lambda i: (off + jnp.minimum(i + 1, nt - 1), 0, 0),
                               memory_space=pltpu.SMEM),
                  pl.BlockSpec(memory_space=pl.ANY),
                  pl.BlockSpec((tb, d), lambda i: (i, 0)),
                  pl.BlockSpec((tb, TOP_K), lambda i: (off + i, 0)),
                  pl.BlockSpec((1, 1, d), lambda i: (mod_row(i // per_b), 0, 0)),
                  pl.BlockSpec((1, d), lambda i: (0, 0)),
                  pl.BlockSpec((1, d), lambda i: (0, 0))],
        out_specs=pl.BlockSpec((tb, d), lambda i: (i, 0)),
        out_shape=jax.ShapeDtypeStruct((n, d), F32),
        scratch_shapes=[pltpu.VMEM((2, TOP_K, tb * ROW_TILES, LANES), F32), pltpu.SemaphoreType.DMA((2,))],
        compiler_params=_params("arbitrary"),
        name="moe_combine",
    )(dest3, dest3, y_sorted, x1.reshape(n, d), gates_t, g2, lnw.reshape(1, d), lnb.reshape(1, d))
    return out.reshape(bn, t, d)


def _dest_kernel(ti_ref, rank_ref, start_ref, dest_ref):
    ti = ti_ref[...]
    tb = ti.shape[1]
    eio = lax.broadcasted_iota(jnp.int32, (N_EXPERTS, tb), 0)
    start = start_ref[:, 0:1].astype(F32)
    base = jnp.concatenate(
        [jnp.sum(jnp.where(eio == ti[k:k + 1, :], start, 0.0), axis=0, keepdims=True) for k in range(TOP_K)], axis=0)
    dest = base.astype(jnp.int32) + rank_ref[...]
    for c in range(tb // COMBINE_TILE):
        dest_ref[c] = dest[:, c * COMBINE_TILE:(c + 1) * COMBINE_TILE]


def _moe_dispatch(top_i):
    ntok = top_i.shape[1]
    m = ntok * TOP_K
    rank, counts = _ranks(top_i)
    padded = (counts + MOE_ROWS - 1) // MOE_ROWS * MOE_ROWS
    pad_ends = jnp.cumsum(padded)
    pad_starts = pad_ends - padded
    tb = ROUTE_TILE if ntok % ROUTE_TILE == 0 else 256
    ntiles = ntok // COMBINE_TILE
    dest = pl.pallas_call(
        _dest_kernel,
        grid=(ntok // tb,),
        in_specs=[pl.BlockSpec((TOP_K, tb), lambda i: (0, i)),
                  pl.BlockSpec((TOP_K, tb), lambda i: (0, i)),
                  pl.BlockSpec((N_EXPERTS, LANES), lambda i: (0, 0))],
        out_specs=pl.BlockSpec((tb // COMBINE_TILE, TOP_K, COMBINE_TILE), lambda i: (i, 0, 0)),
        out_shape=jax.ShapeDtypeStruct((ntiles, TOP_K, COMBINE_TILE), jnp.int32),
        compiler_params=_params("parallel"),
        name="route_dest",
    )(top_i, rank, jnp.broadcast_to(pad_starts[:, None], (N_EXPERTS, LANES)))
    nb = (m + N_EXPERTS * (MOE_ROWS - 1) + MOE_ROWS - 1) // MOE_ROWS
    nb += nb % 2
    p = nb * MOE_ROWS
    tok_ids = jnp.broadcast_to(
        jnp.arange(ntok, dtype=jnp.int32).reshape(ntiles, 1, COMBINE_TILE), (ntiles, TOP_K, COMBINE_TILE))
    slot_tok = jnp.zeros((p,), jnp.int32).at[dest.reshape(-1)].set(tok_ids.reshape(-1))
    block_starts = jnp.arange(nb, dtype=jnp.int32) * MOE_ROWS
    block_expert = jnp.minimum(jnp.sum(block_starts[:, None] >= pad_ends[None, :], axis=1),
                               N_EXPERTS - 1).astype(jnp.int32)
    return dest, slot_tok, block_expert


def kernel(x, c, ctx, c_ctx, lower_bounds_fwd, lower_bounds_bwd, ada_w, ada_b, w_in_even, a_norm_w, conv_w,
           w_in_odd, c_norm_w, d_norm_w, d_norm_b, spatial_w, spatial_b, w_out, ln_mix_w, ln_mix_b,
           ln_ffn_w, ln_ffn_b, router_w, router_b, moe_w1, moe_b1, moe_w2, moe_b2):
    bn, t, d = x.shape
    tc = ctx.shape[1]
    n_lat = bn * t
    n_ctx = bn * tc
    ctx_row = bn
    lat_row = lambda b: b
    ctx_mod_row = lambda b: ctx_row

    lb_f_all = jnp.cumsum(jax.nn.softmax(lower_bounds_fwd.astype(F32), axis=0), axis=0)
    lb_b_all = jnp.cumsum(jax.nn.softmax(lower_bounds_bwd.astype(F32), axis=0), axis=0)

    cond = jnp.zeros((MOD_ROWS, d), F32).at[:bn].set(c).at[ctx_row].set(c_ctx)
    mods = _ada(cond, ada_w, ada_b).reshape(DEPTH, MOD_ROWS, 6, 1, d)

    h_ctx = ctx
    for l in range(DEPTH):
        last = l == DEPTH - 1
        sh1, sc1, g1, sh2, sc2, g2 = (mods[l, :, j] for j in range(6))
        wo = w_out[l].astype(BF16)
        rw_t = router_w[l].T.astype(BF16)
        ntok = n_lat if last else n_lat + n_ctx
        tok_state = (jnp.zeros((ntok * ROW_TILES, LANES), F32), jnp.zeros((TOP_K, ntok), jnp.int32),
                     jnp.zeros((TOP_K, ntok), F32))
        post = dict(wo=wo, lnw=ln_mix_w[l], lnb=ln_mix_b[l], rw_t=rw_t, rb=router_b[l], bm=ROW_TILE)
        if l % 2 == 0:
            e = l // 2
            w_in = w_in_even[e].astype(BF16)
            zin_c = _inproj(h_ctx, sc1, sh1, w_in, ctx_mod_row, bm=min(ROW_TILE, tc))
            zin = _inproj(x, sc1, sh1, w_in, lat_row, bm=ROW_TILE)
            zeros = jnp.zeros((bn, A_HEADS, A_DV, A_DK), F32)
            oc_f, sc_f = _scan(zin_c, lb_f_all[l], zeros, reverse=False)
            a_ctx, sc_b = _scan(zin_c, lb_b_all[l], zeros, reverse=True, o_prev=oc_f, norm_w=a_norm_w[e])
            o_f, _ = _scan(zin, lb_f_all[l], sc_f, reverse=False)
            a_lat, _ = _scan(zin, lb_b_all[l], sc_b, reverse=True, o_prev=o_f, norm_w=a_norm_w[e])
            x1, *tok_state = _outproj_even(a_lat, zin, conv_w[e], x=x, mods=(g1, sc2, sh2), mod_row=lat_row,
                                           tok_state=tok_state, tok_off=0, **post)
            if not last:
                hc1, *tok_state = _outproj_even(a_ctx, zin_c, conv_w[e], x=h_ctx, mods=(g1, sc2, sh2),
                                                mod_row=ctx_mod_row, tok_state=tok_state, tok_off=n_lat, **post)
        else:
            o = l // 2
            w_in = w_in_odd[o].astype(BF16)
            odd = dict(d_norm_w=d_norm_w[o], d_norm_b=d_norm_b[o], sp_w=spatial_w[o], sp_b=spatial_b[o])
            zin = _inproj(x, sc1, sh1, w_in, lat_row, bm=ROW_TILE)
            c_mix = _fourier_mix(zin, c_norm_w[o])
            x1, *tok_state = _outproj_odd(c_mix, zin, x=x, mods=(g1, sc2, sh2), mod_row=lat_row,
                                          tok_state=tok_state, tok_off=0, **odd, **post)
            if not last:
                raise NotImplementedError("an odd layer that is not the last needs the context Fourier mixer")
        h2_all, top_i, top_g = tok_state
        dest, slot_tok, block_expert = _moe_dispatch(top_i)
        w1g, w1l = _w1_split(moe_w1[l])
        y_sorted = _experts(
            h2_all, slot_tok, block_expert, w1g, w1l, moe_w2[l].astype(BF16),
            moe_b1[l][:, None, 0::2], moe_b1[l][:, None, 1::2], moe_b2[l][:, None, :])
        gates_t = top_g.T
        x = _combine(x1, y_sorted, dest, gates_t, 0, g2, lat_row, ln_ffn_w[l], ln_ffn_b[l])
        if not last:
            h_ctx = _combine(hc1, y_sorted, dest, gates_t, n_lat, g2, ctx_mod_row, ln_ffn_w[l], ln_ffn_b[l])
    return x
```

```python
import functools
import math

import numpy as np
import jax
import jax.numpy as jnp
from jax import lax
from jax.experimental import pallas as pl
from jax.experimental.pallas import tpu as pltpu

F32 = jnp.float32
BF16 = jnp.bfloat16

D_MODEL = 1024
DEPTH = 2
GRID_W = 64
A_HEADS = 4
A_DK = 128
A_DV = 128
A_KW = A_HEADS * A_DK
A_WIDTH = A_HEADS * A_DV
B_WIDTH = D_MODEL - A_WIDTH
CONV_W = 3
C_GROUPS = 4
C_GROUP = 128
C_WIDTH = C_GROUPS * C_GROUP
D_WIDTH = D_MODEL - C_WIDTH
D_GROUPS = 4
D_GDIM = D_WIDTH // D_GROUPS
D_CHUNK = 128
SCAN_CHUNK = 32
EVEN_IN = 3 * A_KW + 2 * A_WIDTH + 3 * B_WIDTH
ODD_IN = C_WIDTH + 2 * D_WIDTH
N_EXPERTS = 32
TOP_K = 4
D_EXPERT = 1024
SWIGLU_ALPHA = 1.702
SWIGLU_LIMIT = 7.0
DN_ALPHA = (2 * DEPTH) ** 0.25
EPS = 1e-5

MOD_ROWS = 8
SCAN_ROWS = 256
ROW_TILE = 256
MOE_ROWS = 256
ROUTE_TILE = 512
COMBINE_TILE = 128
HALO = 16
LANES = 128
ROW_TILES = D_MODEL // LANES

_NT = (((1,), (1,)), ((), ()))
_TN = (((0,), (0,)), ((), ()))


def _sigmoid(a):
    return 1.0 / (1.0 + jnp.exp(-a))


def _silu(a):
    return a * _sigmoid(a)


def _gelu(a):
    return 0.5 * a * (1.0 + lax.erf(a * (1.0 / math.sqrt(2.0))))


def _store_row_tiles(ref, val):
    rows = val.shape[0]
    for c in range(ROW_TILES):
        ref[pl.ds(c, rows, stride=ROW_TILES), :] = val[:, c * LANES:(c + 1) * LANES]


def _load_row_tiles(ref, rows):
    return jnp.concatenate([ref[pl.ds(c, rows, stride=ROW_TILES), :] for c in range(ROW_TILES)], axis=1)


def _params(*sem):
    return pltpu.CompilerParams(dimension_semantics=sem, vmem_limit_bytes=56 * 1024 * 1024)


def _ada_kernel(c_ref, w_ref, b_ref, o_ref):
    s = _silu(c_ref[...])
    o_ref[0] = jnp.dot(s, w_ref[0], preferred_element_type=F32, precision=lax.Precision.HIGHEST) + b_ref[0]


def _ada(cond, ada_w, ada_b):
    d = cond.shape[1]
    n6 = ada_w.shape[2]
    tn = 1536
    return pl.pallas_call(
        _ada_kernel,
        grid=(DEPTH, n6 // tn),
        in_specs=[pl.BlockSpec((MOD_ROWS, d), lambda l, n: (0, 0)),
                  pl.BlockSpec((1, d, tn), lambda l, n: (l, 0, n)),
                  pl.BlockSpec((1, 1, tn), lambda l, n: (l, 0, n))],
        out_specs=pl.BlockSpec((1, MOD_ROWS, tn), lambda l, n: (l, 0, n)),
        out_shape=jax.ShapeDtypeStruct((DEPTH, MOD_ROWS, n6), F32),
        compiler_params=_params("parallel", "parallel"),
        name="ada_mod",
    )(cond, ada_w, ada_b.reshape(DEPTH, 1, n6))


def _inproj_kernel(x_ref, sc_ref, sh_ref, w_ref, o_ref, *, tn):
    h = (x_ref[0] * (1.0 + sc_ref[0]) + sh_ref[0]).astype(BF16)
    for n in range(o_ref.shape[2] // tn):
        o_ref[0, :, n * tn:(n + 1) * tn] = jnp.dot(
            h, w_ref[:, n * tn:(n + 1) * tn], preferred_element_type=F32).astype(o_ref.dtype)


def _inproj(x, sc, sh, w, mod_row, bm):
    bn, t, d = x.shape
    nout = w.shape[1]
    mod_spec = pl.BlockSpec((1, 1, d), lambda b, i: (mod_row(b), 0, 0))
    return pl.pallas_call(
        functools.partial(_inproj_kernel, tn=512),
        grid=(bn, t // bm),
        in_specs=[pl.BlockSpec((1, bm, d), lambda b, i: (b, i, 0)), mod_spec, mod_spec,
                  pl.BlockSpec((d, nout), lambda b, i: (0, 0))],
        out_specs=pl.BlockSpec((1, bm, nout), lambda b, i: (b, i, 0)),
        out_shape=jax.ShapeDtypeStruct((bn, t, nout), BF16),
        compiler_params=_params("parallel", "parallel"),
        name="in_proj",
    )(x, sc, sh, w)


def _scan_kernel(*refs, reverse, finalize):
    if finalize:
        (q_ref, z_ref, v_ref, lb_ref, s0_ref, g_ref, op_ref, nw_ref, o_ref, sfin_ref, st_ref, oacc_ref) = refs
    else:
        (q_ref, z_ref, v_ref, lb_ref, s0_ref, o_ref, sfin_ref, st_ref, oacc_ref) = refs
    rows = q_ref.shape[1]
    nchunk = rows // SCAN_CHUNK

    @pl.when(pl.program_id(2) == 0)
    def _():
        st_ref[...] = s0_ref[0, 0]

    q = _silu(q_ref[0].astype(F32))
    lb = lb_ref[...]
    f = lb + (1.0 - lb) * _sigmoid(z_ref[0].astype(F32))
    logf = jnp.log(f)
    k = 1.0 - f
    v = v_ref[0]

    ri = lax.broadcasted_iota(jnp.int32, (rows, rows), 0)
    ci = lax.broadcasted_iota(jnp.int32, (rows, rows), 1)
    same = (ri // SCAN_CHUNK) == (ci // SCAN_CHUNK)
    tri = jnp.where(same & ((ci >= ri) if reverse else (ci <= ri)), 1.0, 0.0).astype(BF16)
    hi = logf.astype(BF16)
    lo = (logf - hi.astype(F32)).astype(BF16)
    b = jnp.dot(tri, hi, preferred_element_type=F32) + jnp.dot(tri, lo, preferred_element_type=F32)

    q_dec = (q * jnp.exp(b)).astype(BF16)
    k_inv = (k * jnp.exp(-b)).astype(BF16)
    cr = lax.broadcasted_iota(jnp.int32, (SCAN_CHUNK, SCAN_CHUNK), 0)
    cc = lax.broadcasted_iota(jnp.int32, (SCAN_CHUNK, SCAN_CHUNK), 1)
    causal = (cc >= cr) if reverse else (cc <= cr)

    st = st_ref[...]
    order = range(nchunk - 1, -1, -1) if reverse else range(nchunk)
    for c in order:
        lo_r, hi_r = c * SCAN_CHUNK, (c + 1) * SCAN_CHUNK
        tot_r = lo_r if reverse else hi_r - 1
        b_tot = b[tot_r:tot_r + 1, :]
        qd = q_dec[lo_r:hi_r]
        ke = (k[lo_r:hi_r] * jnp.exp(b_tot - b[lo_r:hi_r])).astype(BF16)
        vc = v[lo_r:hi_r]
        scores = lax.dot_general(qd, k_inv[lo_r:hi_r], _NT, preferred_element_type=F32)
        scores = jnp.where(causal, scores, 0.0).astype(BF16)
        o_c = (jnp.dot(scores, vc, preferred_element_type=F32)
               + lax.dot_general(qd, st.astype(BF16), _NT, preferred_element_type=F32))
        oacc_ref[lo_r:hi_r, :] = o_c
        st = st * jnp.exp(b_tot) + lax.dot_general(vc, ke, _TN, preferred_element_type=F32)
    st_ref[...] = st
    sfin_ref[0, 0] = st

    o = oacc_ref[...]
    if finalize:
        o = o + op_ref[0]
        o = o * lax.rsqrt(jnp.mean(o * o, axis=-1, keepdims=True) + EPS)
        o = o * nw_ref[...] * _silu(g_ref[0].astype(F32))
    o_ref[0] = o.astype(o_ref.dtype)


def _scan(zin, lb, s0, *, reverse, o_prev=None, norm_w=None):
    bn, t, _ = zin.shape
    finalize = o_prev is not None
    rows = min(SCAN_ROWS, t)
    nb = t // rows
    pos = (lambda n: nb - 1 - n) if reverse else (lambda n: n)
    hw = A_DK

    def col(section):
        return pl.BlockSpec((1, rows, hw), lambda b, h, n: (b, pos(n), section * A_HEADS + h))

    in_specs = [col(0), col(2 if reverse else 1), col(3),
                pl.BlockSpec((1, hw), lambda b, h, n: (0, h)),
                pl.BlockSpec((1, 1, A_DV, A_DK), lambda b, h, n: (b, h, 0, 0))]
    args = [zin, zin, zin, lb.reshape(1, A_KW), s0]
    if finalize:
        in_specs += [col(4),
                     pl.BlockSpec((1, rows, A_DV), lambda b, h, n: (b, pos(n), h)),
                     pl.BlockSpec((1, A_DV), lambda b, h, n: (0, h))]
        args += [zin, o_prev, norm_w.reshape(1, A_WIDTH)]
    return pl.pallas_call(
        functools.partial(_scan_kernel, reverse=reverse, finalize=finalize),
        grid=(bn, A_HEADS, nb),
        in_specs=in_specs,
        out_specs=[pl.BlockSpec((1, rows, A_DV), lambda b, h, n: (b, pos(n), h)),
                   pl.BlockSpec((1, 1, A_DV, A_DK), lambda b, h, n: (b, h, 0, 0))],
        out_shape=[jax.ShapeDtypeStruct((bn, t, A_WIDTH), BF16 if finalize else F32),
                   jax.ShapeDtypeStruct((bn, A_HEADS, A_DV, A_DK), F32)],
        scratch_shapes=[pltpu.VMEM((A_DV, A_DK), F32), pltpu.VMEM((rows, A_DV), F32)],
        compiler_params=_params("parallel", "parallel", "arbitrary"),
        name="hgrn2_scan_bwd" if reverse else "hgrn2_scan_fwd",
    )(*args)


def _post_mix(acc, x_ref, g1_ref, lnw_ref, lnb_ref, sc2_ref, sh2_ref, rw_ref, rb_ref,
              x1_ref, h2_ref, ti_ref, tg_ref):
    r = DN_ALPHA * x_ref[0] + g1_ref[0] * acc
    mu = jnp.mean(r, axis=-1, keepdims=True)
    rc = r - mu
    var = jnp.mean(rc * rc, axis=-1, keepdims=True)
    x1 = rc * lax.rsqrt(var + EPS) * lnw_ref[...] + lnb_ref[...]
    x1_ref[0] = x1
    h2 = x1 * (1.0 + sc2_ref[0]) + sh2_ref[0]
    h2_ref[...] = h2.astype(h2_ref.dtype)
    logits = lax.dot_general(rw_ref[...], h2.astype(BF16), _NT, preferred_element_type=F32) + rb_ref[...]
    iota = lax.broadcasted_iota(jnp.int32, logits.shape, 0)
    vals, idxs = [], []
    cur = logits
    for _ in range(TOP_K):
        m = jnp.max(cur, axis=0, keepdims=True)
        ik = jnp.min(jnp.where(cur == m, iota, N_EXPERTS), axis=0, keepdims=True)
        vals.append(m)
        idxs.append(ik)
        cur = jnp.where(iota == ik, -jnp.inf, cur)
    tv = jnp.concatenate(vals, axis=0)
    e = jnp.exp(tv - tv[0:1])
    tg_ref[...] = e / jnp.sum(e, axis=0, keepdims=True)
    ti_ref[...] = jnp.concatenate(idxs, axis=0)


def _outproj_even_kernel(a_ref, u_ref, gb_ref, gc_ref, up_ref, gcp_ref, un_ref, gcn_ref, cw_ref, wo_ref,
                         x_ref, g1_ref, lnw_ref, lnb_ref, sc2_ref, sh2_ref, rw_ref, rb_ref, alias_ref,
                         x1_ref, h2_ref, ti_ref, tg_ref):
    del alias_ref
    i = pl.program_id(1)
    bm = u_ref.shape[1]
    z = gc_ref[0].astype(F32) * u_ref[0].astype(F32)
    z_before = gcp_ref[0, HALO - 1:HALO, :].astype(F32) * up_ref[0, HALO - 1:HALO, :].astype(F32)
    z_after = gcn_ref[0, 0:1, :].astype(F32) * un_ref[0, 0:1, :].astype(F32)
    z_before = jnp.where(i == 0, 0.0, z_before)
    z_after = jnp.where(i == pl.num_programs(1) - 1, 0.0, z_after)
    row = lax.broadcasted_iota(jnp.int32, z.shape, 0)
    z_prev = jnp.where(row == 0, z_before, pltpu.roll(z, 1, axis=0))
    z_next = jnp.where(row == bm - 1, z_after, pltpu.roll(z, bm - 1, axis=0))
    zc = cw_ref[0:1, :] * z_prev + cw_ref[1:2, :] * z + cw_ref[2:3, :] * z_next
    b_mix = (gb_ref[0].astype(F32) * zc).astype(BF16)
    acc = (jnp.dot(a_ref[0], wo_ref[0:A_WIDTH, :], preferred_element_type=F32)
           + jnp.dot(b_mix, wo_ref[A_WIDTH:, :], preferred_element_type=F32))
    _post_mix(acc, x_ref, g1_ref, lnw_ref, lnb_ref, sc2_ref, sh2_ref, rw_ref, rb_ref,
              x1_ref, h2_ref, ti_ref, tg_ref)


def _outproj_odd_kernel(c_ref, u_ref, v_ref, dw_ref, db_ref, spw_ref, spb_ref, wo_ref,
                        x_ref, g1_ref, lnw_ref, lnb_ref, sc2_ref, sh2_ref, rw_ref, rb_ref, alias_ref,
                        x1_ref, h2_ref, ti_ref, tg_ref):
    del alias_ref
    bm = u_ref.shape[1]
    u = _gelu(u_ref[0].astype(F32))
    v = _gelu(v_ref[0].astype(F32))
    cols = []
    for g in range(D_GROUPS):
        lo_c, hi_c = g * D_GDIM, (g + 1) * D_GDIM
        vg = v[:, lo_c:hi_c]
        mu = jnp.mean(vg, axis=-1, keepdims=True)
        vc = vg - mu
        var = jnp.mean(vc * vc, axis=-1, keepdims=True)
        vn = (vc * lax.rsqrt(var + EPS) * dw_ref[:, lo_c:hi_c] + db_ref[:, lo_c:hi_c]).astype(BF16)
        parts = []
        for ch in range(bm // D_CHUNK):
            sv = jnp.dot(spw_ref[g], vn[ch * D_CHUNK:(ch + 1) * D_CHUNK], preferred_element_type=F32)
            parts.append(sv + spb_ref[:, g:g + 1])
        cols.append(u[:, lo_c:hi_c] * jnp.concatenate(parts, axis=0))
    d_mix = jnp.concatenate(cols, axis=1).astype(BF16)
    acc = (jnp.dot(c_ref[0].astype(BF16), wo_ref[0:C_WIDTH, :], preferred_element_type=F32)
           + jnp.dot(d_mix, wo_ref[C_WIDTH:, :], preferred_element_type=F32))
    _post_mix(acc, x_ref, g1_ref, lnw_ref, lnb_ref, sc2_ref, sh2_ref, rw_ref, rb_ref,
              x1_ref, h2_ref, ti_ref, tg_ref)


def _outproj(kind, mixer_args, mixer_specs, wo, x, mods, mod_row, lnw, lnb, rw_t, rb, tok_state, tok_off, bm):
    bn, t, d = x.shape
    nt = t // bm
    g1, sc2, sh2 = mods
    h2_all, ti_all, tg_all = tok_state
    ntok = h2_all.shape[0]
    off = tok_off // bm
    mod_spec = pl.BlockSpec((1, 1, d), lambda b, i: (mod_row(b), 0, 0))
    vec_spec = pl.BlockSpec((1, d), lambda b, i: (0, 0))
    in_specs = list(mixer_specs) + [
        pl.BlockSpec((d, d), lambda b, i: (0, 0)),
        pl.BlockSpec((1, bm, d), lambda b, i: (b, i, 0)),
        mod_spec, vec_spec, vec_spec, mod_spec, mod_spec,
        pl.BlockSpec((N_EXPERTS, d), lambda b, i: (0, 0)),
        pl.BlockSpec((N_EXPERTS, 1), lambda b, i: (0, 0)),
        pl.BlockSpec(memory_space=pl.ANY), pl.BlockSpec(memory_space=pl.ANY), pl.BlockSpec(memory_space=pl.ANY)]
    n_in = len(in_specs)
    kernel = _outproj_even_kernel if kind == "even" else _outproj_odd_kernel

    def body(*refs):
        ins, outs = refs[:n_in], refs[n_in:]
        kernel(*ins[:n_in - 3], ins[n_in - 3:], *outs)

    return pl.pallas_call(
        body,
        grid=(bn, nt),
        in_specs=in_specs,
        out_specs=[pl.BlockSpec((1, bm, d), lambda b, i: (b, i, 0)),
                   pl.BlockSpec((bm, d), lambda b, i: (off + b * nt + i, 0)),
                   pl.BlockSpec((TOP_K, bm), lambda b, i: (0, off + b * nt + i)),
                   pl.BlockSpec((TOP_K, bm), lambda b, i: (0, off + b * nt + i))],
        out_shape=[jax.ShapeDtypeStruct((bn, t, d), F32),
                   jax.ShapeDtypeStruct((ntok, d), BF16),
                   jax.ShapeDtypeStruct((TOP_K, ntok), jnp.int32),
                   jax.ShapeDtypeStruct((TOP_K, ntok), F32)],
        input_output_aliases={n_in - 3: 1, n_in - 2: 2, n_in - 1: 3},
        compiler_params=_params("parallel", "parallel"),
        name="out_proj_" + kind,
    )(*mixer_args, wo, x, g1, lnw.reshape(1, d), lnb.reshape(1, d), sc2, sh2, rw_t, rb.reshape(N_EXPERTS, 1),
      h2_all, ti_all, tg_all)


def _outproj_even(a_mix, zin, conv_w, **kw):
    bm = kw["bm"]
    t = zin.shape[1]
    hb = bm // HALO
    last = t // HALO - 1
    cur = lambda sec: pl.BlockSpec((1, bm, B_WIDTH), lambda b, i: (b, i, sec))
    before = lambda sec: pl.BlockSpec((1, HALO, B_WIDTH), lambda b, i: (b, jnp.maximum(i * hb - 1, 0), sec))
    after = lambda sec: pl.BlockSpec((1, HALO, B_WIDTH), lambda b, i: (b, jnp.minimum((i + 1) * hb, last), sec))
    specs = [pl.BlockSpec((1, bm, A_WIDTH), lambda b, i: (b, i, 0)), cur(5), cur(6), cur(7),
             before(5), before(7), after(5), after(7),
             pl.BlockSpec((CONV_W, B_WIDTH), lambda b, i: (0, 0))]
    return _outproj("even", [a_mix, zin, zin, zin, zin, zin, zin, zin, conv_w], specs, **kw)


def _outproj_odd(c_mix, zin, d_norm_w, d_norm_b, sp_w, sp_b, **kw):
    bm = kw["bm"]
    cur = lambda sec: pl.BlockSpec((1, bm, D_WIDTH), lambda b, i: (b, i, sec))
    specs = [pl.BlockSpec((1, bm, C_WIDTH), lambda b, i: (b, i, 0)), cur(1), cur(2),
             pl.BlockSpec((1, D_WIDTH), lambda b, i: (0, 0)), pl.BlockSpec((1, D_WIDTH), lambda b, i: (0, 0)),
             pl.BlockSpec((D_GROUPS, D_CHUNK, D_CHUNK), lambda b, i: (0, 0, 0)),
             pl.BlockSpec((D_CHUNK, D_GROUPS), lambda b, i: (0, 0))]
    return _outproj("odd", [c_mix, zin, zin, d_norm_w.reshape(1, D_WIDTH), d_norm_b.reshape(1, D_WIDTH),
                            sp_w.astype(BF16), sp_b.T], specs, **kw)


def _dft_mats(n):
    ang = 2.0 * np.pi * np.outer(np.arange(n), np.arange(n)) / n
    return np.cos(ang), np.sin(ang)


def _fourier_cw_kernel(z_ref, nw_ref, cs_ref, kc_ref, ks_ref, p_ref, q_ref, ps_ref, qs_ref):
    tile = z_ref.shape[1]
    half = 256
    grows = tile // GRID_W
    for hs in range(tile // half):
        z = z_ref[0, hs * half:(hs + 1) * half, :].astype(F32)
        a_parts, b_parts = [], []
        for g in range(C_GROUPS):
            zg = z[:, g * C_GROUP:(g + 1) * C_GROUP]
            zg = zg * lax.rsqrt(jnp.mean(zg * zg, axis=-1, keepdims=True) + EPS) * nw_ref[:, g * C_GROUP:(g + 1) * C_GROUP]
            ab = jnp.dot(zg.astype(BF16), cs_ref[...], preferred_element_type=F32)
            a_parts.append(ab[:, :C_GROUP])
            b_parts.append(ab[:, C_GROUP:])
        ab = jnp.concatenate(a_parts + b_parts, axis=1).astype(BF16)
        m1 = jnp.dot(kc_ref[...], ab, preferred_element_type=F32)
        m2 = jnp.dot(ks_ref[...], ab, preferred_element_type=F32)
        pv = m1[:, :C_WIDTH] - m2[:, C_WIDTH:]
        qv = m2[:, :C_WIDTH] + m1[:, C_WIDTH:]
        for g in range(C_GROUPS):
            ps_ref[g, hs * half:(hs + 1) * half, :] = pv[:, g * C_GROUP:(g + 1) * C_GROUP]
            qs_ref[g, hs * half:(hs + 1) * half, :] = qv[:, g * C_GROUP:(g + 1) * C_GROUP]
    for k2 in range(GRID_W):
        for g in range(C_GROUPS):
            p_ref[0, k2, :, g * C_GROUP:(g + 1) * C_GROUP] = ps_ref[g, pl.ds(k2, grows, stride=GRID_W), :]
            q_ref[0, k2, :, g * C_GROUP:(g + 1) * C_GROUP] = qs_ref[g, pl.ds(k2, grows, stride=GRID_W), :]


def _fourier_r_kernel(p_ref, q_ref, c_ref, s_ref, o_ref, ys_ref):
    nj = p_ref.shape[1]
    rows = p_ref.shape[2]
    for j in range(nj):
        y = (jnp.dot(c_ref[...], p_ref[0, j].astype(BF16), preferred_element_type=F32)
             + jnp.dot(s_ref[...], q_ref[0, j].astype(BF16), preferred_element_type=F32))
        for g in range(C_GROUPS):
            ys_ref[g, j * rows:(j + 1) * rows, :] = y[:, g * C_GROUP:(g + 1) * C_GROUP]

    def put(k1, carry):
        for g in range(C_GROUPS):
            o_ref[0, k1, :, g * C_GROUP:(g + 1) * C_GROUP] = ys_ref[g, pl.ds(k1, nj, stride=rows), :]
        return carry

    lax.fori_loop(0, rows, put, 0)


def _fourier_mix(zin, c_norm_w):
    bn, t, _ = zin.shape
    rows = t // GRID_W
    tile = 512
    grows = tile // GRID_W
    c3, s3 = _dft_mats(C_GROUP)
    c2, s2 = _dft_mats(GRID_W)
    c1, s1 = _dft_mats(rows)
    scale = 1.0 / math.sqrt(rows * GRID_W * C_GROUP)
    cs3 = jnp.asarray(np.concatenate([c3, s3], axis=1), BF16)
    eye = np.eye(256 // GRID_W)
    kc2 = jnp.asarray(np.kron(eye, c2), BF16)
    ks2 = jnp.asarray(np.kron(eye, s2), BF16)
    c1s = jnp.asarray(c1 * scale, BF16)
    s1s = jnp.asarray(-s1 * scale, BF16)
    pq_shape = jax.ShapeDtypeStruct((bn, GRID_W, rows, C_WIDTH), F32)
    pq_spec = pl.BlockSpec((1, GRID_W, grows, C_WIDTH), lambda b, i: (b, 0, i, 0))
    p, q = pl.pallas_call(
        _fourier_cw_kernel,
        grid=(bn, t // tile),
        in_specs=[pl.BlockSpec((1, tile, C_WIDTH), lambda b, i: (b, i, 0)),
                  pl.BlockSpec((1, C_WIDTH), lambda b, i: (0, 0)),
                  pl.BlockSpec((C_GROUP, 2 * C_GROUP), lambda b, i: (0, 0)),
                  pl.BlockSpec((256, 256), lambda b, i: (0, 0)),
                  pl.BlockSpec((256, 256), lambda b, i: (0, 0))],
        out_specs=[pq_spec, pq_spec],
        out_shape=[pq_shape, pq_shape],
        scratch_shapes=[pltpu.VMEM((C_GROUPS, tile, C_GROUP), F32), pltpu.VMEM((C_GROUPS, tile, C_GROUP), F32)],
        compiler_params=_params("parallel", "parallel"),
        name="fourier_chan_col",
    )(zin, c_norm_w.reshape(1, C_WIDTH), cs3, kc2, ks2)
    nj = 8
    in_spec = pl.BlockSpec((1, nj, rows, C_WIDTH), lambda b, j: (b, j, 0, 0))
    mat_spec = pl.BlockSpec((rows, rows), lambda b, j: (0, 0))
    y = pl.pallas_call(
        _fourier_r_kernel,
        grid=(bn, GRID_W // nj),
        in_specs=[in_spec, in_spec, mat_spec, mat_spec],
        out_specs=pl.BlockSpec((1, rows, nj, C_WIDTH), lambda b, j: (b, 0, j, 0)),
        out_shape=jax.ShapeDtypeStruct((bn, rows, GRID_W, C_WIDTH), F32),
        scratch_shapes=[pltpu.VMEM((C_GROUPS, nj * rows, C_GROUP), F32)],
        compiler_params=_params("parallel", "parallel"),
        name="fourier_rows",
    )(p, q, c1s, s1s)
    return y.reshape(bn, t, C_WIDTH)


def _rank_kernel(ti_ref, rank_ref, cnt_ref, run_ref):
    @pl.when(pl.program_id(0) == 0)
    def _():
        run_ref[...] = jnp.zeros_like(run_ref)

    ti = ti_ref[...]
    tb = ti.shape[1]
    eio = lax.broadcasted_iota(jnp.int32, (N_EXPERTS, tb), 0)
    hot = [eio == ti[k:k + 1, :] for k in range(TOP_K)]
    occ = sum(h.astype(F32) for h in hot)
    si = lax.broadcasted_iota(jnp.int32, (tb, tb), 0)
    ti_ = lax.broadcasted_iota(jnp.int32, (tb, tb), 1)
    before = jnp.where(si < ti_, 1.0, 0.0).astype(BF16)
    seen = jnp.dot(occ.astype(BF16), before, preferred_element_type=F32) + run_ref[:, 0:1]
    rank_ref[...] = jnp.concatenate(
        [jnp.sum(jnp.where(h, seen, 0.0), axis=0, keepdims=True) for h in hot], axis=0).astype(jnp.int32)
    run_ref[...] = run_ref[...] + jnp.sum(occ, axis=1, keepdims=True)
    cnt_ref[...] = run_ref[...].astype(jnp.int32)


def _ranks(top_i):
    ntok = top_i.shape[1]
    tb = ROUTE_TILE if ntok % ROUTE_TILE == 0 else 256
    rank, cnt = pl.pallas_call(
        _rank_kernel,
        grid=(ntok // tb,),
        in_specs=[pl.BlockSpec((TOP_K, tb), lambda i: (0, i))],
        out_specs=[pl.BlockSpec((TOP_K, tb), lambda i: (0, i)),
                   pl.BlockSpec((N_EXPERTS, 128), lambda i: (0, 0))],
        out_shape=[jax.ShapeDtypeStruct((TOP_K, ntok), jnp.int32),
                   jax.ShapeDtypeStruct((N_EXPERTS, 128), jnp.int32)],
        scratch_shapes=[pltpu.VMEM((N_EXPERTS, 128), F32)],
        compiler_params=_params("arbitrary"),
        name="route_rank",
    )(top_i)
    return rank, cnt[:, 0]


def _w1_split_kernel(w_ref, g_ref, l_ref, s_ref):
    kdim, cols = w_ref.shape[2], w_ref.shape[3]
    half = cols // 2
    nk = kdim // LANES
    t = w_ref[0, 0].T
    for kc in range(nk):
        s_ref[kc] = t[:, kc * LANES:(kc + 1) * LANES]
    even = jnp.concatenate([s_ref[kc, pl.ds(0, half, stride=2), :] for kc in range(nk)], axis=1)
    odd = jnp.concatenate([s_ref[kc, pl.ds(1, half, stride=2), :] for kc in range(nk)], axis=1)
    g_ref[0] = even.T.astype(BF16)
    l_ref[0] = odd.T.astype(BF16)


def _w1_split(w1_all, layer):
    _, e, d, f2 = w1_all.shape
    cols = 256
    out = jax.ShapeDtypeStruct((e, d, f2 // 2), BF16)
    out_spec = pl.BlockSpec((1, d, cols // 2), lambda i, c: (i, 0, c))
    return pl.pallas_call(
        _w1_split_kernel,
        grid=(e, f2 // cols),
        in_specs=[pl.BlockSpec((1, 1, d, cols), lambda i, c: (layer, i, 0, c))],
        out_specs=[out_spec, out_spec],
        out_shape=[out, out],
        scratch_shapes=[pltpu.VMEM((d // LANES, cols, LANES), F32)],
        compiler_params=_params("parallel", "parallel"),
        name="moe_w1_split",
    )(w1_all)


def _cast_kernel(w_ref, o_ref):
    o_ref[...] = w_ref[0].astype(o_ref.dtype)


def _w2_cast(w2_all, layer):
    _, e, f, d = w2_all.shape
    return pl.pallas_call(
        _cast_kernel,
        grid=(e,),
        in_specs=[pl.BlockSpec((1, 1, f, d), lambda i: (layer, i, 0, 0))],
        out_specs=pl.BlockSpec((1, f, d), lambda i: (i, 0, 0)),
        out_shape=jax.ShapeDtypeStruct((e, f, d), BF16),
        compiler_params=_params("parallel"),
        name="moe_w2_cast",
    )(w2_all)


def _expert_kernel(be_ref, x_ref, w1g_ref, w1l_ref, w2_ref, b1g_ref, b1l_ref, b2_ref, y_ref):
    del be_ref
    rows = x_ref.shape[0] // ROW_TILES
    x = _load_row_tiles(x_ref, rows).astype(BF16)
    glu = jnp.minimum(jnp.dot(x, w1g_ref[0], preferred_element_type=F32) + b1g_ref[0], SWIGLU_LIMIT)
    lin = jnp.clip(jnp.dot(x, w1l_ref[0], preferred_element_type=F32) + b1l_ref[0], -SWIGLU_LIMIT, SWIGLU_LIMIT)
    act = glu * _sigmoid(SWIGLU_ALPHA * glu) * (lin + 1.0)
    _store_row_tiles(y_ref, jnp.dot(act.astype(BF16), w2_ref[0], preferred_element_type=F32) + b2_ref[0])


def _experts(x_sorted, block_expert, w1g, w1l, w2, b1g, b1l, b2):
    d, f = w1g.shape[1], w1g.shape[2]
    nb = block_expert.shape[0]
    buf_rows = MOE_ROWS * ROW_TILES
    wspec = lambda k, n: pl.BlockSpec((1, k, n), lambda j, be: (be[j], 0, 0))
    row_spec = pl.BlockSpec((buf_rows, LANES), lambda j, be: (j, 0))
    grid_spec = pltpu.PrefetchScalarGridSpec(
        num_scalar_prefetch=1,
        grid=(nb,),
        in_specs=[row_spec, wspec(d, f), wspec(d, f), wspec(f, d), wspec(1, f), wspec(1, f), wspec(1, d)],
        out_specs=row_spec)
    return pl.pallas_call(
        _expert_kernel,
        grid_spec=grid_spec,
        out_shape=jax.ShapeDtypeStruct(x_sorted.shape, F32),
        compiler_params=_params("parallel"),
        name="moe_experts",
    )(block_expert, x_sorted, w1g, w1l, w2, b1g, b1l, b2)


def _dispatch_kernel(cnt_ref, start_ref, dst_ref, h_ref, xs_hbm, rowbuf, zero_ref, sem, zsem, *, nt, nslots):
    i = pl.program_id(0)
    tb = h_ref.shape[0]
    slot = i % 2

    def landed(buf):
        for _ in range(TOP_K):
            pltpu.make_async_copy(rowbuf.at[buf], xs_hbm.at[pl.ds(0, tb * ROW_TILES), :], sem.at[buf]).wait()

    @pl.when(i >= 2)
    def _():
        landed(slot)

    _store_row_tiles(rowbuf.at[slot], h_ref[...].astype(F32))
    for r in range(tb):
        for k in range(TOP_K):
            dst = pl.multiple_of(dst_ref[0, k, r] * ROW_TILES, ROW_TILES)
            pltpu.make_async_copy(rowbuf.at[slot, pl.ds(r * ROW_TILES, ROW_TILES), :],
                                  xs_hbm.at[pl.ds(dst, ROW_TILES), :], sem.at[slot]).start(priority=k % 2)

    @pl.when(i == nt - 1)
    def _():
        landed(slot)
        if nt > 1:
            landed(1 - slot)
        zero_ref[...] = jnp.zeros_like(zero_ref)

        def fill(lo, hi):
            def body(row, carry):
                pltpu.make_async_copy(
                    zero_ref, xs_hbm.at[pl.ds(pl.multiple_of(row * ROW_TILES, ROW_TILES), ROW_TILES), :], zsem).start()
                return carry
            lax.fori_loop(lo, hi, body, 0)

            def drain(row, carry):
                pltpu.make_async_copy(zero_ref, xs_hbm.at[pl.ds(0, ROW_TILES), :], zsem).wait()
                return carry
            lax.fori_loop(lo, hi, drain, 0)

        end = 0
        for e in range(N_EXPERTS):
            used = start_ref[e] + cnt_ref[e]
            end = start_ref[e] + (cnt_ref[e] + MOE_ROWS - 1) // MOE_ROWS * MOE_ROWS
            fill(used, end)
        fill(end, nslots)


def _dispatch(h2, dest3, counts, pad_starts, nslots):
    ntok, d = h2.shape
    tb = COMBINE_TILE
    nt = ntok // tb
    grid_spec = pltpu.PrefetchScalarGridSpec(
        num_scalar_prefetch=2,
        grid=(nt,),
        in_specs=[pl.BlockSpec((1, TOP_K, tb), lambda i, c, s: (i, 0, 0), memory_space=pltpu.SMEM),
                  pl.BlockSpec((tb, d), lambda i, c, s: (i, 0))],
        out_specs=pl.BlockSpec(memory_space=pl.ANY),
        scratch_shapes=[pltpu.VMEM((2, tb * ROW_TILES, LANES), F32), pltpu.VMEM((ROW_TILES, LANES), F32),
                        pltpu.SemaphoreType.DMA((2,)), pltpu.SemaphoreType.DMA(())])
    return pl.pallas_call(
        functools.partial(_dispatch_kernel, nt=nt, nslots=nslots),
        grid_spec=grid_spec,
        out_shape=jax.ShapeDtypeStruct((nslots * ROW_TILES, LANES), F32),
        compiler_params=_params("arbitrary"),
        name="moe_dispatch",
    )(counts, pad_starts, dest3, h2)


def _combine_kernel(dst_ref, dstn_ref, y_hbm, x1_ref, gate_ref, g2_ref, lnw_ref, lnb_ref, o_ref, ybuf, sem, *, nt):
    i = pl.program_id(0)
    tb = x1_ref.shape[0]
    slot = i % 2

    def row_copy(src, buf, k, r):
        rows = y_hbm.at[pl.ds(pl.multiple_of(src * ROW_TILES, ROW_TILES), ROW_TILES), :]
        return pltpu.make_async_copy(rows, ybuf.at[buf, k, pl.ds(r * ROW_TILES, ROW_TILES), :], sem.at[buf])

    def gather(idx_ref, buf):
        for r in range(tb):
            for k in range(TOP_K):
                row_copy(idx_ref[0, k, r], buf, k, r).start(priority=k % 2)

    @pl.when(i == 0)
    def _():
        gather(dst_ref, 0)

    @pl.when(i + 1 < nt)
    def _():
        gather(dstn_ref, 1 - slot)

    for k in range(TOP_K):
        pltpu.make_async_copy(y_hbm.at[pl.ds(0, tb * ROW_TILES), :], ybuf.at[slot, k], sem.at[slot]).wait()

    y = sum(gate_ref[:, k:k + 1] * _load_row_tiles(ybuf.at[slot, k], tb) for k in range(TOP_K))
    r = DN_ALPHA * x1_ref[...] + g2_ref[0] * y
    mu = jnp.mean(r, axis=-1, keepdims=True)
    rc = r - mu
    var = jnp.mean(rc * rc, axis=-1, keepdims=True)
    o_ref[...] = rc * lax.rsqrt(var + EPS) * lnw_ref[...] + lnb_ref[...]


def _combine(x1, y_sorted, dest3, gates_t, tok_off, g2, mod_row, lnw, lnb):
    bn, t, d = x1.shape
    n = bn * t
    tb = COMBINE_TILE
    nt = n // tb
    off = tok_off // tb
    per_b = t // tb
    out = pl.pallas_call(
        functools.partial(_combine_kernel, nt=nt),
        grid=(nt,),
        in_specs=[pl.BlockSpec((1, TOP_K, tb), lambda i: (off + i, 0, 0), memory_space=pltpu.SMEM),
                  pl.BlockSpec((1, TOP_K, tb), lambda i: (off + jnp.minimum(i + 1, nt - 1), 0, 0),
                               memory_space=pltpu.SMEM),
                  pl.BlockSpec(memory_space=pl.ANY),
                  pl.BlockSpec((tb, d), lambda i: (i, 0)),
                  pl.BlockSpec((tb, TOP_K), lambda i: (off + i, 0)),
                  pl.BlockSpec((1, 1, d), lambda i: (mod_row(i // per_b), 0, 0)),
                  pl.BlockSpec((1, d), lambda i: (0, 0)),
                  pl.BlockSpec((1, d), lambda i: (0, 0))],
        out_specs=pl.BlockSpec((tb, d), lambda i: (i, 0)),
        out_shape=jax.ShapeDtypeStruct((n, d), F32),
        scratch_shapes=[pltpu.VMEM((2, TOP_K, tb * ROW_TILES, LANES), F32), pltpu.SemaphoreType.DMA((2,))],
        compiler_params=_params("arbitrary"),
        name="moe_combine",
    )(dest3, dest3, y_sorted, x1.reshape(n, d), gates_t, g2, lnw.reshape(1, d), lnb.reshape(1, d))
    return out.reshape(bn, t, d)


def _dest_kernel(ti_ref, rank_ref, start_ref, dest_ref):
    ti = ti_ref[...]
    tb = ti.shape[1]
    eio = lax.broadcasted_iota(jnp.int32, (N_EXPERTS, tb), 0)
    start = start_ref[:, 0:1].astype(F32)
    base = jnp.concatenate(
        [jnp.sum(jnp.where(eio == ti[k:k + 1, :], start, 0.0), axis=0, keepdims=True) for k in range(TOP_K)], axis=0)
    dest = base.astype(jnp.int32) + rank_ref[...]
    for c in range(tb // COMBINE_TILE):
        dest_ref[c] = dest[:, c * COMBINE_TILE:(c + 1) * COMBINE_TILE]


def _moe_layout(top_i):
    ntok = top_i.shape[1]
    m = ntok * TOP_K
    rank, counts = _ranks(top_i)
    padded = (counts + MOE_ROWS - 1) // MOE_ROWS * MOE_ROWS
    pad_ends = jnp.cumsum(padded)
    pad_starts = pad_ends - padded
    tb = ROUTE_TILE if ntok % ROUTE_TILE == 0 else 256
    ntiles = ntok // COMBINE_TILE
    dest = pl.pallas_call(
        _dest_kernel,
        grid=(ntok // tb,),
        in_specs=[pl.BlockSpec((TOP_K, tb), lambda i: (0, i)),
                  pl.BlockSpec((TOP_K, tb), lambda i: (0, i)),
                  pl.BlockSpec((N_EXPERTS, LANES), lambda i: (0, 0))],
        out_specs=pl.BlockSpec((tb // COMBINE_TILE, TOP_K, COMBINE_TILE), lambda i: (i, 0, 0)),
        out_shape=jax.ShapeDtypeStruct((ntiles, TOP_K, COMBINE_TILE), jnp.int32),
        compiler_params=_params("parallel"),
        name="route_dest",
    )(top_i, rank, jnp.broadcast_to(pad_starts[:, None], (N_EXPERTS, LANES)))
    nb = (m + N_EXPERTS * (MOE_ROWS - 1) + MOE_ROWS - 1) // MOE_ROWS
    block_starts = jnp.arange(nb, dtype=jnp.int32) * MOE_ROWS
    block_expert = jnp.minimum(jnp.sum(block_starts[:, None] >= pad_ends[None, :], axis=1),
                               N_EXPERTS - 1).astype(jnp.int32)
    return dest, counts, pad_starts.astype(jnp.int32), block_expert, nb * MOE_ROWS


def kernel(x, c, ctx, c_ctx, lower_bounds_fwd, lower_bounds_bwd, ada_w, ada_b, w_in_even, a_norm_w, conv_w,
           w_in_odd, c_norm_w, d_norm_w, d_norm_b, spatial_w, spatial_b, w_out, ln_mix_w, ln_mix_b,
           ln_ffn_w, ln_ffn_b, router_w, router_b, moe_w1, moe_b1, moe_w2, moe_b2):
    bn, t, d = x.shape
    tc = ctx.shape[1]
    n_lat = bn * t
    n_ctx = bn * tc
    ctx_row = bn
    lat_row = lambda b: b
    ctx_mod_row = lambda b: ctx_row

    lb_f_all = jnp.cumsum(jax.nn.softmax(lower_bounds_fwd.astype(F32), axis=0), axis=0)
    lb_b_all = jnp.cumsum(jax.nn.softmax(lower_bounds_bwd.astype(F32), axis=0), axis=0)

    cond = jnp.zeros((MOD_ROWS, d), F32).at[:bn].set(c).at[ctx_row].set(c_ctx)
    mods = _ada(cond, ada_w, ada_b).reshape(DEPTH, MOD_ROWS, 6, 1, d)

    h_ctx = ctx
    for l in range(DEPTH):
        last = l == DEPTH - 1
        sh1, sc1, g1, sh2, sc2, g2 = (mods[l, :, j] for j in range(6))
        wo = w_out[l].astype(BF16)
        rw_t = router_w[l].T.astype(BF16)
        ntok = n_lat if last else n_lat + n_ctx
        tok_state = (jnp.zeros((ntok, d), BF16), jnp.zeros((TOP_K, ntok), jnp.int32),
                     jnp.zeros((TOP_K, ntok), F32))
        post = dict(wo=wo, lnw=ln_mix_w[l], lnb=ln_mix_b[l], rw_t=rw_t, rb=router_b[l], bm=ROW_TILE)
        if l % 2 == 0:
            e = l // 2
            w_in = w_in_even[e].astype(BF16)
            zin_c = _inproj(h_ctx, sc1, sh1, w_in, ctx_mod_row, bm=min(ROW_TILE, tc))
            zin = _inproj(x, sc1, sh1, w_in, lat_row, bm=ROW_TILE)
            zeros = jnp.zeros((bn, A_HEADS, A_DV, A_DK), F32)
            oc_f, sc_f = _scan(zin_c, lb_f_all[l], zeros, reverse=False)
            a_ctx, sc_b = _scan(zin_c, lb_b_all[l], zeros, reverse=True, o_prev=oc_f, norm_w=a_norm_w[e])
            o_f, _ = _scan(zin, lb_f_all[l], sc_f, reverse=False)
            a_lat, _ = _scan(zin, lb_b_all[l], sc_b, reverse=True, o_prev=o_f, norm_w=a_norm_w[e])
            x1, *tok_state = _outproj_even(a_lat, zin, conv_w[e], x=x, mods=(g1, sc2, sh2), mod_row=lat_row,
                                           tok_state=tok_state, tok_off=0, **post)
            if not last:
                hc1, *tok_state = _outproj_even(a_ctx, zin_c, conv_w[e], x=h_ctx, mods=(g1, sc2, sh2),
                                                mod_row=ctx_mod_row, tok_state=tok_state, tok_off=n_lat, **post)
        else:
            o = l // 2
            w_in = w_in_odd[o].astype(BF16)
            odd = dict(d_norm_w=d_norm_w[o], d_norm_b=d_norm_b[o], sp_w=spatial_w[o], sp_b=spatial_b[o])
            zin = _inproj(x, sc1, sh1, w_in, lat_row, bm=ROW_TILE)
            c_mix = _fourier_mix(zin, c_norm_w[o])
            x1, *tok_state = _outproj_odd(c_mix, zin, x=x, mods=(g1, sc2, sh2), mod_row=lat_row,
                                          tok_state=tok_state, tok_off=0, **odd, **post)
            if not last:
                raise NotImplementedError("an odd layer that is not the last needs the context Fourier mixer")
        h2_all, top_i, top_g = tok_state
        dest, counts, pad_starts, block_expert, nslots = _moe_layout(top_i)
        x_sorted = _dispatch(h2_all, dest, counts, pad_starts, nslots)
        w1g, w1l = _w1_split(moe_w1, l)
        y_sorted = _experts(
            x_sorted, block_expert, w1g, w1l, _w2_cast(moe_w2, l),
            moe_b1[l][:, None, 0::2], moe_b1[l][:, None, 1::2], moe_b2[l][:, None, :])
        gates_t = top_g.T
        x = _combine(x1, y_sorted, dest, gates_t, 0, g2, lat_row, ln_ffn_w[l], ln_ffn_b[l])
        if not last:
            h_ctx = _combine(hc1, y_sorted, dest, gates_t, n_lat, g2, ctx_mod_row, ln_ffn_w[l], ln_ffn_b[l])
    return x
```

```python
import functools
import math

import numpy as np
import jax
import jax.numpy as jnp
from jax import lax
from jax.experimental import pallas as pl
from jax.experimental.pallas import tpu as pltpu

F32 = jnp.float32
BF16 = jnp.bfloat16

D_MODEL = 1024
DEPTH = 2
GRID_W = 64
A_HEADS = 4
A_DK = 128
A_DV = 128
A_KW = A_HEADS * A_DK
A_WIDTH = A_HEADS * A_DV
B_WIDTH = D_MODEL - A_WIDTH
CONV_W = 3
C_GROUPS = 4
C_GROUP = 128
C_WIDTH = C_GROUPS * C_GROUP
D_WIDTH = D_MODEL - C_WIDTH
D_GROUPS = 4
D_GDIM = D_WIDTH // D_GROUPS
D_CHUNK = 128
SCAN_CHUNK = 32
EVEN_IN = 3 * A_KW + 2 * A_WIDTH + 3 * B_WIDTH
ODD_IN = C_WIDTH + 2 * D_WIDTH
N_EXPERTS = 32
TOP_K = 4
D_EXPERT = 1024
SWIGLU_ALPHA = 1.702
SWIGLU_LIMIT = 7.0
DN_ALPHA = (2 * DEPTH) ** 0.25
EPS = 1e-5

MOD_ROWS = 8
SCAN_ROWS = 256
SCAN_HEADS = 4
ROW_TILE = 256
MOE_ROWS = 256
ROUTE_TILE = 512
COMBINE_TILE = 128
HALO = 16
LANES = 128
ROW_TILES = D_MODEL // LANES

_NT = (((1,), (1,)), ((), ()))
_TN = (((0,), (0,)), ((), ()))


def _sigmoid(a):
    return 1.0 / (1.0 + jnp.exp(-a))


def _silu(a):
    return a * _sigmoid(a)


def _gelu(a):
    return 0.5 * a * (1.0 + lax.erf(a * (1.0 / math.sqrt(2.0))))


def _store_row_tiles(ref, val):
    rows = val.shape[0]
    for c in range(ROW_TILES):
        ref[pl.ds(c, rows, stride=ROW_TILES), :] = val[:, c * LANES:(c + 1) * LANES]


def _load_row_tiles(ref, rows):
    return jnp.concatenate([ref[pl.ds(c, rows, stride=ROW_TILES), :] for c in range(ROW_TILES)], axis=1)


def _params(*sem):
    return pltpu.CompilerParams(dimension_semantics=sem, vmem_limit_bytes=56 * 1024 * 1024)


def _ada_kernel(c_ref, w_ref, b_ref, o_ref):
    s = _silu(c_ref[...])
    o_ref[0] = jnp.dot(s, w_ref[0], preferred_element_type=F32, precision=lax.Precision.HIGHEST) + b_ref[0]


def _ada(cond, ada_w, ada_b):
    d = cond.shape[1]
    n6 = ada_w.shape[2]
    tn = 1536
    return pl.pallas_call(
        _ada_kernel,
        grid=(DEPTH, n6 // tn),
        in_specs=[pl.BlockSpec((MOD_ROWS, d), lambda l, n: (0, 0)),
                  pl.BlockSpec((1, d, tn), lambda l, n: (l, 0, n)),
                  pl.BlockSpec((1, 1, tn), lambda l, n: (l, 0, n))],
        out_specs=pl.BlockSpec((1, MOD_ROWS, tn), lambda l, n: (l, 0, n)),
        out_shape=jax.ShapeDtypeStruct((DEPTH, MOD_ROWS, n6), F32),
        compiler_params=_params("parallel", "parallel"),
        name="ada_mod",
    )(cond, ada_w, ada_b.reshape(DEPTH, 1, n6))


def _inproj_kernel(x_ref, sc_ref, sh_ref, w_ref, o_ref, *, tn):
    h = (x_ref[0] * (1.0 + sc_ref[0]) + sh_ref[0]).astype(BF16)
    for n in range(o_ref.shape[2] // tn):
        o_ref[0, :, n * tn:(n + 1) * tn] = jnp.dot(
            h, w_ref[:, n * tn:(n + 1) * tn], preferred_element_type=F32).astype(o_ref.dtype)


def _inproj(x, sc, sh, w, mod_row, bm):
    bn, t, d = x.shape
    nout = w.shape[1]
    mod_spec = pl.BlockSpec((1, 1, d), lambda b, i: (mod_row(b), 0, 0))
    return pl.pallas_call(
        functools.partial(_inproj_kernel, tn=512),
        grid=(bn, t // bm),
        in_specs=[pl.BlockSpec((1, bm, d), lambda b, i: (b, i, 0)), mod_spec, mod_spec,
                  pl.BlockSpec((d, nout), lambda b, i: (0, 0))],
        out_specs=pl.BlockSpec((1, bm, nout), lambda b, i: (b, i, 0)),
        out_shape=jax.ShapeDtypeStruct((bn, t, nout), BF16),
        compiler_params=_params("parallel", "parallel"),
        name="in_proj",
    )(x, sc, sh, w)


def _scan_kernel(*refs, reverse, finalize):
    if finalize:
        (q_ref, z_ref, v_ref, lb_ref, s0_ref, g_ref, op_ref, nw_ref, o_ref, sfin_ref, st_ref, oacc_ref) = refs
    else:
        (q_ref, z_ref, v_ref, lb_ref, s0_ref, o_ref, sfin_ref, st_ref, oacc_ref) = refs
    rows = q_ref.shape[1]
    heads = q_ref.shape[2] // A_DK
    nchunk = rows // SCAN_CHUNK

    @pl.when(pl.program_id(2) == 0)
    def _():
        st_ref[...] = s0_ref[0]

    q = _silu(q_ref[0].astype(F32))
    lb = lb_ref[...]
    f = lb + (1.0 - lb) * _sigmoid(z_ref[0].astype(F32))
    logf = jnp.log(f)
    k = 1.0 - f
    v = v_ref[0]

    ri = lax.broadcasted_iota(jnp.int32, (rows, rows), 0)
    ci = lax.broadcasted_iota(jnp.int32, (rows, rows), 1)
    same = (ri // SCAN_CHUNK) == (ci // SCAN_CHUNK)
    tri = jnp.where(same & ((ci >= ri) if reverse else (ci <= ri)), 1.0, 0.0).astype(BF16)
    hi = logf.astype(BF16)
    lo = (logf - hi.astype(F32)).astype(BF16)
    b = jnp.dot(tri, hi, preferred_element_type=F32) + jnp.dot(tri, lo, preferred_element_type=F32)

    q_dec = (q * jnp.exp(b)).astype(BF16)
    k_inv = (k * jnp.exp(-b)).astype(BF16)
    cr = lax.broadcasted_iota(jnp.int32, (SCAN_CHUNK, SCAN_CHUNK), 0)
    cc = lax.broadcasted_iota(jnp.int32, (SCAN_CHUNK, SCAN_CHUNK), 1)
    causal = (cc >= cr) if reverse else (cc <= cr)

    st = [st_ref[h] for h in range(heads)]
    order = range(nchunk - 1, -1, -1) if reverse else range(nchunk)
    for c in order:
        lo_r, hi_r = c * SCAN_CHUNK, (c + 1) * SCAN_CHUNK
        tot_r = lo_r if reverse else hi_r - 1
        b_tot = b[tot_r:tot_r + 1, :]
        ke = (k[lo_r:hi_r] * jnp.exp(b_tot - b[lo_r:hi_r])).astype(BF16)
        dec = jnp.exp(b_tot)
        for h in range(heads):
            hk = slice(h * A_DK, (h + 1) * A_DK)
            hv = slice(h * A_DV, (h + 1) * A_DV)
            qd = q_dec[lo_r:hi_r, hk]
            vc = v[lo_r:hi_r, hv]
            scores = lax.dot_general(qd, k_inv[lo_r:hi_r, hk], _NT, preferred_element_type=F32)
            scores = jnp.where(causal, scores, 0.0).astype(BF16)
            oacc_ref[lo_r:hi_r, hv] = (jnp.dot(scores, vc, preferred_element_type=F32)
                                       + lax.dot_general(qd, st[h].astype(BF16), _NT, preferred_element_type=F32))
            st[h] = st[h] * dec[:, hk] + lax.dot_general(vc, ke[:, hk], _TN, preferred_element_type=F32)
    for h in range(heads):
        st_ref[h] = st[h]
        sfin_ref[0, h] = st[h]

    o = oacc_ref[...]
    if finalize:
        o = o + op_ref[0]
        gate = nw_ref[...] * _silu(g_ref[0].astype(F32))
        for h in range(heads):
            hv = slice(h * A_DV, (h + 1) * A_DV)
            oh = o[:, hv]
            oh = oh * lax.rsqrt(jnp.mean(oh * oh, axis=-1, keepdims=True) + EPS)
            o_ref[0, :, hv] = (oh * gate[:, hv]).astype(o_ref.dtype)
    else:
        o_ref[0] = o.astype(o_ref.dtype)


def _scan(zin, lb, s0, *, reverse, o_prev=None, norm_w=None):
    bn, t, _ = zin.shape
    finalize = o_prev is not None
    rows = min(SCAN_ROWS, t)
    nb = t // rows
    pos = (lambda n: nb - 1 - n) if reverse else (lambda n: n)
    hp = SCAN_HEADS
    groups = A_HEADS // hp
    hw = hp * A_DK

    def col(section):
        return pl.BlockSpec((1, rows, hw), lambda b, h, n: (b, pos(n), section * groups + h))

    state_spec = pl.BlockSpec((1, hp, A_DV, A_DK), lambda b, h, n: (b, h, 0, 0))
    in_specs = [col(0), col(2 if reverse else 1), col(3),
                pl.BlockSpec((1, hw), lambda b, h, n: (0, h)), state_spec]
    args = [zin, zin, zin, lb.reshape(1, A_KW), s0]
    if finalize:
        in_specs += [col(4),
                     pl.BlockSpec((1, rows, hw), lambda b, h, n: (b, pos(n), h)),
                     pl.BlockSpec((1, hw), lambda b, h, n: (0, h))]
        args += [zin, o_prev, norm_w.reshape(1, A_WIDTH)]
    return pl.pallas_call(
        functools.partial(_scan_kernel, reverse=reverse, finalize=finalize),
        grid=(bn, groups, nb),
        in_specs=in_specs,
        out_specs=[pl.BlockSpec((1, rows, hw), lambda b, h, n: (b, pos(n), h)), state_spec],
        out_shape=[jax.ShapeDtypeStruct((bn, t, A_WIDTH), BF16 if finalize else F32),
                   jax.ShapeDtypeStruct((bn, A_HEADS, A_DV, A_DK), F32)],
        scratch_shapes=[pltpu.VMEM((hp, A_DV, A_DK), F32), pltpu.VMEM((rows, hw), F32)],
        compiler_params=_params("parallel", "parallel", "arbitrary"),
        name="hgrn2_scan_bwd" if reverse else "hgrn2_scan_fwd",
    )(*args)


def _post_mix(acc, x_ref, g1_ref, lnw_ref, lnb_ref, sc2_ref, sh2_ref, rw_ref, rb_ref,
              x1_ref, h2_ref, ti_ref, tg_ref):
    r = DN_ALPHA * x_ref[0] + g1_ref[0] * acc
    mu = jnp.mean(r, axis=-1, keepdims=True)
    rc = r - mu
    var = jnp.mean(rc * rc, axis=-1, keepdims=True)
    x1 = rc * lax.rsqrt(var + EPS) * lnw_ref[...] + lnb_ref[...]
    x1_ref[0] = x1
    h2 = x1 * (1.0 + sc2_ref[0]) + sh2_ref[0]
    h2_ref[...] = h2.astype(h2_ref.dtype)
    logits = lax.dot_general(rw_ref[...], h2.astype(BF16), _NT, preferred_element_type=F32) + rb_ref[...]
    iota = lax.broadcasted_iota(jnp.int32, logits.shape, 0)
    vals, idxs = [], []
    cur = logits
    for _ in range(TOP_K):
        m = jnp.max(cur, axis=0, keepdims=True)
        ik = jnp.min(jnp.where(cur == m, iota, N_EXPERTS), axis=0, keepdims=True)
        vals.append(m)
        idxs.append(ik)
        cur = jnp.where(iota == ik, -jnp.inf, cur)
    tv = jnp.concatenate(vals, axis=0)
    e = jnp.exp(tv - tv[0:1])
    tg_ref[...] = e / jnp.sum(e, axis=0, keepdims=True)
    ti_ref[...] = jnp.concatenate(idxs, axis=0)


def _outproj_even_kernel(a_ref, u_ref, gb_ref, gc_ref, up_ref, gcp_ref, un_ref, gcn_ref, cw_ref, wo_ref,
                         x_ref, g1_ref, lnw_ref, lnb_ref, sc2_ref, sh2_ref, rw_ref, rb_ref, alias_ref,
                         x1_ref, h2_ref, ti_ref, tg_ref):
    del alias_ref
    i = pl.program_id(1)
    bm = u_ref.shape[1]
    z = gc_ref[0].astype(F32) * u_ref[0].astype(F32)
    z_before = gcp_ref[0, HALO - 1:HALO, :].astype(F32) * up_ref[0, HALO - 1:HALO, :].astype(F32)
    z_after = gcn_ref[0, 0:1, :].astype(F32) * un_ref[0, 0:1, :].astype(F32)
    z_before = jnp.where(i == 0, 0.0, z_before)
    z_after = jnp.where(i == pl.num_programs(1) - 1, 0.0, z_after)
    row = lax.broadcasted_iota(jnp.int32, z.shape, 0)
    z_prev = jnp.where(row == 0, z_before, pltpu.roll(z, 1, axis=0))
    z_next = jnp.where(row == bm - 1, z_after, pltpu.roll(z, bm - 1, axis=0))
    zc = cw_ref[0:1, :] * z_prev + cw_ref[1:2, :] * z + cw_ref[2:3, :] * z_next
    b_mix = (gb_ref[0].astype(F32) * zc).astype(BF16)
    acc = (jnp.dot(a_ref[0], wo_ref[0:A_WIDTH, :], preferred_element_type=F32)
           + jnp.dot(b_mix, wo_ref[A_WIDTH:, :], preferred_element_type=F32))
    _post_mix(acc, x_ref, g1_ref, lnw_ref, lnb_ref, sc2_ref, sh2_ref, rw_ref, rb_ref,
              x1_ref, h2_ref, ti_ref, tg_ref)


def _outproj_odd_kernel(c_ref, u_ref, v_ref, dw_ref, db_ref, spw_ref, spb_ref, wo_ref,
                        x_ref, g1_ref, lnw_ref, lnb_ref, sc2_ref, sh2_ref, rw_ref, rb_ref, alias_ref,
                        x1_ref, h2_ref, ti_ref, tg_ref):
    del alias_ref
    bm = u_ref.shape[1]
    u = _gelu(u_ref[0].astype(F32))
    v = _gelu(v_ref[0].astype(F32))
    cols = []
    for g in range(D_GROUPS):
        lo_c, hi_c = g * D_GDIM, (g + 1) * D_GDIM
        vg = v[:, lo_c:hi_c]
        mu = jnp.mean(vg, axis=-1, keepdims=True)
        vc = vg - mu
        var = jnp.mean(vc * vc, axis=-1, keepdims=True)
        vn = (vc * lax.rsqrt(var + EPS) * dw_ref[:, lo_c:hi_c] + db_ref[:, lo_c:hi_c]).astype(BF16)
        parts = []
        for ch in range(bm // D_CHUNK):
            sv = jnp.dot(spw_ref[g], vn[ch * D_CHUNK:(ch + 1) * D_CHUNK], preferred_element_type=F32)
            parts.append(sv + spb_ref[:, g:g + 1])
        cols.append(u[:, lo_c:hi_c] * jnp.concatenate(parts, axis=0))
    d_mix = jnp.concatenate(cols, axis=1).astype(BF16)
    acc = (jnp.dot(c_ref[0].astype(BF16), wo_ref[0:C_WIDTH, :], preferred_element_type=F32)
           + jnp.dot(d_mix, wo_ref[C_WIDTH:, :], preferred_element_type=F32))
    _post_mix(acc, x_ref, g1_ref, lnw_ref, lnb_ref, sc2_ref, sh2_ref, rw_ref, rb_ref,
              x1_ref, h2_ref, ti_ref, tg_ref)


def _outproj(kind, mixer_args, mixer_specs, wo, x, mods, mod_row, lnw, lnb, rw_t, rb, tok_state, tok_off, bm):
    bn, t, d = x.shape
    nt = t // bm
    g1, sc2, sh2 = mods
    h2_all, ti_all, tg_all = tok_state
    ntok = h2_all.shape[0]
    off = tok_off // bm
    mod_spec = pl.BlockSpec((1, 1, d), lambda b, i: (mod_row(b), 0, 0))
    vec_spec = pl.BlockSpec((1, d), lambda b, i: (0, 0))
    in_specs = list(mixer_specs) + [
        pl.BlockSpec((d, d), lambda b, i: (0, 0)),
        pl.BlockSpec((1, bm, d), lambda b, i: (b, i, 0)),
        mod_spec, vec_spec, vec_spec, mod_spec, mod_spec,
        pl.BlockSpec((N_EXPERTS, d), lambda b, i: (0, 0)),
        pl.BlockSpec((N_EXPERTS, 1), lambda b, i: (0, 0)),
        pl.BlockSpec(memory_space=pl.ANY), pl.BlockSpec(memory_space=pl.ANY), pl.BlockSpec(memory_space=pl.ANY)]
    n_in = len(in_specs)
    kernel = _outproj_even_kernel if kind == "even" else _outproj_odd_kernel

    def body(*refs):
        ins, outs = refs[:n_in], refs[n_in:]
        kernel(*ins[:n_in - 3], ins[n_in - 3:], *outs)

    return pl.pallas_call(
        body,
        grid=(bn, nt),
        in_specs=in_specs,
        out_specs=[pl.BlockSpec((1, bm, d), lambda b, i: (b, i, 0)),
                   pl.BlockSpec((bm, d), lambda b, i: (off + b * nt + i, 0)),
                   pl.BlockSpec((TOP_K, bm), lambda b, i: (0, off + b * nt + i)),
                   pl.BlockSpec((TOP_K, bm), lambda b, i: (0, off + b * nt + i))],
        out_shape=[jax.ShapeDtypeStruct((bn, t, d), F32),
                   jax.ShapeDtypeStruct((ntok, d), BF16),
                   jax.ShapeDtypeStruct((TOP_K, ntok), jnp.int32),
                   jax.ShapeDtypeStruct((TOP_K, ntok), F32)],
        input_output_aliases={n_in - 3: 1, n_in - 2: 2, n_in - 1: 3},
        compiler_params=_params("parallel", "parallel"),
        name="out_proj_" + kind,
    )(*mixer_args, wo, x, g1, lnw.reshape(1, d), lnb.reshape(1, d), sc2, sh2, rw_t, rb.reshape(N_EXPERTS, 1),
      h2_all, ti_all, tg_all)


def _outproj_even(a_mix, zin, conv_w, **kw):
    bm = kw["bm"]
    t = zin.shape[1]
    hb = bm // HALO
    last = t // HALO - 1
    cur = lambda sec: pl.BlockSpec((1, bm, B_WIDTH), lambda b, i: (b, i, sec))
    before = lambda sec: pl.BlockSpec((1, HALO, B_WIDTH), lambda b, i: (b, jnp.maximum(i * hb - 1, 0), sec))
    after = lambda sec: pl.BlockSpec((1, HALO, B_WIDTH), lambda b, i: (b, jnp.minimum((i + 1) * hb, last), sec))
    specs = [pl.BlockSpec((1, bm, A_WIDTH), lambda b, i: (b, i, 0)), cur(5), cur(6), cur(7),
             before(5), before(7), after(5), after(7),
             pl.BlockSpec((CONV_W, B_WIDTH), lambda b, i: (0, 0))]
    return _outproj("even", [a_mix, zin, zin, zin, zin, zin, zin, zin, conv_w], specs, **kw)


def _outproj_odd(c_mix, zin, d_norm_w, d_norm_b, sp_w, sp_b, **kw):
    bm = kw["bm"]
    cur = lambda sec: pl.BlockSpec((1, bm, D_WIDTH), lambda b, i: (b, i, sec))
    specs = [pl.BlockSpec((1, bm, C_WIDTH), lambda b, i: (b, i, 0)), cur(1), cur(2),
             pl.BlockSpec((1, D_WIDTH), lambda b, i: (0, 0)), pl.BlockSpec((1, D_WIDTH), lambda b, i: (0, 0)),
             pl.BlockSpec((D_GROUPS, D_CHUNK, D_CHUNK), lambda b, i: (0, 0, 0)),
             pl.BlockSpec((D_CHUNK, D_GROUPS), lambda b, i: (0, 0))]
    return _outproj("odd", [c_mix, zin, zin, d_norm_w.reshape(1, D_WIDTH), d_norm_b.reshape(1, D_WIDTH),
                            sp_w.astype(BF16), sp_b.T], specs, **kw)


def _dft_mats(n):
    ang = 2.0 * np.pi * np.outer(np.arange(n), np.arange(n)) / n
    return np.cos(ang), np.sin(ang)


def _fourier_cw_kernel(z_ref, nw_ref, cs_ref, kc_ref, ks_ref, p_ref, q_ref, ps_ref, qs_ref):
    tile = z_ref.shape[1]
    half = 256
    grows = tile // GRID_W
    for hs in range(tile // half):
        z = z_ref[0, hs * half:(hs + 1) * half, :].astype(F32)
        a_parts, b_parts = [], []
        for g in range(C_GROUPS):
            zg = z[:, g * C_GROUP:(g + 1) * C_GROUP]
            zg = zg * lax.rsqrt(jnp.mean(zg * zg, axis=-1, keepdims=True) + EPS) * nw_ref[:, g * C_GROUP:(g + 1) * C_GROUP]
            ab = jnp.dot(zg.astype(BF16), cs_ref[...], preferred_element_type=F32)
            a_parts.append(ab[:, :C_GROUP])
            b_parts.append(ab[:, C_GROUP:])
        ab = jnp.concatenate(a_parts + b_parts, axis=1).astype(BF16)
        m1 = jnp.dot(kc_ref[...], ab, preferred_element_type=F32)
        m2 = jnp.dot(ks_ref[...], ab, preferred_element_type=F32)
        pv = m1[:, :C_WIDTH] - m2[:, C_WIDTH:]
        qv = m2[:, :C_WIDTH] + m1[:, C_WIDTH:]
        for g in range(C_GROUPS):
            ps_ref[g, hs * half:(hs + 1) * half, :] = pv[:, g * C_GROUP:(g + 1) * C_GROUP]
            qs_ref[g, hs * half:(hs + 1) * half, :] = qv[:, g * C_GROUP:(g + 1) * C_GROUP]
    for k2 in range(GRID_W):
        for g in range(C_GROUPS):
            p_ref[0, k2, :, g * C_GROUP:(g + 1) * C_GROUP] = ps_ref[g, pl.ds(k2, grows, stride=GRID_W), :]
            q_ref[0, k2, :, g * C_GROUP:(g + 1) * C_GROUP] = qs_ref[g, pl.ds(k2, grows, stride=GRID_W), :]


def _fourier_r_kernel(p_ref, q_ref, c_ref, s_ref, o_ref, ys_ref):
    nj = p_ref.shape[1]
    rows = p_ref.shape[2]
    for j in range(nj):
        y = (jnp.dot(c_ref[...], p_ref[0, j].astype(BF16), preferred_element_type=F32)
             + jnp.dot(s_ref[...], q_ref[0, j].astype(BF16), preferred_element_type=F32))
        for g in range(C_GROUPS):
            ys_ref[g, j * rows:(j + 1) * rows, :] = y[:, g * C_GROUP:(g + 1) * C_GROUP]

    def put(k1, carry):
        for g in range(C_GROUPS):
            o_ref[0, k1, :, g * C_GROUP:(g + 1) * C_GROUP] = ys_ref[g, pl.ds(k1, nj, stride=rows), :]
        return carry

    lax.fori_loop(0, rows, put, 0)


def _fourier_mix(zin, c_norm_w):
    bn, t, _ = zin.shape
    rows = t // GRID_W
    tile = 512
    grows = tile // GRID_W
    c3, s3 = _dft_mats(C_GROUP)
    c2, s2 = _dft_mats(GRID_W)
    c1, s1 = _dft_mats(rows)
    scale = 1.0 / math.sqrt(rows * GRID_W * C_GROUP)
    cs3 = jnp.asarray(np.concatenate([c3, s3], axis=1), BF16)
    eye = np.eye(256 // GRID_W)
    kc2 = jnp.asarray(np.kron(eye, c2), BF16)
    ks2 = jnp.asarray(np.kron(eye, s2), BF16)
    c1s = jnp.asarray(c1 * scale, BF16)
    s1s = jnp.asarray(-s1 * scale, BF16)
    pq_shape = jax.ShapeDtypeStruct((bn, GRID_W, rows, C_WIDTH), F32)
    pq_spec = pl.BlockSpec((1, GRID_W, grows, C_WIDTH), lambda b, i: (b, 0, i, 0))
    p, q = pl.pallas_call(
        _fourier_cw_kernel,
        grid=(bn, t // tile),
        in_specs=[pl.BlockSpec((1, tile, C_WIDTH), lambda b, i: (b, i, 0)),
                  pl.BlockSpec((1, C_WIDTH), lambda b, i: (0, 0)),
                  pl.BlockSpec((C_GROUP, 2 * C_GROUP), lambda b, i: (0, 0)),
                  pl.BlockSpec((256, 256), lambda b, i: (0, 0)),
                  pl.BlockSpec((256, 256), lambda b, i: (0, 0))],
        out_specs=[pq_spec, pq_spec],
        out_shape=[pq_shape, pq_shape],
        scratch_shapes=[pltpu.VMEM((C_GROUPS, tile, C_GROUP), F32), pltpu.VMEM((C_GROUPS, tile, C_GROUP), F32)],
        compiler_params=_params("parallel", "parallel"),
        name="fourier_chan_col",
    )(zin, c_norm_w.reshape(1, C_WIDTH), cs3, kc2, ks2)
    nj = 8
    in_spec = pl.BlockSpec((1, nj, rows, C_WIDTH), lambda b, j: (b, j, 0, 0))
    mat_spec = pl.BlockSpec((rows, rows), lambda b, j: (0, 0))
    y = pl.pallas_call(
        _fourier_r_kernel,
        grid=(bn, GRID_W // nj),
        in_specs=[in_spec, in_spec, mat_spec, mat_spec],
        out_specs=pl.BlockSpec((1, rows, nj, C_WIDTH), lambda b, j: (b, 0, j, 0)),
        out_shape=jax.ShapeDtypeStruct((bn, rows, GRID_W, C_WIDTH), F32),
        scratch_shapes=[pltpu.VMEM((C_GROUPS, nj * rows, C_GROUP), F32)],
        compiler_params=_params("parallel", "parallel"),
        name="fourier_rows",
    )(p, q, c1s, s1s)
    return y.reshape(bn, t, C_WIDTH)


def _rank_kernel(ti_ref, rank_ref, cnt_ref, run_ref):
    @pl.when(pl.program_id(0) == 0)
    def _():
        run_ref[...] = jnp.zeros_like(run_ref)

    ti = ti_ref[...]
    tb = ti.shape[1]
    eio = lax.broadcasted_iota(jnp.int32, (N_EXPERTS, tb), 0)
    hot = [eio == ti[k:k + 1, :] for k in range(TOP_K)]
    occ = sum(h.astype(F32) for h in hot)
    si = lax.broadcasted_iota(jnp.int32, (tb, tb), 0)
    ti_ = lax.broadcasted_iota(jnp.int32, (tb, tb), 1)
    before = jnp.where(si < ti_, 1.0, 0.0).astype(BF16)
    seen = jnp.dot(occ.astype(BF16), before, preferred_element_type=F32) + run_ref[:, 0:1]
    rank_ref[...] = jnp.concatenate(
        [jnp.sum(jnp.where(h, seen, 0.0), axis=0, keepdims=True) for h in hot], axis=0).astype(jnp.int32)
    run_ref[...] = run_ref[...] + jnp.sum(occ, axis=1, keepdims=True)
    cnt_ref[...] = run_ref[...].astype(jnp.int32)


def _ranks(top_i):
    ntok = top_i.shape[1]
    tb = ROUTE_TILE if ntok % ROUTE_TILE == 0 else 256
    rank, cnt = pl.pallas_call(
        _rank_kernel,
        grid=(ntok // tb,),
        in_specs=[pl.BlockSpec((TOP_K, tb), lambda i: (0, i))],
        out_specs=[pl.BlockSpec((TOP_K, tb), lambda i: (0, i)),
                   pl.BlockSpec((N_EXPERTS, 128), lambda i: (0, 0))],
        out_shape=[jax.ShapeDtypeStruct((TOP_K, ntok), jnp.int32),
                   jax.ShapeDtypeStruct((N_EXPERTS, 128), jnp.int32)],
        scratch_shapes=[pltpu.VMEM((N_EXPERTS, 128), F32)],
        compiler_params=_params("arbitrary"),
        name="route_rank",
    )(top_i)
    return rank, cnt[:, 0]


def _w1_split_kernel(w_ref, g_ref, l_ref, s_ref):
    kdim, cols = w_ref.shape[2], w_ref.shape[3]
    half = cols // 2
    nk = kdim // LANES
    t = w_ref[0, 0].T
    for kc in range(nk):
        s_ref[kc] = t[:, kc * LANES:(kc + 1) * LANES]
    even = jnp.concatenate([s_ref[kc, pl.ds(0, half, stride=2), :] for kc in range(nk)], axis=1)
    odd = jnp.concatenate([s_ref[kc, pl.ds(1, half, stride=2), :] for kc in range(nk)], axis=1)
    g_ref[0] = even.T.astype(BF16)
    l_ref[0] = odd.T.astype(BF16)


def _w1_split(w1_all, layer):
    _, e, d, f2 = w1_all.shape
    cols = 1024
    out = jax.ShapeDtypeStruct((e, d, f2 // 2), BF16)
    out_spec = pl.BlockSpec((1, d, cols // 2), lambda i, c: (i, 0, c))
    return pl.pallas_call(
        _w1_split_kernel,
        grid=(e, f2 // cols),
        in_specs=[pl.BlockSpec((1, 1, d, cols), lambda i, c: (layer, i, 0, c))],
        out_specs=[out_spec, out_spec],
        out_shape=[out, out],
        scratch_shapes=[pltpu.VMEM((d // LANES, cols, LANES), F32)],
        compiler_params=_params("parallel", "parallel"),
        name="moe_w1_split",
    )(w1_all)


def _cast_kernel(w_ref, o_ref):
    o_ref[...] = w_ref[0].astype(o_ref.dtype)


def _w2_cast(w2_all, layer):
    _, e, f, d = w2_all.shape
    return pl.pallas_call(
        _cast_kernel,
        grid=(e,),
        in_specs=[pl.BlockSpec((1, 1, f, d), lambda i: (layer, i, 0, 0))],
        out_specs=pl.BlockSpec((1, f, d), lambda i: (i, 0, 0)),
        out_shape=jax.ShapeDtypeStruct((e, f, d), BF16),
        compiler_params=_params("parallel"),
        name="moe_w2_cast",
    )(w2_all)


def _expert_kernel(be_ref, used_ref, x_ref, w1g_ref, w1l_ref, w2_ref, b1g_ref, b1l_ref, b2_ref, y_ref):
    del be_ref
    rows = x_ref.shape[0] // ROW_TILES
    in_use = pl.program_id(0) < used_ref[0]

    @pl.when(in_use)
    def _():
        x = _load_row_tiles(x_ref, rows).astype(BF16)
        glu = jnp.minimum(jnp.dot(x, w1g_ref[0], preferred_element_type=F32) + b1g_ref[0], SWIGLU_LIMIT)
        lin = jnp.clip(jnp.dot(x, w1l_ref[0], preferred_element_type=F32) + b1l_ref[0],
                       -SWIGLU_LIMIT, SWIGLU_LIMIT)
        act = glu * _sigmoid(SWIGLU_ALPHA * glu) * (lin + 1.0)
        _store_row_tiles(y_ref, jnp.dot(act.astype(BF16), w2_ref[0], preferred_element_type=F32) + b2_ref[0])

    @pl.when(jnp.logical_not(in_use))
    def _():
        y_ref[...] = jnp.zeros_like(y_ref)


def _experts(x_sorted, block_expert, blocks_used, w1g, w1l, w2, b1g, b1l, b2):
    d, f = w1g.shape[1], w1g.shape[2]
    nb = block_expert.shape[0]
    buf_rows = MOE_ROWS * ROW_TILES
    wspec = lambda k, n: pl.BlockSpec((1, k, n), lambda j, be, nu: (be[j], 0, 0))
    row_spec = pl.BlockSpec((buf_rows, LANES), lambda j, be, nu: (j, 0))
    grid_spec = pltpu.PrefetchScalarGridSpec(
        num_scalar_prefetch=2,
        grid=(nb,),
        in_specs=[row_spec, wspec(d, f), wspec(d, f), wspec(f, d), wspec(1, f), wspec(1, f), wspec(1, d)],
        out_specs=row_spec)
    return pl.pallas_call(
        _expert_kernel,
        grid_spec=grid_spec,
        out_shape=jax.ShapeDtypeStruct(x_sorted.shape, F32),
        compiler_params=_params("parallel"),
        name="moe_experts",
    )(block_expert, blocks_used, x_sorted, w1g, w1l, w2, b1g, b1l, b2)


def _dispatch_kernel(cnt_ref, start_ref, dst_ref, h_ref, xs_hbm, rowbuf, zero_ref, sem, zsem, *, nt, nslots):
    i = pl.program_id(0)
    tb = h_ref.shape[0]
    slot = i % 2

    def landed(buf):
        for _ in range(TOP_K):
            pltpu.make_async_copy(rowbuf.at[buf], xs_hbm.at[pl.ds(0, tb * ROW_TILES), :], sem.at[buf]).wait()

    @pl.when(i >= 2)
    def _():
        landed(slot)

    _store_row_tiles(rowbuf.at[slot], h_ref[...].astype(F32))
    for r in range(tb):
        for k in range(TOP_K):
            dst = pl.multiple_of(dst_ref[0, k, r] * ROW_TILES, ROW_TILES)
            pltpu.make_async_copy(rowbuf.at[slot, pl.ds(r * ROW_TILES, ROW_TILES), :],
                                  xs_hbm.at[pl.ds(dst, ROW_TILES), :], sem.at[slot]).start(priority=k % 2)

    @pl.when(i == nt - 1)
    def _():
        landed(slot)
        if nt > 1:
            landed(1 - slot)
        zero_ref[...] = jnp.zeros_like(zero_ref)

        def fill(lo, hi):
            def body(row, carry):
                pltpu.make_async_copy(
                    zero_ref, xs_hbm.at[pl.ds(pl.multiple_of(row * ROW_TILES, ROW_TILES), ROW_TILES), :], zsem).start()
                return carry
            lax.fori_loop(lo, hi, body, 0)

            def drain(row, carry):
                pltpu.make_async_copy(zero_ref, xs_hbm.at[pl.ds(0, ROW_TILES), :], zsem).wait()
                return carry
            lax.fori_loop(lo, hi, drain, 0)

        end = 0
        for e in range(N_EXPERTS):
            used = start_ref[e] + cnt_ref[e]
            end = start_ref[e] + (cnt_ref[e] + MOE_ROWS - 1) // MOE_ROWS * MOE_ROWS
            fill(used, end)
        fill(end, nslots)


def _dispatch(h2, dest3, counts, pad_starts, nslots):
    ntok, d = h2.shape
    tb = COMBINE_TILE
    nt = ntok // tb
    grid_spec = pltpu.PrefetchScalarGridSpec(
        num_scalar_prefetch=2,
        grid=(nt,),
        in_specs=[pl.BlockSpec((1, TOP_K, tb), lambda i, c, s: (i, 0, 0), memory_space=pltpu.SMEM),
                  pl.BlockSpec((tb, d), lambda i, c, s: (i, 0))],
        out_specs=pl.BlockSpec(memory_space=pl.ANY),
        scratch_shapes=[pltpu.VMEM((2, tb * ROW_TILES, LANES), F32), pltpu.VMEM((ROW_TILES, LANES), F32),
                        pltpu.SemaphoreType.DMA((2,)), pltpu.SemaphoreType.DMA(())])
    return pl.pallas_call(
        functools.partial(_dispatch_kernel, nt=nt, nslots=nslots),
        grid_spec=grid_spec,
        out_shape=jax.ShapeDtypeStruct((nslots * ROW_TILES, LANES), F32),
        compiler_params=_params("arbitrary"),
        name="moe_dispatch",
    )(counts, pad_starts, dest3, h2)


def _combine_kernel(dst_ref, dstn_ref, y_hbm, x1_ref, gate_ref, g2_ref, lnw_ref, lnb_ref, o_ref, ybuf, sem, *, nt):
    i = pl.program_id(0)
    tb = x1_ref.shape[0]
    slot = i % 2

    def row_copy(src, buf, k, r):
        rows = y_hbm.at[pl.ds(pl.multiple_of(src * ROW_TILES, ROW_TILES), ROW_TILES), :]
        return pltpu.make_async_copy(rows, ybuf.at[buf, k, pl.ds(r * ROW_TILES, ROW_TILES), :], sem.at[buf])

    def gather(idx_ref, buf):
        for r in range(tb):
            for k in range(TOP_K):
                row_copy(idx_ref[0, k, r], buf, k, r).start(priority=k % 2)

    @pl.when(i == 0)
    def _():
        gather(dst_ref, 0)

    @pl.when(i + 1 < nt)
    def _():
        gather(dstn_ref, 1 - slot)

    for k in range(TOP_K):
        pltpu.make_async_copy(y_hbm.at[pl.ds(0, tb * ROW_TILES), :], ybuf.at[slot, k], sem.at[slot]).wait()

    y = sum(gate_ref[:, k:k + 1] * _load_row_tiles(ybuf.at[slot, k], tb) for k in range(TOP_K))
    r = DN_ALPHA * x1_ref[...] + g2_ref[0] * y
    mu = jnp.mean(r, axis=-1, keepdims=True)
    rc = r - mu
    var = jnp.mean(rc * rc, axis=-1, keepdims=True)
    o_ref[...] = rc * lax.rsqrt(var + EPS) * lnw_ref[...] + lnb_ref[...]


def _combine(x1, y_sorted, dest3, gates_t, tok_off, g2, mod_row, lnw, lnb):
    bn, t, d = x1.shape
    n = bn * t
    tb = COMBINE_TILE
    nt = n // tb
    off = tok_off // tb
    per_b = t // tb
    out = pl.pallas_call(
        functools.partial(_combine_kernel, nt=nt),
        grid=(nt,),
        in_specs=[pl.BlockSpec((1, TOP_K, tb), lambda i: (off + i, 0, 0), memory_space=pltpu.SMEM),
                  pl.BlockSpec((1, TOP_K, tb), lambda i: (off + jnp.minimum(i + 1, nt - 1), 0, 0),
                               memory_space=pltpu.SMEM),
                  pl.BlockSpec(memory_space=pl.ANY),
                  pl.BlockSpec((tb, d), lambda i: (i, 0)),
                  pl.BlockSpec((tb, TOP_K), lambda i: (off + i, 0)),
                  pl.BlockSpec((1, 1, d), lambda i: (mod_row(i // per_b), 0, 0)),
                  pl.BlockSpec((1, d), lambda i: (0, 0)),
                  pl.BlockSpec((1, d), lambda i: (0, 0))],
        out_specs=pl.BlockSpec((tb, d), lambda i: (i, 0)),
        out_shape=jax.ShapeDtypeStruct((n, d), F32),
        scratch_shapes=[pltpu.VMEM((2, TOP_K, tb * ROW_TILES, LANES), F32), pltpu.SemaphoreType.DMA((2,))],
        compiler_params=_params("arbitrary"),
        name="moe_combine",
    )(dest3, dest3, y_sorted, x1.reshape(n, d), gates_t, g2, lnw.reshape(1, d), lnb.reshape(1, d))
    return out.reshape(bn, t, d)


def _dest_kernel(ti_ref, rank_ref, start_ref, dest_ref):
    ti = ti_ref[...]
    tb = ti.shape[1]
    eio = lax.broadcasted_iota(jnp.int32, (N_EXPERTS, tb), 0)
    start = start_ref[:, 0:1].astype(F32)
    base = jnp.concatenate(
        [jnp.sum(jnp.where(eio == ti[k:k + 1, :], start, 0.0), axis=0, keepdims=True) for k in range(TOP_K)], axis=0)
    dest = base.astype(jnp.int32) + rank_ref[...]
    for c in range(tb // COMBINE_TILE):
        dest_ref[c] = dest[:, c * COMBINE_TILE:(c + 1) * COMBINE_TILE]


def _moe_layout(top_i):
    ntok = top_i.shape[1]
    m = ntok * TOP_K
    rank, counts = _ranks(top_i)
    padded = (counts + MOE_ROWS - 1) // MOE_ROWS * MOE_ROWS
    pad_ends = jnp.cumsum(padded)
    pad_starts = pad_ends - padded
    tb = ROUTE_TILE if ntok % ROUTE_TILE == 0 else 256
    ntiles = ntok // COMBINE_TILE
    dest = pl.pallas_call(
        _dest_kernel,
        grid=(ntok // tb,),
        in_specs=[pl.BlockSpec((TOP_K, tb), lambda i: (0, i)),
                  pl.BlockSpec((TOP_K, tb), lambda i: (0, i)),
                  pl.BlockSpec((N_EXPERTS, LANES), lambda i: (0, 0))],
        out_specs=pl.BlockSpec((tb // COMBINE_TILE, TOP_K, COMBINE_TILE), lambda i: (i, 0, 0)),
        out_shape=jax.ShapeDtypeStruct((ntiles, TOP_K, COMBINE_TILE), jnp.int32),
        compiler_params=_params("parallel"),
        name="route_dest",
    )(top_i, rank, jnp.broadcast_to(pad_starts[:, None], (N_EXPERTS, LANES)))
    nb = (m + N_EXPERTS * (MOE_ROWS - 1) + MOE_ROWS - 1) // MOE_ROWS
    block_starts = jnp.arange(nb, dtype=jnp.int32) * MOE_ROWS
    block_expert = jnp.minimum(jnp.sum(block_starts[:, None] >= pad_ends[None, :], axis=1),
                               N_EXPERTS - 1).astype(jnp.int32)
    blocks_used = (pad_ends[-1:] // MOE_ROWS).astype(jnp.int32)
    return dest, counts, pad_starts.astype(jnp.int32), block_expert, blocks_used, nb * MOE_ROWS


def kernel(x, c, ctx, c_ctx, lower_bounds_fwd, lower_bounds_bwd, ada_w, ada_b, w_in_even, a_norm_w, conv_w,
           w_in_odd, c_norm_w, d_norm_w, d_norm_b, spatial_w, spatial_b, w_out, ln_mix_w, ln_mix_b,
           ln_ffn_w, ln_ffn_b, router_w, router_b, moe_w1, moe_b1, moe_w2, moe_b2):
    bn, t, d = x.shape
    tc = ctx.shape[1]
    n_lat = bn * t
    n_ctx = bn * tc
    ctx_row = bn
    lat_row = lambda b: b
    ctx_mod_row = lambda b: ctx_row

    lb_f_all = jnp.cumsum(jax.nn.softmax(lower_bounds_fwd.astype(F32), axis=0), axis=0)
    lb_b_all = jnp.cumsum(jax.nn.softmax(lower_bounds_bwd.astype(F32), axis=0), axis=0)

    cond = jnp.zeros((MOD_ROWS, d), F32).at[:bn].set(c).at[ctx_row].set(c_ctx)
    mods = _ada(cond, ada_w, ada_b).reshape(DEPTH, MOD_ROWS, 6, 1, d)

    h_ctx = ctx
    for l in range(DEPTH):
        last = l == DEPTH - 1
        sh1, sc1, g1, sh2, sc2, g2 = (mods[l, :, j] for j in range(6))
        wo = w_out[l].astype(BF16)
        rw_t = router_w[l].T.astype(BF16)
        ntok = n_lat if last else n_lat + n_ctx
        tok_state = (jnp.zeros((ntok, d), BF16), jnp.zeros((TOP_K, ntok), jnp.int32),
                     jnp.zeros((TOP_K, ntok), F32))
        post = dict(wo=wo, lnw=ln_mix_w[l], lnb=ln_mix_b[l], rw_t=rw_t, rb=router_b[l], bm=ROW_TILE)
        if l % 2 == 0:
            e = l // 2
            w_in = w_in_even[e].astype(BF16)
            zin_c = _inproj(h_ctx, sc1, sh1, w_in, ctx_mod_row, bm=min(ROW_TILE, tc))
            zin = _inproj(x, sc1, sh1, w_in, lat_row, bm=2 * ROW_TILE)
            zeros = jnp.zeros((bn, A_HEADS, A_DV, A_DK), F32)
            oc_f, sc_f = _scan(zin_c, lb_f_all[l], zeros, reverse=False)
            a_ctx, sc_b = _scan(zin_c, lb_b_all[l], zeros, reverse=True, o_prev=oc_f, norm_w=a_norm_w[e])
            o_f, _ = _scan(zin, lb_f_all[l], sc_f, reverse=False)
            a_lat, _ = _scan(zin, lb_b_all[l], sc_b, reverse=True, o_prev=o_f, norm_w=a_norm_w[e])
            x1, *tok_state = _outproj_even(a_lat, zin, conv_w[e], x=x, mods=(g1, sc2, sh2), mod_row=lat_row,
                                           tok_state=tok_state, tok_off=0, **post)
            if not last:
                hc1, *tok_state = _outproj_even(a_ctx, zin_c, conv_w[e], x=h_ctx, mods=(g1, sc2, sh2),
                                                mod_row=ctx_mod_row, tok_state=tok_state, tok_off=n_lat, **post)
        else:
            o = l // 2
            w_in = w_in_odd[o].astype(BF16)
            odd = dict(d_norm_w=d_norm_w[o], d_norm_b=d_norm_b[o], sp_w=spatial_w[o], sp_b=spatial_b[o])
            zin = _inproj(x, sc1, sh1, w_in, lat_row, bm=ROW_TILE)
            c_mix = _fourier_mix(zin, c_norm_w[o])
            x1, *tok_state = _outproj_odd(c_mix, zin, x=x, mods=(g1, sc2, sh2), mod_row=lat_row,
                                          tok_state=tok_state, tok_off=0, **odd, **post)
            if not last:
                raise NotImplementedError("an odd layer that is not the last needs the context Fourier mixer")
        h2_all, top_i, top_g = tok_state
        dest, counts, pad_starts, block_expert, blocks_used, nslots = _moe_layout(top_i)
        x_sorted = _dispatch(h2_all, dest, counts, pad_starts, nslots)
        w1g, w1l = _w1_split(moe_w1, l)
        y_sorted = _experts(
            x_sorted, block_expert, blocks_used, w1g, w1l, _w2_cast(moe_w2, l),
            moe_b1[l][:, None, 0::2], moe_b1[l][:, None, 1::2], moe_b2[l][:, None, :])
        gates_t = top_g.T
        x = _combine(x1, y_sorted, dest, gates_t, 0, g2, lat_row, ln_ffn_w[l], ln_ffn_b[l])
        if not last:
            h_ctx = _combine(hc1, y_sorted, dest, gates_t, n_lat, g2, ctx_mod_row, ln_ffn_w[l], ln_ffn_b[l])
    return x
```

```python
import functools
import math

import numpy as np
import jax
import jax.numpy as jnp
from jax import lax
from jax.experimental import pallas as pl
from jax.experimental.pallas import tpu as pltpu

F32 = jnp.float32
BF16 = jnp.bfloat16

D_MODEL = 1024
DEPTH = 2
GRID_W = 64
A_HEADS = 4
A_DK = 128
A_DV = 128
A_KW = A_HEADS * A_DK
A_WIDTH = A_HEADS * A_DV
B_WIDTH = D_MODEL - A_WIDTH
CONV_W = 3
C_GROUPS = 4
C_GROUP = 128
C_WIDTH = C_GROUPS * C_GROUP
D_WIDTH = D_MODEL - C_WIDTH
D_GROUPS = 4
D_GDIM = D_WIDTH // D_GROUPS
D_CHUNK = 128
SCAN_CHUNK = 32
EVEN_IN = 3 * A_KW + 2 * A_WIDTH + 3 * B_WIDTH
ODD_IN = C_WIDTH + 2 * D_WIDTH
N_EXPERTS = 32
TOP_K = 4
D_EXPERT = 1024
SWIGLU_ALPHA = 1.702
SWIGLU_LIMIT = 7.0
DN_ALPHA = (2 * DEPTH) ** 0.25
EPS = 1e-5

MOD_ROWS = 8
SCAN_ROWS = 256
SCAN_HEADS = 4
ROW_TILE = 512
SUB_TILE = 256
MOE_ROWS = 512
ROUTE_TILE = 512
COMBINE_TILE = 128
HALO = 16
LANES = 128
ROW_TILES = D_MODEL // LANES

_NT = (((1,), (1,)), ((), ()))
_TN = (((0,), (0,)), ((), ()))


def _sigmoid(a):
    return 1.0 / (1.0 + jnp.exp(-a))


def _silu(a):
    return a * _sigmoid(a)


def _gelu(a):
    return 0.5 * a * (1.0 + lax.erf(a * (1.0 / math.sqrt(2.0))))


def _store_row_tiles(ref, val):
    rows = val.shape[0]
    for c in range(ROW_TILES):
        ref[pl.ds(c, rows, stride=ROW_TILES), :] = val[:, c * LANES:(c + 1) * LANES]


def _load_row_tiles(ref, rows):
    return jnp.concatenate([ref[pl.ds(c, rows, stride=ROW_TILES), :] for c in range(ROW_TILES)], axis=1)


def _params(*sem):
    return pltpu.CompilerParams(dimension_semantics=sem, vmem_limit_bytes=56 * 1024 * 1024)


def _ada_kernel(c_ref, w_ref, b_ref, o_ref):
    s = _silu(c_ref[...])
    o_ref[0] = jnp.dot(s, w_ref[0], preferred_element_type=F32, precision=lax.Precision.HIGHEST) + b_ref[0]


def _ada(cond, ada_w, ada_b):
    d = cond.shape[1]
    n6 = ada_w.shape[2]
    tn = 1536
    return pl.pallas_call(
        _ada_kernel,
        grid=(DEPTH, n6 // tn),
        in_specs=[pl.BlockSpec((MOD_ROWS, d), lambda l, n: (0, 0)),
                  pl.BlockSpec((1, d, tn), lambda l, n: (l, 0, n)),
                  pl.BlockSpec((1, 1, tn), lambda l, n: (l, 0, n))],
        out_specs=pl.BlockSpec((1, MOD_ROWS, tn), lambda l, n: (l, 0, n)),
        out_shape=jax.ShapeDtypeStruct((DEPTH, MOD_ROWS, n6), F32),
        compiler_params=_params("parallel", "parallel"),
        name="ada_mod",
    )(cond, ada_w, ada_b.reshape(DEPTH, 1, n6))


def _inproj_kernel(x_ref, sc_ref, sh_ref, w_ref, o_ref, *, tn):
    h = (x_ref[0] * (1.0 + sc_ref[0]) + sh_ref[0]).astype(BF16)
    for n in range(o_ref.shape[2] // tn):
        o_ref[0, :, n * tn:(n + 1) * tn] = jnp.dot(
            h, w_ref[:, n * tn:(n + 1) * tn], preferred_element_type=F32).astype(o_ref.dtype)


def _inproj(x, sc, sh, w, mod_row, bm):
    bn, t, d = x.shape
    nout = w.shape[1]
    mod_spec = pl.BlockSpec((1, 1, d), lambda b, i: (mod_row(b), 0, 0))
    return pl.pallas_call(
        functools.partial(_inproj_kernel, tn=512),
        grid=(bn, t // bm),
        in_specs=[pl.BlockSpec((1, bm, d), lambda b, i: (b, i, 0)), mod_spec, mod_spec,
                  pl.BlockSpec((d, nout), lambda b, i: (0, 0))],
        out_specs=pl.BlockSpec((1, bm, nout), lambda b, i: (b, i, 0)),
        out_shape=jax.ShapeDtypeStruct((bn, t, nout), BF16),
        compiler_params=_params("parallel", "parallel"),
        name="in_proj",
    )(x, sc, sh, w)


def _scan_kernel(*refs, reverse, finalize):
    if finalize:
        (q_ref, z_ref, v_ref, lb_ref, s0_ref, g_ref, op_ref, nw_ref, o_ref, sfin_ref, st_ref, oacc_ref) = refs
    else:
        (q_ref, z_ref, v_ref, lb_ref, s0_ref, o_ref, sfin_ref, st_ref, oacc_ref) = refs
    rows = q_ref.shape[1]
    heads = q_ref.shape[2] // A_DK
    nchunk = rows // SCAN_CHUNK

    @pl.when(pl.program_id(2) == 0)
    def _():
        st_ref[...] = s0_ref[0]

    q = _silu(q_ref[0].astype(F32))
    lb = lb_ref[...]
    f = lb + (1.0 - lb) * _sigmoid(z_ref[0].astype(F32))
    logf = jnp.log(f)
    k = 1.0 - f
    v = v_ref[0]

    ri = lax.broadcasted_iota(jnp.int32, (rows, rows), 0)
    ci = lax.broadcasted_iota(jnp.int32, (rows, rows), 1)
    same = (ri // SCAN_CHUNK) == (ci // SCAN_CHUNK)
    tri = jnp.where(same & ((ci >= ri) if reverse else (ci <= ri)), 1.0, 0.0).astype(BF16)
    hi = logf.astype(BF16)
    lo = (logf - hi.astype(F32)).astype(BF16)
    b = jnp.dot(tri, hi, preferred_element_type=F32) + jnp.dot(tri, lo, preferred_element_type=F32)

    q_dec = (q * jnp.exp(b)).astype(BF16)
    k_inv = (k * jnp.exp(-b)).astype(BF16)
    cr = lax.broadcasted_iota(jnp.int32, (SCAN_CHUNK, SCAN_CHUNK), 0)
    cc = lax.broadcasted_iota(jnp.int32, (SCAN_CHUNK, SCAN_CHUNK), 1)
    causal = (cc >= cr) if reverse else (cc <= cr)

    st = [st_ref[h] for h in range(heads)]
    order = range(nchunk - 1, -1, -1) if reverse else range(nchunk)
    for c in order:
        lo_r, hi_r = c * SCAN_CHUNK, (c + 1) * SCAN_CHUNK
        tot_r = lo_r if reverse else hi_r - 1
        b_tot = b[tot_r:tot_r + 1, :]
        ke = (k[lo_r:hi_r] * jnp.exp(b_tot - b[lo_r:hi_r])).astype(BF16)
        dec = jnp.exp(b_tot)
        for h in range(heads):
            hk = slice(h * A_DK, (h + 1) * A_DK)
            hv = slice(h * A_DV, (h + 1) * A_DV)
            qd = q_dec[lo_r:hi_r, hk]
            vc = v[lo_r:hi_r, hv]
            scores = lax.dot_general(qd, k_inv[lo_r:hi_r, hk], _NT, preferred_element_type=F32)
            scores = jnp.where(causal, scores, 0.0).astype(BF16)
            oacc_ref[lo_r:hi_r, hv] = (jnp.dot(scores, vc, preferred_element_type=F32)
                                       + lax.dot_general(qd, st[h].astype(BF16), _NT, preferred_element_type=F32))
            st[h] = st[h] * dec[:, hk] + lax.dot_general(vc, ke[:, hk], _TN, preferred_element_type=F32)
    for h in range(heads):
        st_ref[h] = st[h]
        sfin_ref[0, h] = st[h]

    o = oacc_ref[...]
    if finalize:
        o = o + op_ref[0]
        gate = nw_ref[...] * _silu(g_ref[0].astype(F32))
        for h in range(heads):
            hv = slice(h * A_DV, (h + 1) * A_DV)
            oh = o[:, hv]
            oh = oh * lax.rsqrt(jnp.mean(oh * oh, axis=-1, keepdims=True) + EPS)
            o_ref[0, :, hv] = (oh * gate[:, hv]).astype(o_ref.dtype)
    else:
        o_ref[0] = o.astype(o_ref.dtype)


def _scan(zin, lb, s0, *, reverse, o_prev=None, norm_w=None):
    bn, t, _ = zin.shape
    finalize = o_prev is not None
    rows = min(SCAN_ROWS, t)
    nb = t // rows
    pos = (lambda n: nb - 1 - n) if reverse else (lambda n: n)
    hp = SCAN_HEADS
    groups = A_HEADS // hp
    hw = hp * A_DK

    def col(section):
        return pl.BlockSpec((1, rows, hw), lambda b, h, n: (b, pos(n), section * groups + h))

    state_spec = pl.BlockSpec((1, hp, A_DV, A_DK), lambda b, h, n: (b, h, 0, 0))
    in_specs = [col(0), col(2 if reverse else 1), col(3),
                pl.BlockSpec((1, hw), lambda b, h, n: (0, h)), state_spec]
    args = [zin, zin, zin, lb.reshape(1, A_KW), s0]
    if finalize:
        in_specs += [col(4),
                     pl.BlockSpec((1, rows, hw), lambda b, h, n: (b, pos(n), h)),
                     pl.BlockSpec((1, hw), lambda b, h, n: (0, h))]
        args += [zin, o_prev, norm_w.reshape(1, A_WIDTH)]
    return pl.pallas_call(
        functools.partial(_scan_kernel, reverse=reverse, finalize=finalize),
        grid=(bn, groups, nb),
        in_specs=in_specs,
        out_specs=[pl.BlockSpec((1, rows, hw), lambda b, h, n: (b, pos(n), h)), state_spec],
        out_shape=[jax.ShapeDtypeStruct((bn, t, A_WIDTH), BF16 if finalize else F32),
                   jax.ShapeDtypeStruct((bn, A_HEADS, A_DV, A_DK), F32)],
        scratch_shapes=[pltpu.VMEM((hp, A_DV, A_DK), F32), pltpu.VMEM((rows, hw), F32)],
        compiler_params=_params("parallel", "parallel", "arbitrary"),
        name="hgrn2_scan_bwd" if reverse else "hgrn2_scan_fwd",
    )(*args)


def _sub_tiles(bm):
    sub = min(bm, SUB_TILE)
    return [slice(s * sub, (s + 1) * sub) for s in range(bm // sub)]


def _post_mix(acc, rs, x_ref, g1_ref, lnw_ref, lnb_ref, sc2_ref, sh2_ref, rw_ref, rb_ref,
              x1_ref, h2_ref, ti_ref, tg_ref):
    r = DN_ALPHA * x_ref[0, rs, :] + g1_ref[0] * acc
    mu = jnp.mean(r, axis=-1, keepdims=True)
    rc = r - mu
    var = jnp.mean(rc * rc, axis=-1, keepdims=True)
    x1 = rc * lax.rsqrt(var + EPS) * lnw_ref[...] + lnb_ref[...]
    x1_ref[0, rs, :] = x1
    h2 = x1 * (1.0 + sc2_ref[0]) + sh2_ref[0]
    h2_ref[rs, :] = h2.astype(h2_ref.dtype)
    logits = lax.dot_general(rw_ref[...], h2.astype(BF16), _NT, preferred_element_type=F32) + rb_ref[...]
    iota = lax.broadcasted_iota(jnp.int32, logits.shape, 0)
    vals, idxs = [], []
    cur = logits
    for _ in range(TOP_K):
        m = jnp.max(cur, axis=0, keepdims=True)
        ik = jnp.min(jnp.where(cur == m, iota, N_EXPERTS), axis=0, keepdims=True)
        vals.append(m)
        idxs.append(ik)
        cur = jnp.where(iota == ik, -jnp.inf, cur)
    tv = jnp.concatenate(vals, axis=0)
    e = jnp.exp(tv - tv[0:1])
    tg_ref[:, rs] = e / jnp.sum(e, axis=0, keepdims=True)
    ti_ref[:, rs] = jnp.concatenate(idxs, axis=0)


def _outproj_even_kernel(a_ref, u_ref, gb_ref, gc_ref, up_ref, gcp_ref, un_ref, gcn_ref, cw_ref, wo_ref,
                         x_ref, g1_ref, lnw_ref, lnb_ref, sc2_ref, sh2_ref, rw_ref, rb_ref, alias_ref,
                         x1_ref, h2_ref, ti_ref, tg_ref):
    del alias_ref
    i = pl.program_id(1)
    bm = u_ref.shape[1]
    z = gc_ref[0].astype(F32) * u_ref[0].astype(F32)
    z_before = gcp_ref[0, HALO - 1:HALO, :].astype(F32) * up_ref[0, HALO - 1:HALO, :].astype(F32)
    z_after = gcn_ref[0, 0:1, :].astype(F32) * un_ref[0, 0:1, :].astype(F32)
    z_before = jnp.where(i == 0, 0.0, z_before)
    z_after = jnp.where(i == pl.num_programs(1) - 1, 0.0, z_after)
    row = lax.broadcasted_iota(jnp.int32, z.shape, 0)
    z_prev = jnp.where(row == 0, z_before, pltpu.roll(z, 1, axis=0))
    z_next = jnp.where(row == bm - 1, z_after, pltpu.roll(z, bm - 1, axis=0))
    zc = cw_ref[0:1, :] * z_prev + cw_ref[1:2, :] * z + cw_ref[2:3, :] * z_next
    b_mix = (gb_ref[0].astype(F32) * zc).astype(BF16)
    for rs in _sub_tiles(bm):
        acc = (jnp.dot(a_ref[0, rs, :], wo_ref[0:A_WIDTH, :], preferred_element_type=F32)
               + jnp.dot(b_mix[rs], wo_ref[A_WIDTH:, :], preferred_element_type=F32))
        _post_mix(acc, rs, x_ref, g1_ref, lnw_ref, lnb_ref, sc2_ref, sh2_ref, rw_ref, rb_ref,
                  x1_ref, h2_ref, ti_ref, tg_ref)


def _outproj_odd_kernel(c_ref, u_ref, v_ref, dw_ref, db_ref, spw_ref, spb_ref, wo_ref,
                        x_ref, g1_ref, lnw_ref, lnb_ref, sc2_ref, sh2_ref, rw_ref, rb_ref, alias_ref,
                        x1_ref, h2_ref, ti_ref, tg_ref):
    del alias_ref
    bm = u_ref.shape[1]
    u = _gelu(u_ref[0].astype(F32))
    v = _gelu(v_ref[0].astype(F32))
    cols = []
    for g in range(D_GROUPS):
        lo_c, hi_c = g * D_GDIM, (g + 1) * D_GDIM
        vg = v[:, lo_c:hi_c]
        mu = jnp.mean(vg, axis=-1, keepdims=True)
        vc = vg - mu
        var = jnp.mean(vc * vc, axis=-1, keepdims=True)
        vn = (vc * lax.rsqrt(var + EPS) * dw_ref[:, lo_c:hi_c] + db_ref[:, lo_c:hi_c]).astype(BF16)
        parts = []
        for ch in range(bm // D_CHUNK):
            sv = jnp.dot(spw_ref[g], vn[ch * D_CHUNK:(ch + 1) * D_CHUNK], preferred_element_type=F32)
            parts.append(sv + spb_ref[:, g:g + 1])
        cols.append(u[:, lo_c:hi_c] * jnp.concatenate(parts, axis=0))
    d_mix = jnp.concatenate(cols, axis=1).astype(BF16)
    for rs in _sub_tiles(bm):
        acc = (jnp.dot(c_ref[0, rs, :].astype(BF16), wo_ref[0:C_WIDTH, :], preferred_element_type=F32)
               + jnp.dot(d_mix[rs], wo_ref[C_WIDTH:, :], preferred_element_type=F32))
        _post_mix(acc, rs, x_ref, g1_ref, lnw_ref, lnb_ref, sc2_ref, sh2_ref, rw_ref, rb_ref,
                  x1_ref, h2_ref, ti_ref, tg_ref)


def _outproj(kind, mixer_args, mixer_specs, wo, x, mods, mod_row, lnw, lnb, rw_t, rb, tok_state, tok_off, bm):
    bn, t, d = x.shape
    nt = t // bm
    g1, sc2, sh2 = mods
    h2_all, ti_all, tg_all = tok_state
    ntok = h2_all.shape[0]
    off = tok_off // bm
    mod_spec = pl.BlockSpec((1, 1, d), lambda b, i: (mod_row(b), 0, 0))
    vec_spec = pl.BlockSpec((1, d), lambda b, i: (0, 0))
    in_specs = list(mixer_specs) + [
        pl.BlockSpec((d, d), lambda b, i: (0, 0)),
        pl.BlockSpec((1, bm, d), lambda b, i: (b, i, 0)),
        mod_spec, vec_spec, vec_spec, mod_spec, mod_spec,
        pl.BlockSpec((N_EXPERTS, d), lambda b, i: (0, 0)),
        pl.BlockSpec((N_EXPERTS, 1), lambda b, i: (0, 0)),
        pl.BlockSpec(memory_space=pl.ANY), pl.BlockSpec(memory_space=pl.ANY), pl.BlockSpec(memory_space=pl.ANY)]
    n_in = len(in_specs)
    kernel = _outproj_even_kernel if kind == "even" else _outproj_odd_kernel

    def body(*refs):
        ins, outs = refs[:n_in], refs[n_in:]
        kernel(*ins[:n_in - 3], ins[n_in - 3:], *outs)

    return pl.pallas_call(
        body,
        grid=(bn, nt),
        in_specs=in_specs,
        out_specs=[pl.BlockSpec((1, bm, d), lambda b, i: (b, i, 0)),
                   pl.BlockSpec((bm, d), lambda b, i: (off + b * nt + i, 0)),
                   pl.BlockSpec((TOP_K, bm), lambda b, i: (0, off + b * nt + i)),
                   pl.BlockSpec((TOP_K, bm), lambda b, i: (0, off + b * nt + i))],
        out_shape=[jax.ShapeDtypeStruct((bn, t, d), F32),
                   jax.ShapeDtypeStruct((ntok, d), BF16),
                   jax.ShapeDtypeStruct((TOP_K, ntok), jnp.int32),
                   jax.ShapeDtypeStruct((TOP_K, ntok), F32)],
        input_output_aliases={n_in - 3: 1, n_in - 2: 2, n_in - 1: 3},
        compiler_params=_params("parallel", "parallel"),
        name="out_proj_" + kind,
    )(*mixer_args, wo, x, g1, lnw.reshape(1, d), lnb.reshape(1, d), sc2, sh2, rw_t, rb.reshape(N_EXPERTS, 1),
      h2_all, ti_all, tg_all)


def _outproj_even(a_mix, zin, conv_w, **kw):
    bm = kw["bm"]
    t = zin.shape[1]
    hb = bm // HALO
    last = t // HALO - 1
    cur = lambda sec: pl.BlockSpec((1, bm, B_WIDTH), lambda b, i: (b, i, sec))
    before = lambda sec: pl.BlockSpec((1, HALO, B_WIDTH), lambda b, i: (b, jnp.maximum(i * hb - 1, 0), sec))
    after = lambda sec: pl.BlockSpec((1, HALO, B_WIDTH), lambda b, i: (b, jnp.minimum((i + 1) * hb, last), sec))
    specs = [pl.BlockSpec((1, bm, A_WIDTH), lambda b, i: (b, i, 0)), cur(5), cur(6), cur(7),
             before(5), before(7), after(5), after(7),
             pl.BlockSpec((CONV_W, B_WIDTH), lambda b, i: (0, 0))]
    return _outproj("even", [a_mix, zin, zin, zin, zin, zin, zin, zin, conv_w], specs, **kw)


def _outproj_odd(c_mix, zin, d_norm_w, d_norm_b, sp_w, sp_b, **kw):
    bm = kw["bm"]
    cur = lambda sec: pl.BlockSpec((1, bm, D_WIDTH), lambda b, i: (b, i, sec))
    specs = [pl.BlockSpec((1, bm, C_WIDTH), lambda b, i: (b, i, 0)), cur(1), cur(2),
             pl.BlockSpec((1, D_WIDTH), lambda b, i: (0, 0)), pl.BlockSpec((1, D_WIDTH), lambda b, i: (0, 0)),
             pl.BlockSpec((D_GROUPS, D_CHUNK, D_CHUNK), lambda b, i: (0, 0, 0)),
             pl.BlockSpec((D_CHUNK, D_GROUPS), lambda b, i: (0, 0))]
    return _outproj("odd", [c_mix, zin, zin, d_norm_w.reshape(1, D_WIDTH), d_norm_b.reshape(1, D_WIDTH),
                            sp_w.astype(BF16), sp_b.T], specs, **kw)


def _dft_mats(n):
    ang = 2.0 * np.pi * np.outer(np.arange(n), np.arange(n)) / n
    return np.cos(ang), np.sin(ang)


def _fourier_cw_kernel(z_ref, nw_ref, cs_ref, kc_ref, ks_ref, p_ref, q_ref, ps_ref, qs_ref):
    tile = z_ref.shape[1]
    half = 256
    grows = tile // GRID_W
    for hs in range(tile // half):
        z = z_ref[0, hs * half:(hs + 1) * half, :].astype(F32)
        a_parts, b_parts = [], []
        for g in range(C_GROUPS):
            zg = z[:, g * C_GROUP:(g + 1) * C_GROUP]
            zg = zg * lax.rsqrt(jnp.mean(zg * zg, axis=-1, keepdims=True) + EPS) * nw_ref[:, g * C_GROUP:(g + 1) * C_GROUP]
            ab = jnp.dot(zg.astype(BF16), cs_ref[...], preferred_element_type=F32)
            a_parts.append(ab[:, :C_GROUP])
            b_parts.append(ab[:, C_GROUP:])
        ab = jnp.concatenate(a_parts + b_parts, axis=1).astype(BF16)
        m1 = jnp.dot(kc_ref[...], ab, preferred_element_type=F32)
        m2 = jnp.dot(ks_ref[...], ab, preferred_element_type=F32)
        pv = m1[:, :C_WIDTH] - m2[:, C_WIDTH:]
        qv = m2[:, :C_WIDTH] + m1[:, C_WIDTH:]
        for g in range(C_GROUPS):
            ps_ref[g, hs * half:(hs + 1) * half, :] = pv[:, g * C_GROUP:(g + 1) * C_GROUP]
            qs_ref[g, hs * half:(hs + 1) * half, :] = qv[:, g * C_GROUP:(g + 1) * C_GROUP]
    for k2 in range(GRID_W):
        for g in range(C_GROUPS):
            p_ref[0, k2, :, g * C_GROUP:(g + 1) * C_GROUP] = ps_ref[g, pl.ds(k2, grows, stride=GRID_W), :]
            q_ref[0, k2, :, g * C_GROUP:(g + 1) * C_GROUP] = qs_ref[g, pl.ds(k2, grows, stride=GRID_W), :]


def _fourier_r_kernel(p_ref, q_ref, c_ref, s_ref, o_ref, ys_ref):
    nj = p_ref.shape[1]
    rows = p_ref.shape[2]
    for j in range(nj):
        y = (jnp.dot(c_ref[...], p_ref[0, j].astype(BF16), preferred_element_type=F32)
             + jnp.dot(s_ref[...], q_ref[0, j].astype(BF16), preferred_element_type=F32))
        for g in range(C_GROUPS):
            ys_ref[g, j * rows:(j + 1) * rows, :] = y[:, g * C_GROUP:(g + 1) * C_GROUP]

    def put(k1, carry):
        for g in range(C_GROUPS):
            o_ref[0, k1, :, g * C_GROUP:(g + 1) * C_GROUP] = ys_ref[g, pl.ds(k1, nj, stride=rows), :]
        return carry

    lax.fori_loop(0, rows, put, 0)


def _fourier_mix(zin, c_norm_w):
    bn, t, _ = zin.shape
    rows = t // GRID_W
    tile = 512
    grows = tile // GRID_W
    c3, s3 = _dft_mats(C_GROUP)
    c2, s2 = _dft_mats(GRID_W)
    c1, s1 = _dft_mats(rows)
    scale = 1.0 / math.sqrt(rows * GRID_W * C_GROUP)
    cs3 = jnp.asarray(np.concatenate([c3, s3], axis=1), BF16)
    eye = np.eye(256 // GRID_W)
    kc2 = jnp.asarray(np.kron(eye, c2), BF16)
    ks2 = jnp.asarray(np.kron(eye, s2), BF16)
    c1s = jnp.asarray(c1 * scale, BF16)
    s1s = jnp.asarray(-s1 * scale, BF16)
    pq_shape = jax.ShapeDtypeStruct((bn, GRID_W, rows, C_WIDTH), F32)
    pq_spec = pl.BlockSpec((1, GRID_W, grows, C_WIDTH), lambda b, i: (b, 0, i, 0))
    p, q = pl.pallas_call(
        _fourier_cw_kernel,
        grid=(bn, t // tile),
        in_specs=[pl.BlockSpec((1, tile, C_WIDTH), lambda b, i: (b, i, 0)),
                  pl.BlockSpec((1, C_WIDTH), lambda b, i: (0, 0)),
                  pl.BlockSpec((C_GROUP, 2 * C_GROUP), lambda b, i: (0, 0)),
                  pl.BlockSpec((256, 256), lambda b, i: (0, 0)),
                  pl.BlockSpec((256, 256), lambda b, i: (0, 0))],
        out_specs=[pq_spec, pq_spec],
        out_shape=[pq_shape, pq_shape],
        scratch_shapes=[pltpu.VMEM((C_GROUPS, tile, C_GROUP), F32), pltpu.VMEM((C_GROUPS, tile, C_GROUP), F32)],
        compiler_params=_params("parallel", "parallel"),
        name="fourier_chan_col",
    )(zin, c_norm_w.reshape(1, C_WIDTH), cs3, kc2, ks2)
    nj = 8
    in_spec = pl.BlockSpec((1, nj, rows, C_WIDTH), lambda b, j: (b, j, 0, 0))
    mat_spec = pl.BlockSpec((rows, rows), lambda b, j: (0, 0))
    y = pl.pallas_call(
        _fourier_r_kernel,
        grid=(bn, GRID_W // nj),
        in_specs=[in_spec, in_spec, mat_spec, mat_spec],
        out_specs=pl.BlockSpec((1, rows, nj, C_WIDTH), lambda b, j: (b, 0, j, 0)),
        out_shape=jax.ShapeDtypeStruct((bn, rows, GRID_W, C_WIDTH), F32),
        scratch_shapes=[pltpu.VMEM((C_GROUPS, nj * rows, C_GROUP), F32)],
        compiler_params=_params("parallel", "parallel"),
        name="fourier_rows",
    )(p, q, c1s, s1s)
    return y.reshape(bn, t, C_WIDTH)


def _rank_kernel(ti_ref, rank_ref, cnt_ref, run_ref):
    @pl.when(pl.program_id(0) == 0)
    def _():
        run_ref[...] = jnp.zeros_like(run_ref)

    ti = ti_ref[...]
    tb = ti.shape[1]
    eio = lax.broadcasted_iota(jnp.int32, (N_EXPERTS, tb), 0)
    hot = [eio == ti[k:k + 1, :] for k in range(TOP_K)]
    occ = sum(h.astype(F32) for h in hot)
    si = lax.broadcasted_iota(jnp.int32, (tb, tb), 0)
    ti_ = lax.broadcasted_iota(jnp.int32, (tb, tb), 1)
    before = jnp.where(si < ti_, 1.0, 0.0).astype(BF16)
    seen = jnp.dot(occ.astype(BF16), before, preferred_element_type=F32) + run_ref[:, 0:1]
    rank_ref[...] = jnp.concatenate(
        [jnp.sum(jnp.where(h, seen, 0.0), axis=0, keepdims=True) for h in hot], axis=0).astype(jnp.int32)
    run_ref[...] = run_ref[...] + jnp.sum(occ, axis=1, keepdims=True)
    cnt_ref[...] = run_ref[...].astype(jnp.int32)


def _ranks(top_i):
    ntok = top_i.shape[1]
    tb = ROUTE_TILE if ntok % ROUTE_TILE == 0 else 256
    rank, cnt = pl.pallas_call(
        _rank_kernel,
        grid=(ntok // tb,),
        in_specs=[pl.BlockSpec((TOP_K, tb), lambda i: (0, i))],
        out_specs=[pl.BlockSpec((TOP_K, tb), lambda i: (0, i)),
                   pl.BlockSpec((N_EXPERTS, 128), lambda i: (0, 0))],
        out_shape=[jax.ShapeDtypeStruct((TOP_K, ntok), jnp.int32),
                   jax.ShapeDtypeStruct((N_EXPERTS, 128), jnp.int32)],
        scratch_shapes=[pltpu.VMEM((N_EXPERTS, 128), F32)],
        compiler_params=_params("arbitrary"),
        name="route_rank",
    )(top_i)
    return rank, cnt[:, 0]


def _w1_split_kernel(w_ref, g_ref, l_ref, s_ref):
    kdim, cols = w_ref.shape[2], w_ref.shape[3]
    half = cols // 2
    nk = kdim // LANES
    t = w_ref[0, 0].T
    for kc in range(nk):
        s_ref[kc] = t[:, kc * LANES:(kc + 1) * LANES]
    even = jnp.concatenate([s_ref[kc, pl.ds(0, half, stride=2), :] for kc in range(nk)], axis=1)
    odd = jnp.concatenate([s_ref[kc, pl.ds(1, half, stride=2), :] for kc in range(nk)], axis=1)
    g_ref[0] = even.T.astype(BF16)
    l_ref[0] = odd.T.astype(BF16)


def _w1_split(w1_all, layer):
    _, e, d, f2 = w1_all.shape
    cols = 1024
    out = jax.ShapeDtypeStruct((e, d, f2 // 2), BF16)
    out_spec = pl.BlockSpec((1, d, cols // 2), lambda i, c: (i, 0, c))
    return pl.pallas_call(
        _w1_split_kernel,
        grid=(e, f2 // cols),
        in_specs=[pl.BlockSpec((1, 1, d, cols), lambda i, c: (layer, i, 0, c))],
        out_specs=[out_spec, out_spec],
        out_shape=[out, out],
        scratch_shapes=[pltpu.VMEM((d // LANES, cols, LANES), F32)],
        compiler_params=_params("parallel", "parallel"),
        name="moe_w1_split",
    )(w1_all)


def _cast_kernel(w_ref, o_ref):
    o_ref[...] = w_ref[0].astype(o_ref.dtype)


def _w2_cast(w2_all, layer):
    _, e, f, d = w2_all.shape
    return pl.pallas_call(
        _cast_kernel,
        grid=(e,),
        in_specs=[pl.BlockSpec((1, 1, f, d), lambda i: (layer, i, 0, 0))],
        out_specs=pl.BlockSpec((1, f, d), lambda i: (i, 0, 0)),
        out_shape=jax.ShapeDtypeStruct((e, f, d), BF16),
        compiler_params=_params("parallel"),
        name="moe_w2_cast",
    )(w2_all)


def _expert_kernel(be_ref, used_ref, x_ref, w1g_ref, w1l_ref, w2_ref, b1g_ref, b1l_ref, b2_ref, y_ref):
    del be_ref
    rows = x_ref.shape[0] // ROW_TILES
    in_use = pl.program_id(0) < used_ref[0]

    @pl.when(in_use)
    def _():
        x = _load_row_tiles(x_ref, rows).astype(BF16)
        glu = jnp.minimum(jnp.dot(x, w1g_ref[0], preferred_element_type=F32) + b1g_ref[0], SWIGLU_LIMIT)
        lin = jnp.clip(jnp.dot(x, w1l_ref[0], preferred_element_type=F32) + b1l_ref[0],
                       -SWIGLU_LIMIT, SWIGLU_LIMIT)
        act = glu * _sigmoid(SWIGLU_ALPHA * glu) * (lin + 1.0)
        _store_row_tiles(y_ref, jnp.dot(act.astype(BF16), w2_ref[0], preferred_element_type=F32) + b2_ref[0])

    @pl.when(jnp.logical_not(in_use))
    def _():
        y_ref[...] = jnp.zeros_like(y_ref)


def _experts(x_sorted, block_expert, blocks_used, w1g, w1l, w2, b1g, b1l, b2):
    d, f = w1g.shape[1], w1g.shape[2]
    nb = block_expert.shape[0]
    buf_rows = MOE_ROWS * ROW_TILES
    wspec = lambda k, n: pl.BlockSpec((1, k, n), lambda j, be, nu: (be[j], 0, 0))
    row_spec = pl.BlockSpec((buf_rows, LANES), lambda j, be, nu: (j, 0))
    grid_spec = pltpu.PrefetchScalarGridSpec(
        num_scalar_prefetch=2,
        grid=(nb,),
        in_specs=[row_spec, wspec(d, f), wspec(d, f), wspec(f, d), wspec(1, f), wspec(1, f), wspec(1, d)],
        out_specs=row_spec)
    return pl.pallas_call(
        _expert_kernel,
        grid_spec=grid_spec,
        out_shape=jax.ShapeDtypeStruct(x_sorted.shape, F32),
        compiler_params=_params("parallel"),
        name="moe_experts",
    )(block_expert, blocks_used, x_sorted, w1g, w1l, w2, b1g, b1l, b2)


def _dispatch_kernel(cnt_ref, start_ref, dst_ref, h_ref, xs_hbm, rowbuf, zero_ref, sem, zsem, *, nt, nslots):
    i = pl.program_id(0)
    tb = h_ref.shape[0]
    slot = i % 2

    def landed(buf):
        for _ in range(TOP_K):
            pltpu.make_async_copy(rowbuf.at[buf], xs_hbm.at[pl.ds(0, tb * ROW_TILES), :], sem.at[buf]).wait()

    @pl.when(i >= 2)
    def _():
        landed(slot)

    _store_row_tiles(rowbuf.at[slot], h_ref[...].astype(F32))
    for r in range(tb):
        for k in range(TOP_K):
            dst = pl.multiple_of(dst_ref[0, k, r] * ROW_TILES, ROW_TILES)
            pltpu.make_async_copy(rowbuf.at[slot, pl.ds(r * ROW_TILES, ROW_TILES), :],
                                  xs_hbm.at[pl.ds(dst, ROW_TILES), :], sem.at[slot]).start(priority=k % 2)

    @pl.when(i == nt - 1)
    def _():
        landed(slot)
        if nt > 1:
            landed(1 - slot)
        zero_ref[...] = jnp.zeros_like(zero_ref)

        def fill(lo, hi):
            def body(row, carry):
                pltpu.make_async_copy(
                    zero_ref, xs_hbm.at[pl.ds(pl.multiple_of(row * ROW_TILES, ROW_TILES), ROW_TILES), :], zsem).start()
                return carry
            lax.fori_loop(lo, hi, body, 0)

            def drain(row, carry):
                pltpu.make_async_copy(zero_ref, xs_hbm.at[pl.ds(0, ROW_TILES), :], zsem).wait()
                return carry
            lax.fori_loop(lo, hi, drain, 0)

        end = 0
        for e in range(N_EXPERTS):
            used = start_ref[e] + cnt_ref[e]
            end = start_ref[e] + (cnt_ref[e] + MOE_ROWS - 1) // MOE_ROWS * MOE_ROWS
            fill(used, end)
        fill(end, nslots)


def _dispatch(h2, dest3, counts, pad_starts, nslots):
    ntok, d = h2.shape
    tb = COMBINE_TILE
    nt = ntok // tb
    grid_spec = pltpu.PrefetchScalarGridSpec(
        num_scalar_prefetch=2,
        grid=(nt,),
        in_specs=[pl.BlockSpec((1, TOP_K, tb), lambda i, c, s: (i, 0, 0), memory_space=pltpu.SMEM),
                  pl.BlockSpec((tb, d), lambda i, c, s: (i, 0))],
        out_specs=pl.BlockSpec(memory_space=pl.ANY),
        scratch_shapes=[pltpu.VMEM((2, tb * ROW_TILES, LANES), F32), pltpu.VMEM((ROW_TILES, LANES), F32),
                        pltpu.SemaphoreType.DMA((2,)), pltpu.SemaphoreType.DMA(())])
    return pl.pallas_call(
        functools.partial(_dispatch_kernel, nt=nt, nslots=nslots),
        grid_spec=grid_spec,
        out_shape=jax.ShapeDtypeStruct((nslots * ROW_TILES, LANES), F32),
        compiler_params=_params("arbitrary"),
        name="moe_dispatch",
    )(counts, pad_starts, dest3, h2)


def _combine_kernel(dst_ref, dstn_ref, y_hbm, x1_ref, gate_ref, g2_ref, lnw_ref, lnb_ref, o_ref, ybuf, sem, *, nt):
    i = pl.program_id(0)
    tb = x1_ref.shape[0]
    slot = i % 2

    def row_copy(src, buf, k, r):
        rows = y_hbm.at[pl.ds(pl.multiple_of(src * ROW_TILES, ROW_TILES), ROW_TILES), :]
        return pltpu.make_async_copy(rows, ybuf.at[buf, k, pl.ds(r * ROW_TILES, ROW_TILES), :], sem.at[buf])

    def gather(idx_ref, buf):
        for r in range(tb):
            for k in range(TOP_K):
                row_copy(idx_ref[0, k, r], buf, k, r).start(priority=k % 2)

    @pl.when(i == 0)
    def _():
        gather(dst_ref, 0)

    @pl.when(i + 1 < nt)
    def _():
        gather(dstn_ref, 1 - slot)

    for k in range(TOP_K):
        pltpu.make_async_copy(y_hbm.at[pl.ds(0, tb * ROW_TILES), :], ybuf.at[slot, k], sem.at[slot]).wait()

    y = sum(gate_ref[:, k:k + 1] * _load_row_tiles(ybuf.at[slot, k], tb) for k in range(TOP_K))
    r = DN_ALPHA * x1_ref[...] + g2_ref[0] * y
    mu = jnp.mean(r, axis=-1, keepdims=True)
    rc = r - mu
    var = jnp.mean(rc * rc, axis=-1, keepdims=True)
    o_ref[...] = rc * lax.rsqrt(var + EPS) * lnw_ref[...] + lnb_ref[...]


def _combine(x1, y_sorted, dest3, gates_t, tok_off, g2, mod_row, lnw, lnb):
    bn, t, d = x1.shape
    n = bn * t
    tb = COMBINE_TILE
    nt = n // tb
    off = tok_off // tb
    per_b = t // tb
    out = pl.pallas_call(
        functools.partial(_combine_kernel, nt=nt),
        grid=(nt,),
        in_specs=[pl.BlockSpec((1, TOP_K, tb), lambda i: (off + i, 0, 0), memory_space=pltpu.SMEM),
                  pl.BlockSpec((1, TOP_K, tb), lambda i: (off + jnp.minimum(i + 1, nt - 1), 0, 0),
                               memory_space=pltpu.SMEM),
                  pl.BlockSpec(memory_space=pl.ANY),
                  pl.BlockSpec((tb, d), lambda i: (i, 0)),
                  pl.BlockSpec((tb, TOP_K), lambda i: (off + i, 0)),
                  pl.BlockSpec((1, 1, d), lambda i: (mod_row(i // per_b), 0, 0)),
                  pl.BlockSpec((1, d), lambda i: (0, 0)),
                  pl.BlockSpec((1, d), lambda i: (0, 0))],
        out_specs=pl.BlockSpec((tb, d), lambda i: (i, 0)),
        out_shape=jax.ShapeDtypeStruct((n, d), F32),
        scratch_shapes=[pltpu.VMEM((2, TOP_K, tb * ROW_TILES, LANES), F32), pltpu.SemaphoreType.DMA((2,))],
        compiler_params=_params("arbitrary"),
        name="moe_combine",
    )(dest3, dest3, y_sorted, x1.reshape(n, d), gates_t, g2, lnw.reshape(1, d), lnb.reshape(1, d))
    return out.reshape(bn, t, d)


def _dest_kernel(ti_ref, rank_ref, start_ref, dest_ref):
    ti = ti_ref[...]
    tb = ti.shape[1]
    eio = lax.broadcasted_iota(jnp.int32, (N_EXPERTS, tb), 0)
    start = start_ref[:, 0:1].astype(F32)
    base = jnp.concatenate(
        [jnp.sum(jnp.where(eio == ti[k:k + 1, :], start, 0.0), axis=0, keepdims=True) for k in range(TOP_K)], axis=0)
    dest = base.astype(jnp.int32) + rank_ref[...]
    for c in range(tb // COMBINE_TILE):
        dest_ref[c] = dest[:, c * COMBINE_TILE:(c + 1) * COMBINE_TILE]


def _moe_layout(top_i):
    ntok = top_i.shape[1]
    m = ntok * TOP_K
    rank, counts = _ranks(top_i)
    padded = (counts + MOE_ROWS - 1) // MOE_ROWS * MOE_ROWS
    pad_ends = jnp.cumsum(padded)
    pad_starts = pad_ends - padded
    tb = ROUTE_TILE if ntok % ROUTE_TILE == 0 else 256
    ntiles = ntok // COMBINE_TILE
    dest = pl.pallas_call(
        _dest_kernel,
        grid=(ntok // tb,),
        in_specs=[pl.BlockSpec((TOP_K, tb), lambda i: (0, i)),
                  pl.BlockSpec((TOP_K, tb), lambda i: (0, i)),
                  pl.BlockSpec((N_EXPERTS, LANES), lambda i: (0, 0))],
        out_specs=pl.BlockSpec((tb // COMBINE_TILE, TOP_K, COMBINE_TILE), lambda i: (i, 0, 0)),
        out_shape=jax.ShapeDtypeStruct((ntiles, TOP_K, COMBINE_TILE), jnp.int32),
        compiler_params=_params("parallel"),
        name="route_dest",
    )(top_i, rank, jnp.broadcast_to(pad_starts[:, None], (N_EXPERTS, LANES)))
    nb = (m + N_EXPERTS * (MOE_ROWS - 1) + MOE_ROWS - 1) // MOE_ROWS
    block_starts = jnp.arange(nb, dtype=jnp.int32) * MOE_ROWS
    block_expert = jnp.minimum(jnp.sum(block_starts[:, None] >= pad_ends[None, :], axis=1),
                               N_EXPERTS - 1).astype(jnp.int32)
    blocks_used = (pad_ends[-1:] // MOE_ROWS).astype(jnp.int32)
    return dest, counts, pad_starts.astype(jnp.int32), block_expert, blocks_used, nb * MOE_ROWS


def kernel(x, c, ctx, c_ctx, lower_bounds_fwd, lower_bounds_bwd, ada_w, ada_b, w_in_even, a_norm_w, conv_w,
           w_in_odd, c_norm_w, d_norm_w, d_norm_b, spatial_w, spatial_b, w_out, ln_mix_w, ln_mix_b,
           ln_ffn_w, ln_ffn_b, router_w, router_b, moe_w1, moe_b1, moe_w2, moe_b2):
    bn, t, d = x.shape
    tc = ctx.shape[1]
    n_lat = bn * t
    n_ctx = bn * tc
    ctx_row = bn
    lat_row = lambda b: b
    ctx_mod_row = lambda b: ctx_row

    lb_f_all = jnp.cumsum(jax.nn.softmax(lower_bounds_fwd.astype(F32), axis=0), axis=0)
    lb_b_all = jnp.cumsum(jax.nn.softmax(lower_bounds_bwd.astype(F32), axis=0), axis=0)

    cond = jnp.zeros((MOD_ROWS, d), F32).at[:bn].set(c).at[ctx_row].set(c_ctx)
    mods = _ada(cond, ada_w, ada_b).reshape(DEPTH, MOD_ROWS, 6, 1, d)

    h_ctx = ctx
    for l in range(DEPTH):
        last = l == DEPTH - 1
        sh1, sc1, g1, sh2, sc2, g2 = (mods[l, :, j] for j in range(6))
        wo = w_out[l].astype(BF16)
        rw_t = router_w[l].T.astype(BF16)
        ntok = n_lat if last else n_lat + n_ctx
        tok_state = (jnp.zeros((ntok, d), BF16), jnp.zeros((TOP_K, ntok), jnp.int32),
                     jnp.zeros((TOP_K, ntok), F32))
        post = dict(wo=wo, lnw=ln_mix_w[l], lnb=ln_mix_b[l], rw_t=rw_t, rb=router_b[l])
        bm_ctx = min(ROW_TILE, tc)
        if l % 2 == 0:
            e = l // 2
            w_in = w_in_even[e].astype(BF16)
            zin_c = _inproj(h_ctx, sc1, sh1, w_in, ctx_mod_row, bm=bm_ctx)
            zin = _inproj(x, sc1, sh1, w_in, lat_row, bm=ROW_TILE)
            zeros = jnp.zeros((bn, A_HEADS, A_DV, A_DK), F32)
            oc_f, sc_f = _scan(zin_c, lb_f_all[l], zeros, reverse=False)
            a_ctx, sc_b = _scan(zin_c, lb_b_all[l], zeros, reverse=True, o_prev=oc_f, norm_w=a_norm_w[e])
            o_f, _ = _scan(zin, lb_f_all[l], sc_f, reverse=False)
            a_lat, _ = _scan(zin, lb_b_all[l], sc_b, reverse=True, o_prev=o_f, norm_w=a_norm_w[e])
            x1, *tok_state = _outproj_even(a_lat, zin, conv_w[e], x=x, mods=(g1, sc2, sh2), mod_row=lat_row,
                                           tok_state=tok_state, tok_off=0, bm=ROW_TILE, **post)
            if not last:
                hc1, *tok_state = _outproj_even(a_ctx, zin_c, conv_w[e], x=h_ctx, mods=(g1, sc2, sh2),
                                                mod_row=ctx_mod_row, tok_state=tok_state, tok_off=n_lat,
                                                bm=bm_ctx, **post)
        else:
            o = l // 2
            w_in = w_in_odd[o].astype(BF16)
            odd = dict(d_norm_w=d_norm_w[o], d_norm_b=d_norm_b[o], sp_w=spatial_w[o], sp_b=spatial_b[o])
            zin = _inproj(x, sc1, sh1, w_in, lat_row, bm=ROW_TILE)
            c_mix = _fourier_mix(zin, c_norm_w[o])
            x1, *tok_state = _outproj_odd(c_mix, zin, x=x, mods=(g1, sc2, sh2), mod_row=lat_row,
                                          tok_state=tok_state, tok_off=0, bm=ROW_TILE, **odd, **post)
            if not last:
                raise NotImplementedError("an odd layer that is not the last needs the context Fourier mixer")
        h2_all, top_i, top_g = tok_state
        dest, counts, pad_starts, block_expert, blocks_used, nslots = _moe_layout(top_i)
        x_sorted = _dispatch(h2_all, dest, counts, pad_starts, nslots)
        w1g, w1l = _w1_split(moe_w1, l)
        y_sorted = _experts(
            x_sorted, block_expert, blocks_used, w1g, w1l, _w2_cast(moe_w2, l),
            moe_b1[l][:, None, 0::2], moe_b1[l][:, None, 1::2], moe_b2[l][:, None, :])
        gates_t = top_g.T
        x = _combine(x1, y_sorted, dest, gates_t, 0, g2, lat_row, ln_ffn_w[l], ln_ffn_b[l])
        if not last:
            h_ctx = _combine(hc1, y_sorted, dest, gates_t, n_lat, g2, ctx_mod_row, ln_ffn_w[l], ln_ffn_b[l])
    return x
```

```python
import functools
import math

import numpy as np
import jax
import jax.numpy as jnp
from jax import lax
from jax.experimental import pallas as pl
from jax.experimental.pallas import tpu as pltpu

F32 = jnp.float32
BF16 = jnp.bfloat16

D_MODEL = 1024
DEPTH = 2
GRID_W = 64
A_HEADS = 4
A_DK = 128
A_DV = 128
A_KW = A_HEADS * A_DK
A_WIDTH = A_HEADS * A_DV
B_WIDTH = D_MODEL - A_WIDTH
CONV_W = 3
C_GROUPS = 4
C_GROUP = 128
C_WIDTH = C_GROUPS * C_GROUP
D_WIDTH = D_MODEL - C_WIDTH
D_GROUPS = 4
D_GDIM = D_WIDTH // D_GROUPS
D_CHUNK = 128
SCAN_CHUNK = 32
EVEN_IN = 3 * A_KW + 2 * A_WIDTH + 3 * B_WIDTH
ODD_IN = C_WIDTH + 2 * D_WIDTH
N_EXPERTS = 32
TOP_K = 4
D_EXPERT = 1024
SWIGLU_ALPHA = 1.702
SWIGLU_LIMIT = 7.0
DN_ALPHA = (2 * DEPTH) ** 0.25
EPS = 1e-5

MOD_ROWS = 8
SCAN_ROWS = 256
SCAN_HEADS = 4
ROW_TILE = 512
SUB_TILE = 256
MOE_ROWS = 512
ROUTE_TILE = 512
COMBINE_TILE = 128
HALO = 16
LANES = 128
ROW_TILES = D_MODEL // LANES

_NT = (((1,), (1,)), ((), ()))
_TN = (((0,), (0,)), ((), ()))


def _sigmoid(a):
    return 1.0 / (1.0 + jnp.exp(-a))


def _silu(a):
    return a * _sigmoid(a)


def _gelu(a):
    return 0.5 * a * (1.0 + lax.erf(a * (1.0 / math.sqrt(2.0))))


def _store_row_tiles(ref, val):
    rows = val.shape[0]
    for c in range(ROW_TILES):
        ref[pl.ds(c, rows, stride=ROW_TILES), :] = val[:, c * LANES:(c + 1) * LANES]


def _load_row_tiles(ref, rows):
    return jnp.concatenate([ref[pl.ds(c, rows, stride=ROW_TILES), :] for c in range(ROW_TILES)], axis=1)


def _params(*sem):
    return pltpu.CompilerParams(dimension_semantics=sem, vmem_limit_bytes=56 * 1024 * 1024)


def _ada_kernel(c_ref, w_ref, b_ref, o_ref):
    s = _silu(c_ref[...])
    o_ref[0] = jnp.dot(s, w_ref[0], preferred_element_type=F32, precision=lax.Precision.HIGHEST) + b_ref[0]


def _ada(cond, ada_w, ada_b):
    d = cond.shape[1]
    n6 = ada_w.shape[2]
    tn = 1536
    return pl.pallas_call(
        _ada_kernel,
        grid=(DEPTH, n6 // tn),
        in_specs=[pl.BlockSpec((MOD_ROWS, d), lambda l, n: (0, 0)),
                  pl.BlockSpec((1, d, tn), lambda l, n: (l, 0, n)),
                  pl.BlockSpec((1, 1, tn), lambda l, n: (l, 0, n))],
        out_specs=pl.BlockSpec((1, MOD_ROWS, tn), lambda l, n: (l, 0, n)),
        out_shape=jax.ShapeDtypeStruct((DEPTH, MOD_ROWS, n6), F32),
        compiler_params=_params("parallel", "parallel"),
        name="ada_mod",
    )(cond, ada_w, ada_b.reshape(DEPTH, 1, n6))


def _inproj_kernel(x_ref, sc_ref, sh_ref, w_ref, o_ref, *, tn):
    h = (x_ref[0] * (1.0 + sc_ref[0]) + sh_ref[0]).astype(BF16)
    for n in range(o_ref.shape[2] // tn):
        o_ref[0, :, n * tn:(n + 1) * tn] = jnp.dot(
            h, w_ref[:, n * tn:(n + 1) * tn], preferred_element_type=F32).astype(o_ref.dtype)


def _inproj(x, sc, sh, w, mod_row, bm):
    bn, t, d = x.shape
    nout = w.shape[1]
    mod_spec = pl.BlockSpec((1, 1, d), lambda b, i: (mod_row(b), 0, 0))
    return pl.pallas_call(
        functools.partial(_inproj_kernel, tn=512),
        grid=(bn, t // bm),
        in_specs=[pl.BlockSpec((1, bm, d), lambda b, i: (b, i, 0)), mod_spec, mod_spec,
                  pl.BlockSpec((d, nout), lambda b, i: (0, 0))],
        out_specs=pl.BlockSpec((1, bm, nout), lambda b, i: (b, i, 0)),
        out_shape=jax.ShapeDtypeStruct((bn, t, nout), BF16),
        compiler_params=_params("parallel", "parallel"),
        name="in_proj",
    )(x, sc, sh, w)


def _scan_kernel(*refs, reverse, finalize):
    if finalize:
        (q_ref, z_ref, v_ref, lb_ref, s0_ref, g_ref, op_ref, nw_ref, o_ref, sfin_ref, st_ref, oacc_ref) = refs
    else:
        (q_ref, z_ref, v_ref, lb_ref, s0_ref, o_ref, sfin_ref, st_ref, oacc_ref) = refs
    rows = q_ref.shape[1]
    heads = q_ref.shape[2] // A_DK
    nchunk = rows // SCAN_CHUNK

    @pl.when(pl.program_id(2) == 0)
    def _():
        st_ref[...] = s0_ref[0]

    q = _silu(q_ref[0].astype(F32))
    lb = lb_ref[...]
    f = lb + (1.0 - lb) * _sigmoid(z_ref[0].astype(F32))
    logf = jnp.log(f)
    k = 1.0 - f
    v = v_ref[0]

    ri = lax.broadcasted_iota(jnp.int32, (rows, rows), 0)
    ci = lax.broadcasted_iota(jnp.int32, (rows, rows), 1)
    same = (ri // SCAN_CHUNK) == (ci // SCAN_CHUNK)
    causal = same & ((ci >= ri) if reverse else (ci <= ri))
    tri = jnp.where(causal, 1.0, 0.0).astype(BF16)
    blk = jnp.where(same, 1.0, 0.0).astype(BF16)
    hi = logf.astype(BF16)
    lo = (logf - hi.astype(F32)).astype(BF16)
    b = jnp.dot(tri, hi, preferred_element_type=F32) + jnp.dot(tri, lo, preferred_element_type=F32)
    b_tot = jnp.dot(blk, hi, preferred_element_type=F32) + jnp.dot(blk, lo, preferred_element_type=F32)

    q_dec = (q * jnp.exp(b)).astype(BF16)
    k_inv = (k * jnp.exp(-b)).astype(BF16)
    k_end = (k * jnp.exp(b_tot - b)).astype(BF16)
    dec = jnp.exp(b_tot)

    wide = nchunk * A_DK
    spread = (lax.broadcasted_iota(jnp.int32, (rows, wide), 0) // SCAN_CHUNK
              == lax.broadcasted_iota(jnp.int32, (rows, wide), 1) // A_DK)
    order = range(nchunk - 1, -1, -1) if reverse else range(nchunk)
    for h in range(heads):
        hk = slice(h * A_DK, (h + 1) * A_DK)
        hv = slice(h * A_DV, (h + 1) * A_DV)
        qd, vh = q_dec[:, hk], v[:, hv]
        scores = lax.dot_general(qd, k_inv[:, hk], _NT, preferred_element_type=F32)
        scores = jnp.where(causal, scores, 0.0).astype(BF16)
        o_intra = jnp.dot(scores, vh, preferred_element_type=F32)
        ke_wide = jnp.where(spread, jnp.concatenate([k_end[:, hk]] * nchunk, axis=1), 0.0).astype(BF16)
        upd = lax.dot_general(vh, ke_wide, _TN, preferred_element_type=F32)
        st = st_ref[h]
        seen = [None] * nchunk
        for c in order:
            seen[c] = st.astype(BF16)
            r0 = c * SCAN_CHUNK
            st = st * dec[r0:r0 + 1, hk] + upd[:, c * A_DK:(c + 1) * A_DK]
        st_ref[h] = st
        sfin_ref[0, h] = st
        qd_wide = jnp.where(spread, jnp.concatenate([qd] * nchunk, axis=1), 0.0).astype(BF16)
        o_inter = lax.dot_general(qd_wide, jnp.concatenate(seen, axis=1), _NT, preferred_element_type=F32)
        oacc_ref[:, hv] = o_intra + o_inter

    o = oacc_ref[...]
    if finalize:
        o = o + op_ref[0]
        gate = nw_ref[...] * _silu(g_ref[0].astype(F32))
        for h in range(heads):
            hv = slice(h * A_DV, (h + 1) * A_DV)
            oh = o[:, hv]
            oh = oh * lax.rsqrt(jnp.mean(oh * oh, axis=-1, keepdims=True) + EPS)
            o_ref[0, :, hv] = (oh * gate[:, hv]).astype(o_ref.dtype)
    else:
        o_ref[0] = o.astype(o_ref.dtype)


def _scan(zin, lb, s0, *, reverse, o_prev=None, norm_w=None):
    bn, t, _ = zin.shape
    finalize = o_prev is not None
    rows = min(SCAN_ROWS, t)
    nb = t // rows
    pos = (lambda n: nb - 1 - n) if reverse else (lambda n: n)
    hp = SCAN_HEADS
    groups = A_HEADS // hp
    hw = hp * A_DK

    def col(section):
        return pl.BlockSpec((1, rows, hw), lambda b, h, n: (b, pos(n), section * groups + h))

    state_spec = pl.BlockSpec((1, hp, A_DV, A_DK), lambda b, h, n: (b, h, 0, 0))
    in_specs = [col(0), col(2 if reverse else 1), col(3),
                pl.BlockSpec((1, hw), lambda b, h, n: (0, h)), state_spec]
    args = [zin, zin, zin, lb.reshape(1, A_KW), s0]
    if finalize:
        in_specs += [col(4),
                     pl.BlockSpec((1, rows, hw), lambda b, h, n: (b, pos(n), h)),
                     pl.BlockSpec((1, hw), lambda b, h, n: (0, h))]
        args += [zin, o_prev, norm_w.reshape(1, A_WIDTH)]
    return pl.pallas_call(
        functools.partial(_scan_kernel, reverse=reverse, finalize=finalize),
        grid=(bn, groups, nb),
        in_specs=in_specs,
        out_specs=[pl.BlockSpec((1, rows, hw), lambda b, h, n: (b, pos(n), h)), state_spec],
        out_shape=[jax.ShapeDtypeStruct((bn, t, A_WIDTH), BF16 if finalize else F32),
                   jax.ShapeDtypeStruct((bn, A_HEADS, A_DV, A_DK), F32)],
        scratch_shapes=[pltpu.VMEM((hp, A_DV, A_DK), F32), pltpu.VMEM((rows, hw), F32)],
        compiler_params=_params("parallel", "parallel", "arbitrary"),
        name="hgrn2_scan_bwd" if reverse else "hgrn2_scan_fwd",
    )(*args)


def _sub_tiles(bm):
    sub = min(bm, SUB_TILE)
    return [slice(s * sub, (s + 1) * sub) for s in range(bm // sub)]


def _post_mix(acc, rs, x_ref, g1_ref, lnw_ref, lnb_ref, sc2_ref, sh2_ref, rw_ref, rb_ref,
              x1_ref, h2_ref, ti_ref, tg_ref):
    r = DN_ALPHA * x_ref[0, rs, :] + g1_ref[0] * acc
    mu = jnp.mean(r, axis=-1, keepdims=True)
    rc = r - mu
    var = jnp.mean(rc * rc, axis=-1, keepdims=True)
    x1 = rc * lax.rsqrt(var + EPS) * lnw_ref[...] + lnb_ref[...]
    x1_ref[0, rs, :] = x1
    h2 = x1 * (1.0 + sc2_ref[0]) + sh2_ref[0]
    h2_ref[rs, :] = h2.astype(h2_ref.dtype)
    logits = lax.dot_general(rw_ref[...], h2.astype(BF16), _NT, preferred_element_type=F32) + rb_ref[...]
    iota = lax.broadcasted_iota(jnp.int32, logits.shape, 0)
    vals, idxs = [], []
    cur = logits
    for _ in range(TOP_K):
        m = jnp.max(cur, axis=0, keepdims=True)
        ik = jnp.min(jnp.where(cur == m, iota, N_EXPERTS), axis=0, keepdims=True)
        vals.append(m)
        idxs.append(ik)
        cur = jnp.where(iota == ik, -jnp.inf, cur)
    tv = jnp.concatenate(vals, axis=0)
    e = jnp.exp(tv - tv[0:1])
    tg_ref[:, rs] = e / jnp.sum(e, axis=0, keepdims=True)
    ti_ref[:, rs] = jnp.concatenate(idxs, axis=0)


def _outproj_even_kernel(a_ref, u_ref, gb_ref, gc_ref, up_ref, gcp_ref, un_ref, gcn_ref, cw_ref, wo_ref,
                         x_ref, g1_ref, lnw_ref, lnb_ref, sc2_ref, sh2_ref, rw_ref, rb_ref, alias_ref,
                         x1_ref, h2_ref, ti_ref, tg_ref):
    del alias_ref
    i = pl.program_id(1)
    bm = u_ref.shape[1]
    z = gc_ref[0].astype(F32) * u_ref[0].astype(F32)
    z_before = gcp_ref[0, HALO - 1:HALO, :].astype(F32) * up_ref[0, HALO - 1:HALO, :].astype(F32)
    z_after = gcn_ref[0, 0:1, :].astype(F32) * un_ref[0, 0:1, :].astype(F32)
    z_before = jnp.where(i == 0, 0.0, z_before)
    z_after = jnp.where(i == pl.num_programs(1) - 1, 0.0, z_after)
    row = lax.broadcasted_iota(jnp.int32, z.shape, 0)
    z_prev = jnp.where(row == 0, z_before, pltpu.roll(z, 1, axis=0))
    z_next = jnp.where(row == bm - 1, z_after, pltpu.roll(z, bm - 1, axis=0))
    zc = cw_ref[0:1, :] * z_prev + cw_ref[1:2, :] * z + cw_ref[2:3, :] * z_next
    b_mix = (gb_ref[0].astype(F32) * zc).astype(BF16)
    for rs in _sub_tiles(bm):
        acc = (jnp.dot(a_ref[0, rs, :], wo_ref[0:A_WIDTH, :], preferred_element_type=F32)
               + jnp.dot(b_mix[rs], wo_ref[A_WIDTH:, :], preferred_element_type=F32))
        _post_mix(acc, rs, x_ref, g1_ref, lnw_ref, lnb_ref, sc2_ref, sh2_ref, rw_ref, rb_ref,
                  x1_ref, h2_ref, ti_ref, tg_ref)


def _outproj_odd_kernel(c_ref, u_ref, v_ref, dw_ref, db_ref, spw_ref, spb_ref, wo_ref,
                        x_ref, g1_ref, lnw_ref, lnb_ref, sc2_ref, sh2_ref, rw_ref, rb_ref, alias_ref,
                        x1_ref, h2_ref, ti_ref, tg_ref):
    del alias_ref
    bm = u_ref.shape[1]
    u = _gelu(u_ref[0].astype(F32))
    v = _gelu(v_ref[0].astype(F32))
    cols = []
    for g in range(D_GROUPS):
        lo_c, hi_c = g * D_GDIM, (g + 1) * D_GDIM
        vg = v[:, lo_c:hi_c]
        mu = jnp.mean(vg, axis=-1, keepdims=True)
        vc = vg - mu
        var = jnp.mean(vc * vc, axis=-1, keepdims=True)
        vn = (vc * lax.rsqrt(var + EPS) * dw_ref[:, lo_c:hi_c] + db_ref[:, lo_c:hi_c]).astype(BF16)
        parts = []
        for ch in range(bm // D_CHUNK):
            sv = jnp.dot(spw_ref[g], vn[ch * D_CHUNK:(ch + 1) * D_CHUNK], preferred_element_type=F32)
            parts.append(sv + spb_ref[:, g:g + 1])
        cols.append(u[:, lo_c:hi_c] * jnp.concatenate(parts, axis=0))
    d_mix = jnp.concatenate(cols, axis=1).astype(BF16)
    for rs in _sub_tiles(bm):
        acc = (jnp.dot(c_ref[0, rs, :].astype(BF16), wo_ref[0:C_WIDTH, :], preferred_element_type=F32)
               + jnp.dot(d_mix[rs], wo_ref[C_WIDTH:, :], preferred_element_type=F32))
        _post_mix(acc, rs, x_ref, g1_ref, lnw_ref, lnb_ref, sc2_ref, sh2_ref, rw_ref, rb_ref,
                  x1_ref, h2_ref, ti_ref, tg_ref)


def _outproj(kind, mixer_args, mixer_specs, wo, x, mods, mod_row, lnw, lnb, rw_t, rb, tok_state, tok_off, bm):
    bn, t, d = x.shape
    nt = t // bm
    g1, sc2, sh2 = mods
    h2_all, ti_all, tg_all = tok_state
    ntok = h2_all.shape[0]
    off = tok_off // bm
    mod_spec = pl.BlockSpec((1, 1, d), lambda b, i: (mod_row(b), 0, 0))
    vec_spec = pl.BlockSpec((1, d), lambda b, i: (0, 0))
    in_specs = list(mixer_specs) + [
        pl.BlockSpec((d, d), lambda b, i: (0, 0)),
        pl.BlockSpec((1, bm, d), lambda b, i: (b, i, 0)),
        mod_spec, vec_spec, vec_spec, mod_spec, mod_spec,
        pl.BlockSpec((N_EXPERTS, d), lambda b, i: (0, 0)),
        pl.BlockSpec((N_EXPERTS, 1), lambda b, i: (0, 0)),
        pl.BlockSpec(memory_space=pl.ANY), pl.BlockSpec(memory_space=pl.ANY), pl.BlockSpec(memory_space=pl.ANY)]
    n_in = len(in_specs)
    kernel = _outproj_even_kernel if kind == "even" else _outproj_odd_kernel

    def body(*refs):
        ins, outs = refs[:n_in], refs[n_in:]
        kernel(*ins[:n_in - 3], ins[n_in - 3:], *outs)

    return pl.pallas_call(
        body,
        grid=(bn, nt),
        in_specs=in_specs,
        out_specs=[pl.BlockSpec((1, bm, d), lambda b, i: (b, i, 0)),
                   pl.BlockSpec((bm, d), lambda b, i: (off + b * nt + i, 0)),
                   pl.BlockSpec((TOP_K, bm), lambda b, i: (0, off + b * nt + i)),
                   pl.BlockSpec((TOP_K, bm), lambda b, i: (0, off + b * nt + i))],
        out_shape=[jax.ShapeDtypeStruct((bn, t, d), F32),
                   jax.ShapeDtypeStruct((ntok, d), BF16),
                   jax.ShapeDtypeStruct((TOP_K, ntok), jnp.int32),
                   jax.ShapeDtypeStruct((TOP_K, ntok), F32)],
        input_output_aliases={n_in - 3: 1, n_in - 2: 2, n_in - 1: 3},
        compiler_params=_params("parallel", "parallel"),
        name="out_proj_" + kind,
    )(*mixer_args, wo, x, g1, lnw.reshape(1, d), lnb.reshape(1, d), sc2, sh2, rw_t, rb.reshape(N_EXPERTS, 1),
      h2_all, ti_all, tg_all)


def _outproj_even(a_mix, zin, conv_w, **kw):
    bm = kw["bm"]
    t = zin.shape[1]
    hb = bm // HALO
    last = t // HALO - 1
    cur = lambda sec: pl.BlockSpec((1, bm, B_WIDTH), lambda b, i: (b, i, sec))
    before = lambda sec: pl.BlockSpec((1, HALO, B_WIDTH), lambda b, i: (b, jnp.maximum(i * hb - 1, 0), sec))
    after = lambda sec: pl.BlockSpec((1, HALO, B_WIDTH), lambda b, i: (b, jnp.minimum((i + 1) * hb, last), sec))
    specs = [pl.BlockSpec((1, bm, A_WIDTH), lambda b, i: (b, i, 0)), cur(5), cur(6), cur(7),
             before(5), before(7), after(5), after(7),
             pl.BlockSpec((CONV_W, B_WIDTH), lambda b, i: (0, 0))]
    return _outproj("even", [a_mix, zin, zin, zin, zin, zin, zin, zin, conv_w], specs, **kw)


def _outproj_odd(c_mix, zin, d_norm_w, d_norm_b, sp_w, sp_b, **kw):
    bm = kw["bm"]
    cur = lambda sec: pl.BlockSpec((1, bm, D_WIDTH), lambda b, i: (b, i, sec))
    specs = [pl.BlockSpec((1, bm, C_WIDTH), lambda b, i: (b, i, 0)), cur(1), cur(2),
             pl.BlockSpec((1, D_WIDTH), lambda b, i: (0, 0)), pl.BlockSpec((1, D_WIDTH), lambda b, i: (0, 0)),
             pl.BlockSpec((D_GROUPS, D_CHUNK, D_CHUNK), lambda b, i: (0, 0, 0)),
             pl.BlockSpec((D_CHUNK, D_GROUPS), lambda b, i: (0, 0))]
    return _outproj("odd", [c_mix, zin, zin, d_norm_w.reshape(1, D_WIDTH), d_norm_b.reshape(1, D_WIDTH),
                            sp_w.astype(BF16), sp_b.T], specs, **kw)


def _dft_mats(n):
    ang = 2.0 * np.pi * np.outer(np.arange(n), np.arange(n)) / n
    return np.cos(ang), np.sin(ang)


def _fourier_cw_kernel(z_ref, nw_ref, cs_ref, kc_ref, ks_ref, p_ref, q_ref, ps_ref, qs_ref):
    tile = z_ref.shape[1]
    half = 256
    grows = tile // GRID_W
    for hs in range(tile // half):
        z = z_ref[0, hs * half:(hs + 1) * half, :].astype(F32)
        a_parts, b_parts = [], []
        for g in range(C_GROUPS):
            zg = z[:, g * C_GROUP:(g + 1) * C_GROUP]
            zg = zg * lax.rsqrt(jnp.mean(zg * zg, axis=-1, keepdims=True) + EPS) * nw_ref[:, g * C_GROUP:(g + 1) * C_GROUP]
            ab = jnp.dot(zg.astype(BF16), cs_ref[...], preferred_element_type=F32)
            a_parts.append(ab[:, :C_GROUP])
            b_parts.append(ab[:, C_GROUP:])
        ab = jnp.concatenate(a_parts + b_parts, axis=1).astype(BF16)
        m1 = jnp.dot(kc_ref[...], ab, preferred_element_type=F32)
        m2 = jnp.dot(ks_ref[...], ab, preferred_element_type=F32)
        pv = m1[:, :C_WIDTH] - m2[:, C_WIDTH:]
        qv = m2[:, :C_WIDTH] + m1[:, C_WIDTH:]
        for g in range(C_GROUPS):
            ps_ref[g, hs * half:(hs + 1) * half, :] = pv[:, g * C_GROUP:(g + 1) * C_GROUP]
            qs_ref[g, hs * half:(hs + 1) * half, :] = qv[:, g * C_GROUP:(g + 1) * C_GROUP]
    for k2 in range(GRID_W):
        for g in range(C_GROUPS):
            p_ref[0, k2, :, g * C_GROUP:(g + 1) * C_GROUP] = ps_ref[g, pl.ds(k2, grows, stride=GRID_W), :]
            q_ref[0, k2, :, g * C_GROUP:(g + 1) * C_GROUP] = qs_ref[g, pl.ds(k2, grows, stride=GRID_W), :]


def _fourier_r_kernel(p_ref, q_ref, c_ref, s_ref, o_ref, ys_ref):
    nj = p_ref.shape[1]
    rows = p_ref.shape[2]
    for j in range(nj):
        y = (jnp.dot(c_ref[...], p_ref[0, j].astype(BF16), preferred_element_type=F32)
             + jnp.dot(s_ref[...], q_ref[0, j].astype(BF16), preferred_element_type=F32))
        for g in range(C_GROUPS):
            ys_ref[g, j * rows:(j + 1) * rows, :] = y[:, g * C_GROUP:(g + 1) * C_GROUP]

    def put(k1, carry):
        for g in range(C_GROUPS):
            o_ref[0, k1, :, g * C_GROUP:(g + 1) * C_GROUP] = ys_ref[g, pl.ds(k1, nj, stride=rows), :]
        return carry

    lax.fori_loop(0, rows, put, 0)


def _fourier_mix(zin, c_norm_w):
    bn, t, _ = zin.shape
    rows = t // GRID_W
    tile = 512
    grows = tile // GRID_W
    c3, s3 = _dft_mats(C_GROUP)
    c2, s2 = _dft_mats(GRID_W)
    c1, s1 = _dft_mats(rows)
    scale = 1.0 / math.sqrt(rows * GRID_W * C_GROUP)
    cs3 = jnp.asarray(np.concatenate([c3, s3], axis=1), BF16)
    eye = np.eye(256 // GRID_W)
    kc2 = jnp.asarray(np.kron(eye, c2), BF16)
    ks2 = jnp.asarray(np.kron(eye, s2), BF16)
    c1s = jnp.asarray(c1 * scale, BF16)
    s1s = jnp.asarray(-s1 * scale, BF16)
    pq_shape = jax.ShapeDtypeStruct((bn, GRID_W, rows, C_WIDTH), F32)
    pq_spec = pl.BlockSpec((1, GRID_W, grows, C_WIDTH), lambda b, i: (b, 0, i, 0))
    p, q = pl.pallas_call(
        _fourier_cw_kernel,
        grid=(bn, t // tile),
        in_specs=[pl.BlockSpec((1, tile, C_WIDTH), lambda b, i: (b, i, 0)),
                  pl.BlockSpec((1, C_WIDTH), lambda b, i: (0, 0)),
                  pl.BlockSpec((C_GROUP, 2 * C_GROUP), lambda b, i: (0, 0)),
                  pl.BlockSpec((256, 256), lambda b, i: (0, 0)),
                  pl.BlockSpec((256, 256), lambda b, i: (0, 0))],
        out_specs=[pq_spec, pq_spec],
        out_shape=[pq_shape, pq_shape],
        scratch_shapes=[pltpu.VMEM((C_GROUPS, tile, C_GROUP), F32), pltpu.VMEM((C_GROUPS, tile, C_GROUP), F32)],
        compiler_params=_params("parallel", "parallel"),
        name="fourier_chan_col",
    )(zin, c_norm_w.reshape(1, C_WIDTH), cs3, kc2, ks2)
    nj = 8
    in_spec = pl.BlockSpec((1, nj, rows, C_WIDTH), lambda b, j: (b, j, 0, 0))
    mat_spec = pl.BlockSpec((rows, rows), lambda b, j: (0, 0))
    y = pl.pallas_call(
        _fourier_r_kernel,
        grid=(bn, GRID_W // nj),
        in_specs=[in_spec, in_spec, mat_spec, mat_spec],
        out_specs=pl.BlockSpec((1, rows, nj, C_WIDTH), lambda b, j: (b, 0, j, 0)),
        out_shape=jax.ShapeDtypeStruct((bn, rows, GRID_W, C_WIDTH), F32),
        scratch_shapes=[pltpu.VMEM((C_GROUPS, nj * rows, C_GROUP), F32)],
        compiler_params=_params("parallel", "parallel"),
        name="fourier_rows",
    )(p, q, c1s, s1s)
    return y.reshape(bn, t, C_WIDTH)


def _rank_kernel(ti_ref, rank_ref, cnt_ref, run_ref):
    @pl.when(pl.program_id(0) == 0)
    def _():
        run_ref[...] = jnp.zeros_like(run_ref)

    ti = ti_ref[...]
    tb = ti.shape[1]
    eio = lax.broadcasted_iota(jnp.int32, (N_EXPERTS, tb), 0)
    hot = [eio == ti[k:k + 1, :] for k in range(TOP_K)]
    occ = sum(h.astype(F32) for h in hot)
    si = lax.broadcasted_iota(jnp.int32, (tb, tb), 0)
    ti_ = lax.broadcasted_iota(jnp.int32, (tb, tb), 1)
    before = jnp.where(si < ti_, 1.0, 0.0).astype(BF16)
    seen = jnp.dot(occ.astype(BF16), before, preferred_element_type=F32) + run_ref[:, 0:1]
    rank_ref[...] = jnp.concatenate(
        [jnp.sum(jnp.where(h, seen, 0.0), axis=0, keepdims=True) for h in hot], axis=0).astype(jnp.int32)
    run_ref[...] = run_ref[...] + jnp.sum(occ, axis=1, keepdims=True)
    cnt_ref[...] = run_ref[...].astype(jnp.int32)


def _ranks(top_i):
    ntok = top_i.shape[1]
    tb = ROUTE_TILE if ntok % ROUTE_TILE == 0 else 256
    rank, cnt = pl.pallas_call(
        _rank_kernel,
        grid=(ntok // tb,),
        in_specs=[pl.BlockSpec((TOP_K, tb), lambda i: (0, i))],
        out_specs=[pl.BlockSpec((TOP_K, tb), lambda i: (0, i)),
                   pl.BlockSpec((N_EXPERTS, 128), lambda i: (0, 0))],
        out_shape=[jax.ShapeDtypeStruct((TOP_K, ntok), jnp.int32),
                   jax.ShapeDtypeStruct((N_EXPERTS, 128), jnp.int32)],
        scratch_shapes=[pltpu.VMEM((N_EXPERTS, 128), F32)],
        compiler_params=_params("arbitrary"),
        name="route_rank",
    )(top_i)
    return rank, cnt[:, 0]


def _w1_split_kernel(w_ref, g_ref, l_ref, s_ref):
    kdim, cols = w_ref.shape[2], w_ref.shape[3]
    half = cols // 2
    nk = kdim // LANES
    t = w_ref[0, 0].T
    for kc in range(nk):
        s_ref[kc] = t[:, kc * LANES:(kc + 1) * LANES]
    even = jnp.concatenate([s_ref[kc, pl.ds(0, half, stride=2), :] for kc in range(nk)], axis=1)
    odd = jnp.concatenate([s_ref[kc, pl.ds(1, half, stride=2), :] for kc in range(nk)], axis=1)
    g_ref[0] = even.T.astype(BF16)
    l_ref[0] = odd.T.astype(BF16)


def _w1_split(w1_all, layer):
    _, e, d, f2 = w1_all.shape
    cols = 1024
    out = jax.ShapeDtypeStruct((e, d, f2 // 2), BF16)
    out_spec = pl.BlockSpec((1, d, cols // 2), lambda i, c: (i, 0, c))
    return pl.pallas_call(
        _w1_split_kernel,
        grid=(e, f2 // cols),
        in_specs=[pl.BlockSpec((1, 1, d, cols), lambda i, c: (layer, i, 0, c))],
        out_specs=[out_spec, out_spec],
        out_shape=[out, out],
        scratch_shapes=[pltpu.VMEM((d // LANES, cols, LANES), F32)],
        compiler_params=_params("parallel", "parallel"),
        name="moe_w1_split",
    )(w1_all)


def _cast_kernel(w_ref, o_ref):
    o_ref[...] = w_ref[0].astype(o_ref.dtype)


def _w2_cast(w2_all, layer):
    _, e, f, d = w2_all.shape
    return pl.pallas_call(
        _cast_kernel,
        grid=(e,),
        in_specs=[pl.BlockSpec((1, 1, f, d), lambda i: (layer, i, 0, 0))],
        out_specs=pl.BlockSpec((1, f, d), lambda i: (i, 0, 0)),
        out_shape=jax.ShapeDtypeStruct((e, f, d), BF16),
        compiler_params=_params("parallel"),
        name="moe_w2_cast",
    )(w2_all)


def _expert_kernel(be_ref, used_ref, x_ref, w1g_ref, w1l_ref, w2_ref, b1g_ref, b1l_ref, b2_ref, y_ref):
    del be_ref
    rows = x_ref.shape[0] // ROW_TILES
    in_use = pl.program_id(0) < used_ref[0]

    @pl.when(in_use)
    def _():
        x = _load_row_tiles(x_ref, rows).astype(BF16)
        glu = jnp.minimum(jnp.dot(x, w1g_ref[0], preferred_element_type=F32) + b1g_ref[0], SWIGLU_LIMIT)
        lin = jnp.clip(jnp.dot(x, w1l_ref[0], preferred_element_type=F32) + b1l_ref[0],
                       -SWIGLU_LIMIT, SWIGLU_LIMIT)
        act = glu * _sigmoid(SWIGLU_ALPHA * glu) * (lin + 1.0)
        _store_row_tiles(y_ref, jnp.dot(act.astype(BF16), w2_ref[0], preferred_element_type=F32) + b2_ref[0])

    @pl.when(jnp.logical_not(in_use))
    def _():
        y_ref[...] = jnp.zeros_like(y_ref)


def _experts(x_sorted, block_expert, blocks_used, w1g, w1l, w2, b1g, b1l, b2):
    d, f = w1g.shape[1], w1g.shape[2]
    nb = block_expert.shape[0]
    buf_rows = MOE_ROWS * ROW_TILES
    wspec = lambda k, n: pl.BlockSpec((1, k, n), lambda j, be, nu: (be[j], 0, 0))
    row_spec = pl.BlockSpec((buf_rows, LANES), lambda j, be, nu: (j, 0))
    grid_spec = pltpu.PrefetchScalarGridSpec(
        num_scalar_prefetch=2,
        grid=(nb,),
        in_specs=[row_spec, wspec(d, f), wspec(d, f), wspec(f, d), wspec(1, f), wspec(1, f), wspec(1, d)],
        out_specs=row_spec)
    return pl.pallas_call(
        _expert_kernel,
        grid_spec=grid_spec,
        out_shape=jax.ShapeDtypeStruct(x_sorted.shape, F32),
        compiler_params=_params("parallel"),
        name="moe_experts",
    )(block_expert, blocks_used, x_sorted, w1g, w1l, w2, b1g, b1l, b2)


def _dispatch_kernel(cnt_ref, start_ref, dst_ref, h_ref, xs_hbm, rowbuf, zero_ref, sem, zsem, *, nt, nslots):
    i = pl.program_id(0)
    tb = h_ref.shape[0]
    slot = i % 2

    def landed(buf):
        for _ in range(TOP_K):
            pltpu.make_async_copy(rowbuf.at[buf], xs_hbm.at[pl.ds(0, tb * ROW_TILES), :], sem.at[buf]).wait()

    @pl.when(i >= 2)
    def _():
        landed(slot)

    _store_row_tiles(rowbuf.at[slot], h_ref[...].astype(F32))
    for r in range(tb):
        for k in range(TOP_K):
            dst = pl.multiple_of(dst_ref[0, k, r] * ROW_TILES, ROW_TILES)
            pltpu.make_async_copy(rowbuf.at[slot, pl.ds(r * ROW_TILES, ROW_TILES), :],
                                  xs_hbm.at[pl.ds(dst, ROW_TILES), :], sem.at[slot]).start(priority=k % 2)

    @pl.when(i == nt - 1)
    def _():
        landed(slot)
        if nt > 1:
            landed(1 - slot)
        zero_ref[...] = jnp.zeros_like(zero_ref)

        zrows = zero_ref.shape[0] // ROW_TILES

        def zero_rows(pos, n):
            cp = pltpu.make_async_copy(
                zero_ref.at[pl.ds(0, n * ROW_TILES), :],
                xs_hbm.at[pl.ds(pl.multiple_of(pos * ROW_TILES, ROW_TILES), n * ROW_TILES), :], zsem)
            cp.start()
            cp.wait()

        end = 0
        for e in range(N_EXPERTS):
            pos = start_ref[e] + cnt_ref[e]
            end = start_ref[e] + (cnt_ref[e] + MOE_ROWS - 1) // MOE_ROWS * MOE_ROWS
            gap = end - pos
            bit = zrows
            while bit >= 1:
                take = (gap & bit) != 0
                pl.when(take)(functools.partial(zero_rows, pos, bit))
                pos = pos + jnp.where(take, bit, 0)
                bit //= 2

        def tail(i, carry):
            zero_rows(end + i * zrows, zrows)
            return carry
        lax.fori_loop(0, (nslots - end) // zrows, tail, 0)


def _dispatch(h2, dest3, counts, pad_starts, nslots):
    ntok, d = h2.shape
    tb = COMBINE_TILE
    nt = ntok // tb
    assert MOE_ROWS & (MOE_ROWS - 1) == 0, "the padding fill decomposes gap lengths into powers of two"
    grid_spec = pltpu.PrefetchScalarGridSpec(
        num_scalar_prefetch=2,
        grid=(nt,),
        in_specs=[pl.BlockSpec((1, TOP_K, tb), lambda i, c, s: (i, 0, 0), memory_space=pltpu.SMEM),
                  pl.BlockSpec((tb, d), lambda i, c, s: (i, 0))],
        out_specs=pl.BlockSpec(memory_space=pl.ANY),
        scratch_shapes=[pltpu.VMEM((2, tb * ROW_TILES, LANES), F32),
                        pltpu.VMEM((MOE_ROWS // 2 * ROW_TILES, LANES), F32),
                        pltpu.SemaphoreType.DMA((2,)), pltpu.SemaphoreType.DMA(())])
    return pl.pallas_call(
        functools.partial(_dispatch_kernel, nt=nt, nslots=nslots),
        grid_spec=grid_spec,
        out_shape=jax.ShapeDtypeStruct((nslots * ROW_TILES, LANES), F32),
        compiler_params=_params("arbitrary"),
        name="moe_dispatch",
    )(counts, pad_starts, dest3, h2)


def _combine_kernel(dst_ref, dstn_ref, y_hbm, x1_ref, gate_ref, g2_ref, lnw_ref, lnb_ref, o_ref, ybuf, sem, *, nt):
    i = pl.program_id(0)
    tb = x1_ref.shape[0]
    slot = i % 2

    def row_copy(src, buf, k, r):
        rows = y_hbm.at[pl.ds(pl.multiple_of(src * ROW_TILES, ROW_TILES), ROW_TILES), :]
        return pltpu.make_async_copy(rows, ybuf.at[buf, k, pl.ds(r * ROW_TILES, ROW_TILES), :], sem.at[buf])

    def gather(idx_ref, buf):
        for r in range(tb):
            for k in range(TOP_K):
                row_copy(idx_ref[0, k, r], buf, k, r).start(priority=k % 2)

    @pl.when(i == 0)
    def _():
        gather(dst_ref, 0)

    @pl.when(i + 1 < nt)
    def _():
        gather(dstn_ref, 1 - slot)

    for k in range(TOP_K):
        pltpu.make_async_copy(y_hbm.at[pl.ds(0, tb * ROW_TILES), :], ybuf.at[slot, k], sem.at[slot]).wait()

    y = sum(gate_ref[:, k:k + 1] * _load_row_tiles(ybuf.at[slot, k], tb) for k in range(TOP_K))
    r = DN_ALPHA * x1_ref[...] + g2_ref[0] * y
    mu = jnp.mean(r, axis=-1, keepdims=True)
    rc = r - mu
    var = jnp.mean(rc * rc, axis=-1, keepdims=True)
    o_ref[...] = rc * lax.rsqrt(var + EPS) * lnw_ref[...] + lnb_ref[...]


def _combine(x1, y_sorted, dest3, gates_t, tok_off, g2, mod_row, lnw, lnb):
    bn, t, d = x1.shape
    n = bn * t
    tb = COMBINE_TILE
    nt = n // tb
    off = tok_off // tb
    per_b = t // tb
    out = pl.pallas_call(
        functools.partial(_combine_kernel, nt=nt),
        grid=(nt,),
        in_specs=[pl.BlockSpec((1, TOP_K, tb), lambda i: (off + i, 0, 0), memory_space=pltpu.SMEM),
                  pl.BlockSpec((1, TOP_K, tb), lambda i: (off + jnp.minimum(i + 1, nt - 1), 0, 0),
                               memory_space=pltpu.SMEM),
                  pl.BlockSpec(memory_space=pl.ANY),
                  pl.BlockSpec((tb, d), lambda i: (i, 0)),
                  pl.BlockSpec((tb, TOP_K), lambda i: (off + i, 0)),
                  pl.BlockSpec((1, 1, d), lambda i: (mod_row(i // per_b), 0, 0)),
                  pl.BlockSpec((1, d), lambda i: (0, 0)),
                  pl.BlockSpec((1, d), lambda i: (0, 0))],
        out_specs=pl.BlockSpec((tb, d), lambda i: (i, 0)),
        out_shape=jax.ShapeDtypeStruct((n, d), F32),
        scratch_shapes=[pltpu.VMEM((2, TOP_K, tb * ROW_TILES, LANES), F32), pltpu.SemaphoreType.DMA((2,))],
        compiler_params=_params("arbitrary"),
        name="moe_combine",
    )(dest3, dest3, y_sorted, x1.reshape(n, d), gates_t, g2, lnw.reshape(1, d), lnb.reshape(1, d))
    return out.reshape(bn, t, d)


def _dest_kernel(ti_ref, rank_ref, start_ref, dest_ref):
    ti = ti_ref[...]
    tb = ti.shape[1]
    eio = lax.broadcasted_iota(jnp.int32, (N_EXPERTS, tb), 0)
    start = start_ref[:, 0:1].astype(F32)
    base = jnp.concatenate(
        [jnp.sum(jnp.where(eio == ti[k:k + 1, :], start, 0.0), axis=0, keepdims=True) for k in range(TOP_K)], axis=0)
    dest = base.astype(jnp.int32) + rank_ref[...]
    for c in range(tb // COMBINE_TILE):
        dest_ref[c] = dest[:, c * COMBINE_TILE:(c + 1) * COMBINE_TILE]


def _moe_layout(top_i):
    ntok = top_i.shape[1]
    m = ntok * TOP_K
    rank, counts = _ranks(top_i)
    padded = (counts + MOE_ROWS - 1) // MOE_ROWS * MOE_ROWS
    pad_ends = jnp.cumsum(padded)
    pad_starts = pad_ends - padded
    tb = ROUTE_TILE if ntok % ROUTE_TILE == 0 else 256
    ntiles = ntok // COMBINE_TILE
    dest = pl.pallas_call(
        _dest_kernel,
        grid=(ntok // tb,),
        in_specs=[pl.BlockSpec((TOP_K, tb), lambda i: (0, i)),
                  pl.BlockSpec((TOP_K, tb), lambda i: (0, i)),
                  pl.BlockSpec((N_EXPERTS, LANES), lambda i: (0, 0))],
        out_specs=pl.BlockSpec((tb // COMBINE_TILE, TOP_K, COMBINE_TILE), lambda i: (i, 0, 0)),
        out_shape=jax.ShapeDtypeStruct((ntiles, TOP_K, COMBINE_TILE), jnp.int32),
        compiler_params=_params("parallel"),
        name="route_dest",
    )(top_i, rank, jnp.broadcast_to(pad_starts[:, None], (N_EXPERTS, LANES)))
    nb = (m + N_EXPERTS * (MOE_ROWS - 1) + MOE_ROWS - 1) // MOE_ROWS
    block_starts = jnp.arange(nb, dtype=jnp.int32) * MOE_ROWS
    block_expert = jnp.minimum(jnp.sum(block_starts[:, None] >= pad_ends[None, :], axis=1),
                               N_EXPERTS - 1).astype(jnp.int32)
    blocks_used = (pad_ends[-1:] // MOE_ROWS).astype(jnp.int32)
    return dest, counts, pad_starts.astype(jnp.int32), block_expert, blocks_used, nb * MOE_ROWS


def kernel(x, c, ctx, c_ctx, lower_bounds_fwd, lower_bounds_bwd, ada_w, ada_b, w_in_even, a_norm_w, conv_w,
           w_in_odd, c_norm_w, d_norm_w, d_norm_b, spatial_w, spatial_b, w_out, ln_mix_w, ln_mix_b,
           ln_ffn_w, ln_ffn_b, router_w, router_b, moe_w1, moe_b1, moe_w2, moe_b2):
    bn, t, d = x.shape
    tc = ctx.shape[1]
    n_lat = bn * t
    n_ctx = bn * tc
    ctx_row = bn
    lat_row = lambda b: b
    ctx_mod_row = lambda b: ctx_row

    lb_f_all = jnp.cumsum(jax.nn.softmax(lower_bounds_fwd.astype(F32), axis=0), axis=0)
    lb_b_all = jnp.cumsum(jax.nn.softmax(lower_bounds_bwd.astype(F32), axis=0), axis=0)

    cond = jnp.zeros((MOD_ROWS, d), F32).at[:bn].set(c).at[ctx_row].set(c_ctx)
    mods = _ada(cond, ada_w, ada_b).reshape(DEPTH, MOD_ROWS, 6, 1, d)

    h_ctx = ctx
    for l in range(DEPTH):
        last = l == DEPTH - 1
        sh1, sc1, g1, sh2, sc2, g2 = (mods[l, :, j] for j in range(6))
        wo = w_out[l].astype(BF16)
        rw_t = router_w[l].T.astype(BF16)
        ntok = n_lat if last else n_lat + n_ctx
        tok_state = (jnp.zeros((ntok, d), BF16), jnp.zeros((TOP_K, ntok), jnp.int32),
                     jnp.zeros((TOP_K, ntok), F32))
        post = dict(wo=wo, lnw=ln_mix_w[l], lnb=ln_mix_b[l], rw_t=rw_t, rb=router_b[l])
        bm_ctx = min(ROW_TILE, tc)
        if l % 2 == 0:
            e = l // 2
            w_in = w_in_even[e].astype(BF16)
            zin_c = _inproj(h_ctx, sc1, sh1, w_in, ctx_mod_row, bm=bm_ctx)
            zin = _inproj(x, sc1, sh1, w_in, lat_row, bm=ROW_TILE)
            zeros = jnp.zeros((bn, A_HEADS, A_DV, A_DK), F32)
            oc_f, sc_f = _scan(zin_c, lb_f_all[l], zeros, reverse=False)
            a_ctx, sc_b = _scan(zin_c, lb_b_all[l], zeros, reverse=True, o_prev=oc_f, norm_w=a_norm_w[e])
            o_f, _ = _scan(zin, lb_f_all[l], sc_f, reverse=False)
            a_lat, _ = _scan(zin, lb_b_all[l], sc_b, reverse=True, o_prev=o_f, norm_w=a_norm_w[e])
            x1, *tok_state = _outproj_even(a_lat, zin, conv_w[e], x=x, mods=(g1, sc2, sh2), mod_row=lat_row,
                                           tok_state=tok_state, tok_off=0, bm=ROW_TILE, **post)
            if not last:
                hc1, *tok_state = _outproj_even(a_ctx, zin_c, conv_w[e], x=h_ctx, mods=(g1, sc2, sh2),
                                                mod_row=ctx_mod_row, tok_state=tok_state, tok_off=n_lat,
                                                bm=bm_ctx, **post)
        else:
            o = l // 2
            w_in = w_in_odd[o].astype(BF16)
            odd = dict(d_norm_w=d_norm_w[o], d_norm_b=d_norm_b[o], sp_w=spatial_w[o], sp_b=spatial_b[o])
            zin = _inproj(x, sc1, sh1, w_in, lat_row, bm=ROW_TILE)
            c_mix = _fourier_mix(zin, c_norm_w[o])
            x1, *tok_state = _outproj_odd(c_mix, zin, x=x, mods=(g1, sc2, sh2), mod_row=lat_row,
                                          tok_state=tok_state, tok_off=0, bm=ROW_TILE, **odd, **post)
            if not last:
                raise NotImplementedError("an odd layer that is not the last needs the context Fourier mixer")
        h2_all, top_i, top_g = tok_state
        dest, counts, pad_starts, block_expert, blocks_used, nslots = _moe_layout(top_i)
        x_sorted = _dispatch(h2_all, dest, counts, pad_starts, nslots)
        w1g, w1l = _w1_split(moe_w1, l)
        y_sorted = _experts(
            x_sorted, block_expert, blocks_used, w1g, w1l, _w2_cast(moe_w2, l),
            moe_b1[l][:, None, 0::2], moe_b1[l][:, None, 1::2], moe_b2[l][:, None, :])
        gates_t = top_g.T
        x = _combine(x1, y_sorted, dest, gates_t, 0, g2, lat_row, ln_ffn_w[l], ln_ffn_b[l])
        if not last:
            h_ctx = _combine(hc1, y_sorted, dest, gates_t, n_lat, g2, ctx_mod_row, ln_ffn_w[l], ln_ffn_b[l])
    return x
```

```python
import functools
import math

import numpy as np
import jax
import jax.numpy as jnp
from jax import lax
from jax.experimental import pallas as pl
from jax.experimental.pallas import tpu as pltpu

F32 = jnp.float32
BF16 = jnp.bfloat16

D_MODEL = 1024
DEPTH = 2
GRID_W = 64
A_HEADS = 4
A_DK = 128
A_DV = 128
A_KW = A_HEADS * A_DK
A_WIDTH = A_HEADS * A_DV
B_WIDTH = D_MODEL - A_WIDTH
CONV_W = 3
C_GROUPS = 4
C_GROUP = 128
C_WIDTH = C_GROUPS * C_GROUP
D_WIDTH = D_MODEL - C_WIDTH
D_GROUPS = 4
D_GDIM = D_WIDTH // D_GROUPS
D_CHUNK = 128
SCAN_CHUNK = 32
EVEN_IN = 3 * A_KW + 2 * A_WIDTH + 3 * B_WIDTH
ODD_IN = C_WIDTH + 2 * D_WIDTH
N_EXPERTS = 32
TOP_K = 4
D_EXPERT = 1024
SWIGLU_ALPHA = 1.702
SWIGLU_LIMIT = 7.0
DN_ALPHA = (2 * DEPTH) ** 0.25
EPS = 1e-5

MOD_ROWS = 8
SCAN_ROWS = 256
SCAN_HEADS = 4
ROW_TILE = 512
SUB_TILE = 256
MOE_ROWS = 512
ROUTE_TILE = 512
COMBINE_TILE = 128
HALO = 16
LANES = 128
ROW_TILES = D_MODEL // LANES
ROW_PITCH = ROW_TILES + 1

_NT = (((1,), (1,)), ((), ()))
_TN = (((0,), (0,)), ((), ()))


def _sigmoid(a):
    return 1.0 / (1.0 + jnp.exp(-a))


def _silu(a):
    return a * _sigmoid(a)


def _gelu(a):
    return 0.5 * a * (1.0 + lax.erf(a * (1.0 / math.sqrt(2.0))))


def _store_row_tiles(ref, val, pitch=ROW_TILES):
    rows = val.shape[0]
    for c in range(ROW_TILES):
        ref[pl.ds(c, rows, stride=pitch), :] = val[:, c * LANES:(c + 1) * LANES]


def _load_row_tiles(ref, rows, pitch=ROW_TILES):
    return jnp.concatenate([ref[pl.ds(c, rows, stride=pitch), :] for c in range(ROW_TILES)], axis=1)


def _params(*sem):
    return pltpu.CompilerParams(dimension_semantics=sem, vmem_limit_bytes=56 * 1024 * 1024)


def _ada_kernel(c_ref, w_ref, b_ref, o_ref):
    s = _silu(c_ref[...])
    o_ref[0] = jnp.dot(s, w_ref[0], preferred_element_type=F32, precision=lax.Precision.HIGHEST) + b_ref[0]


def _ada(cond, ada_w, ada_b):
    d = cond.shape[1]
    n6 = ada_w.shape[2]
    tn = 1536
    return pl.pallas_call(
        _ada_kernel,
        grid=(DEPTH, n6 // tn),
        in_specs=[pl.BlockSpec((MOD_ROWS, d), lambda l, n: (0, 0)),
                  pl.BlockSpec((1, d, tn), lambda l, n: (l, 0, n)),
                  pl.BlockSpec((1, 1, tn), lambda l, n: (l, 0, n))],
        out_specs=pl.BlockSpec((1, MOD_ROWS, tn), lambda l, n: (l, 0, n)),
        out_shape=jax.ShapeDtypeStruct((DEPTH, MOD_ROWS, n6), F32),
        compiler_params=_params("parallel", "parallel"),
        name="ada_mod",
    )(cond, ada_w, ada_b.reshape(DEPTH, 1, n6))


def _inproj_kernel(x_ref, sc_ref, sh_ref, w_ref, o_ref, *, tn):
    h = (x_ref[0] * (1.0 + sc_ref[0]) + sh_ref[0]).astype(BF16)
    for n in range(o_ref.shape[2] // tn):
        o_ref[0, :, n * tn:(n + 1) * tn] = jnp.dot(
            h, w_ref[:, n * tn:(n + 1) * tn], preferred_element_type=F32).astype(o_ref.dtype)


def _inproj(x, sc, sh, w, mod_row, bm):
    bn, t, d = x.shape
    nout = w.shape[1]
    mod_spec = pl.BlockSpec((1, 1, d), lambda b, i: (mod_row(b), 0, 0))
    return pl.pallas_call(
        functools.partial(_inproj_kernel, tn=512),
        grid=(bn, t // bm),
        in_specs=[pl.BlockSpec((1, bm, d), lambda b, i: (b, i, 0)), mod_spec, mod_spec,
                  pl.BlockSpec((d, nout), lambda b, i: (0, 0))],
        out_specs=pl.BlockSpec((1, bm, nout), lambda b, i: (b, i, 0)),
        out_shape=jax.ShapeDtypeStruct((bn, t, nout), BF16),
        compiler_params=_params("parallel", "parallel"),
        name="in_proj",
    )(x, sc, sh, w)


def _scan_kernel(*refs, reverse, finalize):
    if finalize:
        (q_ref, z_ref, v_ref, lb_ref, s0_ref, g_ref, op_ref, nw_ref, o_ref, sfin_ref, st_ref, oacc_ref) = refs
    else:
        (q_ref, z_ref, v_ref, lb_ref, s0_ref, o_ref, sfin_ref, st_ref, oacc_ref) = refs
    rows = q_ref.shape[1]
    heads = q_ref.shape[2] // A_DK
    nchunk = rows // SCAN_CHUNK

    @pl.when(pl.program_id(2) == 0)
    def _():
        st_ref[...] = s0_ref[0]

    q = _silu(q_ref[0].astype(F32))
    lb = lb_ref[...]
    f = lb + (1.0 - lb) * _sigmoid(z_ref[0].astype(F32))
    logf = jnp.log(f)
    k = 1.0 - f
    v = v_ref[0]

    ri = lax.broadcasted_iota(jnp.int32, (rows, rows), 0)
    ci = lax.broadcasted_iota(jnp.int32, (rows, rows), 1)
    same = (ri // SCAN_CHUNK) == (ci // SCAN_CHUNK)
    causal = same & ((ci >= ri) if reverse else (ci <= ri))
    tri = jnp.where(causal, 1.0, 0.0).astype(BF16)
    blk = jnp.where(same, 1.0, 0.0).astype(BF16)
    hi = logf.astype(BF16)
    lo = (logf - hi.astype(F32)).astype(BF16)
    b = jnp.dot(tri, hi, preferred_element_type=F32) + jnp.dot(tri, lo, preferred_element_type=F32)
    b_tot = jnp.dot(blk, hi, preferred_element_type=F32) + jnp.dot(blk, lo, preferred_element_type=F32)

    q_dec = (q * jnp.exp(b)).astype(BF16)
    k_inv = (k * jnp.exp(-b)).astype(BF16)
    k_end = (k * jnp.exp(b_tot - b)).astype(BF16)
    dec = jnp.exp(b_tot)

    wide = nchunk * A_DK
    spread = (lax.broadcasted_iota(jnp.int32, (rows, wide), 0) // SCAN_CHUNK
              == lax.broadcasted_iota(jnp.int32, (rows, wide), 1) // A_DK)
    order = range(nchunk - 1, -1, -1) if reverse else range(nchunk)
    for h in range(heads):
        hk = slice(h * A_DK, (h + 1) * A_DK)
        hv = slice(h * A_DV, (h + 1) * A_DV)
        qd, vh = q_dec[:, hk], v[:, hv]
        scores = lax.dot_general(qd, k_inv[:, hk], _NT, preferred_element_type=F32)
        scores = jnp.where(causal, scores, 0.0).astype(BF16)
        o_intra = jnp.dot(scores, vh, preferred_element_type=F32)
        ke_wide = jnp.where(spread, jnp.concatenate([k_end[:, hk]] * nchunk, axis=1), 0.0).astype(BF16)
        upd = lax.dot_general(vh, ke_wide, _TN, preferred_element_type=F32)
        st = st_ref[h]
        seen = [None] * nchunk
        for c in order:
            seen[c] = st.astype(BF16)
            r0 = c * SCAN_CHUNK
            st = st * dec[r0:r0 + 1, hk] + upd[:, c * A_DK:(c + 1) * A_DK]
        st_ref[h] = st
        sfin_ref[0, h] = st
        qd_wide = jnp.where(spread, jnp.concatenate([qd] * nchunk, axis=1), 0.0).astype(BF16)
        o_inter = lax.dot_general(qd_wide, jnp.concatenate(seen, axis=1), _NT, preferred_element_type=F32)
        oacc_ref[:, hv] = o_intra + o_inter

    o = oacc_ref[...]
    if finalize:
        o = o + op_ref[0]
        gate = nw_ref[...] * _silu(g_ref[0].astype(F32))
        for h in range(heads):
            hv = slice(h * A_DV, (h + 1) * A_DV)
            oh = o[:, hv]
            oh = oh * lax.rsqrt(jnp.mean(oh * oh, axis=-1, keepdims=True) + EPS)
            o_ref[0, :, hv] = (oh * gate[:, hv]).astype(o_ref.dtype)
    else:
        o_ref[0] = o.astype(o_ref.dtype)


def _scan(zin, lb, s0, *, reverse, o_prev=None, norm_w=None):
    bn, t, _ = zin.shape
    finalize = o_prev is not None
    rows = min(SCAN_ROWS, t)
    nb = t // rows
    pos = (lambda n: nb - 1 - n) if reverse else (lambda n: n)
    hp = SCAN_HEADS
    groups = A_HEADS // hp
    hw = hp * A_DK

    def col(section):
        return pl.BlockSpec((1, rows, hw), lambda b, h, n: (b, pos(n), section * groups + h))

    state_spec = pl.BlockSpec((1, hp, A_DV, A_DK), lambda b, h, n: (b, h, 0, 0))
    in_specs = [col(0), col(2 if reverse else 1), col(3),
                pl.BlockSpec((1, hw), lambda b, h, n: (0, h)), state_spec]
    args = [zin, zin, zin, lb.reshape(1, A_KW), s0]
    if finalize:
        in_specs += [col(4),
                     pl.BlockSpec((1, rows, hw), lambda b, h, n: (b, pos(n), h)),
                     pl.BlockSpec((1, hw), lambda b, h, n: (0, h))]
        args += [zin, o_prev, norm_w.reshape(1, A_WIDTH)]
    return pl.pallas_call(
        functools.partial(_scan_kernel, reverse=reverse, finalize=finalize),
        grid=(bn, groups, nb),
        in_specs=in_specs,
        out_specs=[pl.BlockSpec((1, rows, hw), lambda b, h, n: (b, pos(n), h)), state_spec],
        out_shape=[jax.ShapeDtypeStruct((bn, t, A_WIDTH), BF16 if finalize else F32),
                   jax.ShapeDtypeStruct((bn, A_HEADS, A_DV, A_DK), F32)],
        scratch_shapes=[pltpu.VMEM((hp, A_DV, A_DK), F32), pltpu.VMEM((rows, hw), F32)],
        compiler_params=_params("parallel", "parallel", "arbitrary"),
        name="hgrn2_scan_bwd" if reverse else "hgrn2_scan_fwd",
    )(*args)


def _sub_tiles(bm):
    sub = min(bm, SUB_TILE)
    return [slice(s * sub, (s + 1) * sub) for s in range(bm // sub)]


def _post_mix(acc, rs, x_ref, g1_ref, lnw_ref, lnb_ref, sc2_ref, sh2_ref, rw_ref, rb_ref,
              x1_ref, h2_ref, ti_ref, tg_ref):
    r = DN_ALPHA * x_ref[0, rs, :] + g1_ref[0] * acc
    mu = jnp.mean(r, axis=-1, keepdims=True)
    rc = r - mu
    var = jnp.mean(rc * rc, axis=-1, keepdims=True)
    x1 = rc * lax.rsqrt(var + EPS) * lnw_ref[...] + lnb_ref[...]
    x1_ref[0, rs, :] = x1
    h2 = x1 * (1.0 + sc2_ref[0]) + sh2_ref[0]
    h2_ref[rs, :] = h2.astype(h2_ref.dtype)
    logits = lax.dot_general(rw_ref[...], h2.astype(BF16), _NT, preferred_element_type=F32) + rb_ref[...]
    iota = lax.broadcasted_iota(jnp.int32, logits.shape, 0)
    vals, idxs = [], []
    cur = logits
    for _ in range(TOP_K):
        m = jnp.max(cur, axis=0, keepdims=True)
        ik = jnp.min(jnp.where(cur == m, iota, N_EXPERTS), axis=0, keepdims=True)
        vals.append(m)
        idxs.append(ik)
        cur = jnp.where(iota == ik, -jnp.inf, cur)
    tv = jnp.concatenate(vals, axis=0)
    e = jnp.exp(tv - tv[0:1])
    tg_ref[:, rs] = e / jnp.sum(e, axis=0, keepdims=True)
    ti_ref[:, rs] = jnp.concatenate(idxs, axis=0)


def _outproj_even_kernel(a_ref, u_ref, gb_ref, gc_ref, up_ref, gcp_ref, un_ref, gcn_ref, cw_ref, wo_ref,
                         x_ref, g1_ref, lnw_ref, lnb_ref, sc2_ref, sh2_ref, rw_ref, rb_ref, alias_ref,
                         x1_ref, h2_ref, ti_ref, tg_ref):
    del alias_ref
    i = pl.program_id(1)
    bm = u_ref.shape[1]
    z = gc_ref[0].astype(F32) * u_ref[0].astype(F32)
    z_before = gcp_ref[0, HALO - 1:HALO, :].astype(F32) * up_ref[0, HALO - 1:HALO, :].astype(F32)
    z_after = gcn_ref[0, 0:1, :].astype(F32) * un_ref[0, 0:1, :].astype(F32)
    z_before = jnp.where(i == 0, 0.0, z_before)
    z_after = jnp.where(i == pl.num_programs(1) - 1, 0.0, z_after)
    row = lax.broadcasted_iota(jnp.int32, z.shape, 0)
    z_prev = jnp.where(row == 0, z_before, pltpu.roll(z, 1, axis=0))
    z_next = jnp.where(row == bm - 1, z_after, pltpu.roll(z, bm - 1, axis=0))
    zc = cw_ref[0:1, :] * z_prev + cw_ref[1:2, :] * z + cw_ref[2:3, :] * z_next
    b_mix = (gb_ref[0].astype(F32) * zc).astype(BF16)
    for rs in _sub_tiles(bm):
        acc = (jnp.dot(a_ref[0, rs, :], wo_ref[0:A_WIDTH, :], preferred_element_type=F32)
               + jnp.dot(b_mix[rs], wo_ref[A_WIDTH:, :], preferred_element_type=F32))
        _post_mix(acc, rs, x_ref, g1_ref, lnw_ref, lnb_ref, sc2_ref, sh2_ref, rw_ref, rb_ref,
                  x1_ref, h2_ref, ti_ref, tg_ref)


def _outproj_odd_kernel(c_ref, u_ref, v_ref, dw_ref, db_ref, spw_ref, spb_ref, wo_ref,
                        x_ref, g1_ref, lnw_ref, lnb_ref, sc2_ref, sh2_ref, rw_ref, rb_ref, alias_ref,
                        x1_ref, h2_ref, ti_ref, tg_ref):
    del alias_ref
    bm = u_ref.shape[1]
    u = _gelu(u_ref[0].astype(F32))
    v = _gelu(v_ref[0].astype(F32))
    cols = []
    for g in range(D_GROUPS):
        lo_c, hi_c = g * D_GDIM, (g + 1) * D_GDIM
        vg = v[:, lo_c:hi_c]
        mu = jnp.mean(vg, axis=-1, keepdims=True)
        vc = vg - mu
        var = jnp.mean(vc * vc, axis=-1, keepdims=True)
        vn = (vc * lax.rsqrt(var + EPS) * dw_ref[:, lo_c:hi_c] + db_ref[:, lo_c:hi_c]).astype(BF16)
        parts = []
        for ch in range(bm // D_CHUNK):
            sv = jnp.dot(spw_ref[g], vn[ch * D_CHUNK:(ch + 1) * D_CHUNK], preferred_element_type=F32)
            parts.append(sv + spb_ref[:, g:g + 1])
        cols.append(u[:, lo_c:hi_c] * jnp.concatenate(parts, axis=0))
    d_mix = jnp.concatenate(cols, axis=1).astype(BF16)
    for rs in _sub_tiles(bm):
        acc = (jnp.dot(c_ref[0, rs, :].astype(BF16), wo_ref[0:C_WIDTH, :], preferred_element_type=F32)
               + jnp.dot(d_mix[rs], wo_ref[C_WIDTH:, :], preferred_element_type=F32))
        _post_mix(acc, rs, x_ref, g1_ref, lnw_ref, lnb_ref, sc2_ref, sh2_ref, rw_ref, rb_ref,
                  x1_ref, h2_ref, ti_ref, tg_ref)


def _outproj(kind, mixer_args, mixer_specs, wo, x, mods, mod_row, lnw, lnb, rw_t, rb, tok_state, tok_off, bm):
    bn, t, d = x.shape
    nt = t // bm
    g1, sc2, sh2 = mods
    h2_all, ti_all, tg_all = tok_state
    ntok = h2_all.shape[0]
    off = tok_off // bm
    mod_spec = pl.BlockSpec((1, 1, d), lambda b, i: (mod_row(b), 0, 0))
    vec_spec = pl.BlockSpec((1, d), lambda b, i: (0, 0))
    in_specs = list(mixer_specs) + [
        pl.BlockSpec((d, d), lambda b, i: (0, 0)),
        pl.BlockSpec((1, bm, d), lambda b, i: (b, i, 0)),
        mod_spec, vec_spec, vec_spec, mod_spec, mod_spec,
        pl.BlockSpec((N_EXPERTS, d), lambda b, i: (0, 0)),
        pl.BlockSpec((N_EXPERTS, 1), lambda b, i: (0, 0)),
        pl.BlockSpec(memory_space=pl.ANY), pl.BlockSpec(memory_space=pl.ANY), pl.BlockSpec(memory_space=pl.ANY)]
    n_in = len(in_specs)
    kernel = _outproj_even_kernel if kind == "even" else _outproj_odd_kernel

    def body(*refs):
        ins, outs = refs[:n_in], refs[n_in:]
        kernel(*ins[:n_in - 3], ins[n_in - 3:], *outs)

    return pl.pallas_call(
        body,
        grid=(bn, nt),
        in_specs=in_specs,
        out_specs=[pl.BlockSpec((1, bm, d), lambda b, i: (b, i, 0)),
                   pl.BlockSpec((bm, d), lambda b, i: (off + b * nt + i, 0)),
                   pl.BlockSpec((TOP_K, bm), lambda b, i: (0, off + b * nt + i)),
                   pl.BlockSpec((TOP_K, bm), lambda b, i: (0, off + b * nt + i))],
        out_shape=[jax.ShapeDtypeStruct((bn, t, d), F32),
                   jax.ShapeDtypeStruct((ntok, d), BF16),
                   jax.ShapeDtypeStruct((TOP_K, ntok), jnp.int32),
                   jax.ShapeDtypeStruct((TOP_K, ntok), F32)],
        input_output_aliases={n_in - 3: 1, n_in - 2: 2, n_in - 1: 3},
        compiler_params=_params("parallel", "parallel"),
        name="out_proj_" + kind,
    )(*mixer_args, wo, x, g1, lnw.reshape(1, d), lnb.reshape(1, d), sc2, sh2, rw_t, rb.reshape(N_EXPERTS, 1),
      h2_all, ti_all, tg_all)


def _outproj_even(a_mix, zin, conv_w, **kw):
    bm = kw["bm"]
    t = zin.shape[1]
    hb = bm // HALO
    last = t // HALO - 1
    cur = lambda sec: pl.BlockSpec((1, bm, B_WIDTH), lambda b, i: (b, i, sec))
    before = lambda sec: pl.BlockSpec((1, HALO, B_WIDTH), lambda b, i: (b, jnp.maximum(i * hb - 1, 0), sec))
    after = lambda sec: pl.BlockSpec((1, HALO, B_WIDTH), lambda b, i: (b, jnp.minimum((i + 1) * hb, last), sec))
    specs = [pl.BlockSpec((1, bm, A_WIDTH), lambda b, i: (b, i, 0)), cur(5), cur(6), cur(7),
             before(5), before(7), after(5), after(7),
             pl.BlockSpec((CONV_W, B_WIDTH), lambda b, i: (0, 0))]
    return _outproj("even", [a_mix, zin, zin, zin, zin, zin, zin, zin, conv_w], specs, **kw)


def _outproj_odd(c_mix, zin, d_norm_w, d_norm_b, sp_w, sp_b, **kw):
    bm = kw["bm"]
    cur = lambda sec: pl.BlockSpec((1, bm, D_WIDTH), lambda b, i: (b, i, sec))
    specs = [pl.BlockSpec((1, bm, C_WIDTH), lambda b, i: (b, i, 0)), cur(1), cur(2),
             pl.BlockSpec((1, D_WIDTH), lambda b, i: (0, 0)), pl.BlockSpec((1, D_WIDTH), lambda b, i: (0, 0)),
             pl.BlockSpec((D_GROUPS, D_CHUNK, D_CHUNK), lambda b, i: (0, 0, 0)),
             pl.BlockSpec((D_CHUNK, D_GROUPS), lambda b, i: (0, 0))]
    return _outproj("odd", [c_mix, zin, zin, d_norm_w.reshape(1, D_WIDTH), d_norm_b.reshape(1, D_WIDTH),
                            sp_w.astype(BF16), sp_b.T], specs, **kw)


def _dft_mats(n):
    ang = 2.0 * np.pi * np.outer(np.arange(n), np.arange(n)) / n
    return np.cos(ang), np.sin(ang)


def _fourier_cw_kernel(z_ref, nw_ref, cs_ref, kc_ref, ks_ref, p_ref, q_ref, ps_ref, qs_ref):
    tile = z_ref.shape[1]
    half = 256
    grows = tile // GRID_W
    for hs in range(tile // half):
        z = z_ref[0, hs * half:(hs + 1) * half, :].astype(F32)
        a_parts, b_parts = [], []
        for g in range(C_GROUPS):
            zg = z[:, g * C_GROUP:(g + 1) * C_GROUP]
            zg = zg * lax.rsqrt(jnp.mean(zg * zg, axis=-1, keepdims=True) + EPS) * nw_ref[:, g * C_GROUP:(g + 1) * C_GROUP]
            ab = jnp.dot(zg.astype(BF16), cs_ref[...], preferred_element_type=F32)
            a_parts.append(ab[:, :C_GROUP])
            b_parts.append(ab[:, C_GROUP:])
        ab = jnp.concatenate(a_parts + b_parts, axis=1).astype(BF16)
        m1 = jnp.dot(kc_ref[...], ab, preferred_element_type=F32)
        m2 = jnp.dot(ks_ref[...], ab, preferred_element_type=F32)
        pv = m1[:, :C_WIDTH] - m2[:, C_WIDTH:]
        qv = m2[:, :C_WIDTH] + m1[:, C_WIDTH:]
        for g in range(C_GROUPS):
            ps_ref[g, hs * half:(hs + 1) * half, :] = pv[:, g * C_GROUP:(g + 1) * C_GROUP]
            qs_ref[g, hs * half:(hs + 1) * half, :] = qv[:, g * C_GROUP:(g + 1) * C_GROUP]
    for k2 in range(GRID_W):
        for g in range(C_GROUPS):
            p_ref[0, k2, :, g * C_GROUP:(g + 1) * C_GROUP] = ps_ref[g, pl.ds(k2, grows, stride=GRID_W), :]
            q_ref[0, k2, :, g * C_GROUP:(g + 1) * C_GROUP] = qs_ref[g, pl.ds(k2, grows, stride=GRID_W), :]


def _fourier_r_kernel(p_ref, q_ref, c_ref, s_ref, o_ref, ys_ref):
    nj = p_ref.shape[1]
    rows = p_ref.shape[2]
    for j in range(nj):
        y = (jnp.dot(c_ref[...], p_ref[0, j].astype(BF16), preferred_element_type=F32)
             + jnp.dot(s_ref[...], q_ref[0, j].astype(BF16), preferred_element_type=F32))
        for g in range(C_GROUPS):
            ys_ref[g, j * rows:(j + 1) * rows, :] = y[:, g * C_GROUP:(g + 1) * C_GROUP]

    def put(k1, carry):
        for g in range(C_GROUPS):
            o_ref[0, k1, :, g * C_GROUP:(g + 1) * C_GROUP] = ys_ref[g, pl.ds(k1, nj, stride=rows), :]
        return carry

    lax.fori_loop(0, rows, put, 0)


def _fourier_mix(zin, c_norm_w):
    bn, t, _ = zin.shape
    rows = t // GRID_W
    tile = 512
    grows = tile // GRID_W
    c3, s3 = _dft_mats(C_GROUP)
    c2, s2 = _dft_mats(GRID_W)
    c1, s1 = _dft_mats(rows)
    scale = 1.0 / math.sqrt(rows * GRID_W * C_GROUP)
    cs3 = jnp.asarray(np.concatenate([c3, s3], axis=1), BF16)
    eye = np.eye(256 // GRID_W)
    kc2 = jnp.asarray(np.kron(eye, c2), BF16)
    ks2 = jnp.asarray(np.kron(eye, s2), BF16)
    c1s = jnp.asarray(c1 * scale, BF16)
    s1s = jnp.asarray(-s1 * scale, BF16)
    pq_shape = jax.ShapeDtypeStruct((bn, GRID_W, rows, C_WIDTH), F32)
    pq_spec = pl.BlockSpec((1, GRID_W, grows, C_WIDTH), lambda b, i: (b, 0, i, 0))
    p, q = pl.pallas_call(
        _fourier_cw_kernel,
        grid=(bn, t // tile),
        in_specs=[pl.BlockSpec((1, tile, C_WIDTH), lambda b, i: (b, i, 0)),
                  pl.BlockSpec((1, C_WIDTH), lambda b, i: (0, 0)),
                  pl.BlockSpec((C_GROUP, 2 * C_GROUP), lambda b, i: (0, 0)),
                  pl.BlockSpec((256, 256), lambda b, i: (0, 0)),
                  pl.BlockSpec((256, 256), lambda b, i: (0, 0))],
        out_specs=[pq_spec, pq_spec],
        out_shape=[pq_shape, pq_shape],
        scratch_shapes=[pltpu.VMEM((C_GROUPS, tile, C_GROUP), F32), pltpu.VMEM((C_GROUPS, tile, C_GROUP), F32)],
        compiler_params=_params("parallel", "parallel"),
        name="fourier_chan_col",
    )(zin, c_norm_w.reshape(1, C_WIDTH), cs3, kc2, ks2)
    nj = 8
    in_spec = pl.BlockSpec((1, nj, rows, C_WIDTH), lambda b, j: (b, j, 0, 0))
    mat_spec = pl.BlockSpec((rows, rows), lambda b, j: (0, 0))
    y = pl.pallas_call(
        _fourier_r_kernel,
        grid=(bn, GRID_W // nj),
        in_specs=[in_spec, in_spec, mat_spec, mat_spec],
        out_specs=pl.BlockSpec((1, rows, nj, C_WIDTH), lambda b, j: (b, 0, j, 0)),
        out_shape=jax.ShapeDtypeStruct((bn, rows, GRID_W, C_WIDTH), F32),
        scratch_shapes=[pltpu.VMEM((C_GROUPS, nj * rows, C_GROUP), F32)],
        compiler_params=_params("parallel", "parallel"),
        name="fourier_rows",
    )(p, q, c1s, s1s)
    return y.reshape(bn, t, C_WIDTH)


def _rank_kernel(ti_ref, rank_ref, cnt_ref, run_ref):
    @pl.when(pl.program_id(0) == 0)
    def _():
        run_ref[...] = jnp.zeros_like(run_ref)

    ti = ti_ref[...]
    tb = ti.shape[1]
    eio = lax.broadcasted_iota(jnp.int32, (N_EXPERTS, tb), 0)
    hot = [eio == ti[k:k + 1, :] for k in range(TOP_K)]
    occ = sum(h.astype(F32) for h in hot)
    si = lax.broadcasted_iota(jnp.int32, (tb, tb), 0)
    ti_ = lax.broadcasted_iota(jnp.int32, (tb, tb), 1)
    before = jnp.where(si < ti_, 1.0, 0.0).astype(BF16)
    seen = jnp.dot(occ.astype(BF16), before, preferred_element_type=F32) + run_ref[:, 0:1]
    rank_ref[...] = jnp.concatenate(
        [jnp.sum(jnp.where(h, seen, 0.0), axis=0, keepdims=True) for h in hot], axis=0).astype(jnp.int32)
    run_ref[...] = run_ref[...] + jnp.sum(occ, axis=1, keepdims=True)
    cnt_ref[...] = run_ref[...].astype(jnp.int32)


def _ranks(top_i):
    ntok = top_i.shape[1]
    tb = ROUTE_TILE if ntok % ROUTE_TILE == 0 else 256
    rank, cnt = pl.pallas_call(
        _rank_kernel,
        grid=(ntok // tb,),
        in_specs=[pl.BlockSpec((TOP_K, tb), lambda i: (0, i))],
        out_specs=[pl.BlockSpec((TOP_K, tb), lambda i: (0, i)),
                   pl.BlockSpec((N_EXPERTS, 128), lambda i: (0, 0))],
        out_shape=[jax.ShapeDtypeStruct((TOP_K, ntok), jnp.int32),
                   jax.ShapeDtypeStruct((N_EXPERTS, 128), jnp.int32)],
        scratch_shapes=[pltpu.VMEM((N_EXPERTS, 128), F32)],
        compiler_params=_params("arbitrary"),
        name="route_rank",
    )(top_i)
    return rank, cnt[:, 0]


def _w1_split_kernel(w_ref, g_ref, l_ref, s_ref):
    kdim, cols = w_ref.shape[2], w_ref.shape[3]
    half = cols // 2
    nk = kdim // LANES
    t = w_ref[0, 0].T
    for kc in range(nk):
        s_ref[kc] = t[:, kc * LANES:(kc + 1) * LANES]
    even = jnp.concatenate([s_ref[kc, pl.ds(0, half, stride=2), :] for kc in range(nk)], axis=1)
    odd = jnp.concatenate([s_ref[kc, pl.ds(1, half, stride=2), :] for kc in range(nk)], axis=1)
    g_ref[0] = even.T.astype(BF16)
    l_ref[0] = odd.T.astype(BF16)


def _w1_split(w1_all, layer):
    _, e, d, f2 = w1_all.shape
    cols = 1024
    out = jax.ShapeDtypeStruct((e, d, f2 // 2), BF16)
    out_spec = pl.BlockSpec((1, d, cols // 2), lambda i, c: (i, 0, c))
    return pl.pallas_call(
        _w1_split_kernel,
        grid=(e, f2 // cols),
        in_specs=[pl.BlockSpec((1, 1, d, cols), lambda i, c: (layer, i, 0, c))],
        out_specs=[out_spec, out_spec],
        out_shape=[out, out],
        scratch_shapes=[pltpu.VMEM((d // LANES, cols, LANES), F32)],
        compiler_params=_params("parallel", "parallel"),
        name="moe_w1_split",
    )(w1_all)


def _cast_kernel(w_ref, o_ref):
    o_ref[...] = w_ref[0].astype(o_ref.dtype)


def _w2_cast(w2_all, layer):
    _, e, f, d = w2_all.shape
    return pl.pallas_call(
        _cast_kernel,
        grid=(e,),
        in_specs=[pl.BlockSpec((1, 1, f, d), lambda i: (layer, i, 0, 0))],
        out_specs=pl.BlockSpec((1, f, d), lambda i: (i, 0, 0)),
        out_shape=jax.ShapeDtypeStruct((e, f, d), BF16),
        compiler_params=_params("parallel"),
        name="moe_w2_cast",
    )(w2_all)


def _expert_kernel(be_ref, used_ref, x_ref, w1g_ref, w1l_ref, w2_ref, b1g_ref, b1l_ref, b2_ref, y_ref):
    del be_ref
    rows = x_ref.shape[0] // ROW_TILES
    in_use = pl.program_id(0) < used_ref[0]

    @pl.when(in_use)
    def _():
        x = _load_row_tiles(x_ref, rows).astype(BF16)
        glu = jnp.minimum(jnp.dot(x, w1g_ref[0], preferred_element_type=F32) + b1g_ref[0], SWIGLU_LIMIT)
        lin = jnp.clip(jnp.dot(x, w1l_ref[0], preferred_element_type=F32) + b1l_ref[0],
                       -SWIGLU_LIMIT, SWIGLU_LIMIT)
        act = glu * _sigmoid(SWIGLU_ALPHA * glu) * (lin + 1.0)
        _store_row_tiles(y_ref, jnp.dot(act.astype(BF16), w2_ref[0], preferred_element_type=F32) + b2_ref[0])

    @pl.when(jnp.logical_not(in_use))
    def _():
        y_ref[...] = jnp.zeros_like(y_ref)


def _experts(x_sorted, block_expert, blocks_used, w1g, w1l, w2, b1g, b1l, b2):
    d, f = w1g.shape[1], w1g.shape[2]
    nb = block_expert.shape[0]
    buf_rows = MOE_ROWS * ROW_TILES
    wspec = lambda k, n: pl.BlockSpec((1, k, n), lambda j, be, nu: (be[j], 0, 0))
    row_spec = pl.BlockSpec((buf_rows, LANES), lambda j, be, nu: (j, 0))
    grid_spec = pltpu.PrefetchScalarGridSpec(
        num_scalar_prefetch=2,
        grid=(nb,),
        in_specs=[row_spec, wspec(d, f), wspec(d, f), wspec(f, d), wspec(1, f), wspec(1, f), wspec(1, d)],
        out_specs=row_spec)
    return pl.pallas_call(
        _expert_kernel,
        grid_spec=grid_spec,
        out_shape=jax.ShapeDtypeStruct(x_sorted.shape, F32),
        compiler_params=_params("parallel"),
        name="moe_experts",
    )(block_expert, blocks_used, x_sorted, w1g, w1l, w2, b1g, b1l, b2)


def _dispatch_kernel(cnt_ref, start_ref, dst_ref, h_ref, xs_hbm, rowbuf, zero_ref, sem, zsem, *, nt, nslots):
    i = pl.program_id(0)
    tb = h_ref.shape[0]
    slot = i % 2

    def landed(buf):
        for _ in range(TOP_K):
            pltpu.make_async_copy(rowbuf.at[buf, pl.ds(0, tb * ROW_TILES), :],
                                  xs_hbm.at[pl.ds(0, tb * ROW_TILES), :], sem.at[buf]).wait()

    @pl.when(i >= 2)
    def _():
        landed(slot)

    _store_row_tiles(rowbuf.at[slot], h_ref[...].astype(F32), pitch=ROW_PITCH)
    for r in range(tb):
        for k in range(TOP_K):
            dst = pl.multiple_of(dst_ref[0, k, r] * ROW_TILES, ROW_TILES)
            pltpu.make_async_copy(rowbuf.at[slot, pl.ds(r * ROW_PITCH, ROW_TILES), :],
                                  xs_hbm.at[pl.ds(dst, ROW_TILES), :], sem.at[slot]).start(priority=k % 2)

    @pl.when(i == nt - 1)
    def _():
        landed(slot)
        if nt > 1:
            landed(1 - slot)
        zero_ref[...] = jnp.zeros_like(zero_ref)

        zrows = zero_ref.shape[0] // ROW_TILES

        def zero_rows(pos, n):
            cp = pltpu.make_async_copy(
                zero_ref.at[pl.ds(0, n * ROW_TILES), :],
                xs_hbm.at[pl.ds(pl.multiple_of(pos * ROW_TILES, ROW_TILES), n * ROW_TILES), :], zsem)
            cp.start()
            cp.wait()

        end = 0
        for e in range(N_EXPERTS):
            pos = start_ref[e] + cnt_ref[e]
            end = start_ref[e] + (cnt_ref[e] + MOE_ROWS - 1) // MOE_ROWS * MOE_ROWS
            gap = end - pos
            bit = zrows
            while bit >= 1:
                take = (gap & bit) != 0
                pl.when(take)(functools.partial(zero_rows, pos, bit))
                pos = pos + jnp.where(take, bit, 0)
                bit //= 2

        def tail(i, carry):
            zero_rows(end + i * zrows, zrows)
            return carry
        lax.fori_loop(0, (nslots - end) // zrows, tail, 0)


def _dispatch(h2, dest3, counts, pad_starts, nslots):
    ntok, d = h2.shape
    tb = COMBINE_TILE
    nt = ntok // tb
    assert MOE_ROWS & (MOE_ROWS - 1) == 0, "the padding fill decomposes gap lengths into powers of two"
    grid_spec = pltpu.PrefetchScalarGridSpec(
        num_scalar_prefetch=2,
        grid=(nt,),
        in_specs=[pl.BlockSpec((1, TOP_K, tb), lambda i, c, s: (i, 0, 0), memory_space=pltpu.SMEM),
                  pl.BlockSpec((tb, d), lambda i, c, s: (i, 0))],
        out_specs=pl.BlockSpec(memory_space=pl.ANY),
        scratch_shapes=[pltpu.VMEM((2, tb * ROW_PITCH, LANES), F32),
                        pltpu.VMEM((MOE_ROWS // 2 * ROW_TILES, LANES), F32),
                        pltpu.SemaphoreType.DMA((2,)), pltpu.SemaphoreType.DMA(())])
    return pl.pallas_call(
        functools.partial(_dispatch_kernel, nt=nt, nslots=nslots),
        grid_spec=grid_spec,
        out_shape=jax.ShapeDtypeStruct((nslots * ROW_TILES, LANES), F32),
        compiler_params=_params("arbitrary"),
        name="moe_dispatch",
    )(counts, pad_starts, dest3, h2)


def _combine_kernel(dst_ref, dstn_ref, y_hbm, x1_ref, gate_ref, g2_ref, lnw_ref, lnb_ref, o_ref, ybuf, sem, *, nt):
    i = pl.program_id(0)
    tb = x1_ref.shape[0]
    slot = i % 2

    def row_copy(src, buf, k, r):
        rows = y_hbm.at[pl.ds(pl.multiple_of(src * ROW_TILES, ROW_TILES), ROW_TILES), :]
        return pltpu.make_async_copy(rows, ybuf.at[buf, k, pl.ds(r * ROW_PITCH, ROW_TILES), :], sem.at[buf])

    def gather(idx_ref, buf):
        for r in range(tb):
            for k in range(TOP_K):
                row_copy(idx_ref[0, k, r], buf, k, r).start(priority=k % 2)

    @pl.when(i == 0)
    def _():
        gather(dst_ref, 0)

    @pl.when(i + 1 < nt)
    def _():
        gather(dstn_ref, 1 - slot)

    for k in range(TOP_K):
        pltpu.make_async_copy(y_hbm.at[pl.ds(0, tb * ROW_TILES), :],
                              ybuf.at[slot, k, pl.ds(0, tb * ROW_TILES), :], sem.at[slot]).wait()

    y = sum(gate_ref[:, k:k + 1] * _load_row_tiles(ybuf.at[slot, k], tb, pitch=ROW_PITCH) for k in range(TOP_K))
    r = DN_ALPHA * x1_ref[...] + g2_ref[0] * y
    mu = jnp.mean(r, axis=-1, keepdims=True)
    rc = r - mu
    var = jnp.mean(rc * rc, axis=-1, keepdims=True)
    o_ref[...] = rc * lax.rsqrt(var + EPS) * lnw_ref[...] + lnb_ref[...]


def _combine(x1, y_sorted, dest3, gates_t, tok_off, g2, mod_row, lnw, lnb):
    bn, t, d = x1.shape
    n = bn * t
    tb = COMBINE_TILE
    nt = n // tb
    off = tok_off // tb
    per_b = t // tb
    out = pl.pallas_call(
        functools.partial(_combine_kernel, nt=nt),
        grid=(nt,),
        in_specs=[pl.BlockSpec((1, TOP_K, tb), lambda i: (off + i, 0, 0), memory_space=pltpu.SMEM),
                  pl.BlockSpec((1, TOP_K, tb), lambda i: (off + jnp.minimum(i + 1, nt - 1), 0, 0),
                               memory_space=pltpu.SMEM),
                  pl.BlockSpec(memory_space=pl.ANY),
                  pl.BlockSpec((tb, d), lambda i: (i, 0)),
                  pl.BlockSpec((tb, TOP_K), lambda i: (off + i, 0)),
                  pl.BlockSpec((1, 1, d), lambda i: (mod_row(i // per_b), 0, 0)),
                  pl.BlockSpec((1, d), lambda i: (0, 0)),
                  pl.BlockSpec((1, d), lambda i: (0, 0))],
        out_specs=pl.BlockSpec((tb, d), lambda i: (i, 0)),
        out_shape=jax.ShapeDtypeStruct((n, d), F32),
        scratch_shapes=[pltpu.VMEM((2, TOP_K, tb * ROW_PITCH, LANES), F32), pltpu.SemaphoreType.DMA((2,))],
        compiler_params=_params("arbitrary"),
        name="moe_combine",
    )(dest3, dest3, y_sorted, x1.reshape(n, d), gates_t, g2, lnw.reshape(1, d), lnb.reshape(1, d))
    return out.reshape(bn, t, d)


def _dest_kernel(ti_ref, rank_ref, start_ref, dest_ref):
    ti = ti_ref[...]
    tb = ti.shape[1]
    eio = lax.broadcasted_iota(jnp.int32, (N_EXPERTS, tb), 0)
    start = start_ref[:, 0:1].astype(F32)
    base = jnp.concatenate(
        [jnp.sum(jnp.where(eio == ti[k:k + 1, :], start, 0.0), axis=0, keepdims=True) for k in range(TOP_K)], axis=0)
    dest = base.astype(jnp.int32) + rank_ref[...]
    for c in range(tb // COMBINE_TILE):
        dest_ref[c] = dest[:, c * COMBINE_TILE:(c + 1) * COMBINE_TILE]


def _moe_layout(top_i):
    ntok = top_i.shape[1]
    m = ntok * TOP_K
    rank, counts = _ranks(top_i)
    padded = (counts + MOE_ROWS - 1) // MOE_ROWS * MOE_ROWS
    pad_ends = jnp.cumsum(padded)
    pad_starts = pad_ends - padded
    tb = ROUTE_TILE if ntok % ROUTE_TILE == 0 else 256
    ntiles = ntok // COMBINE_TILE
    dest = pl.pallas_call(
        _dest_kernel,
        grid=(ntok // tb,),
        in_specs=[pl.BlockSpec((TOP_K, tb), lambda i: (0, i)),
                  pl.BlockSpec((TOP_K, tb), lambda i: (0, i)),
                  pl.BlockSpec((N_EXPERTS, LANES), lambda i: (0, 0))],
        out_specs=pl.BlockSpec((tb // COMBINE_TILE, TOP_K, COMBINE_TILE), lambda i: (i, 0, 0)),
        out_shape=jax.ShapeDtypeStruct((ntiles, TOP_K, COMBINE_TILE), jnp.int32),
        compiler_params=_params("parallel"),
        name="route_dest",
    )(top_i, rank, jnp.broadcast_to(pad_starts[:, None], (N_EXPERTS, LANES)))
    nb = (m + N_EXPERTS * (MOE_ROWS - 1) + MOE_ROWS - 1) // MOE_ROWS
    block_starts = jnp.arange(nb, dtype=jnp.int32) * MOE_ROWS
    block_expert = jnp.minimum(jnp.sum(block_starts[:, None] >= pad_ends[None, :], axis=1),
                               N_EXPERTS - 1).astype(jnp.int32)
    blocks_used = (pad_ends[-1:] // MOE_ROWS).astype(jnp.int32)
    return dest, counts, pad_starts.astype(jnp.int32), block_expert, blocks_used, nb * MOE_ROWS


def kernel(x, c, ctx, c_ctx, lower_bounds_fwd, lower_bounds_bwd, ada_w, ada_b, w_in_even, a_norm_w, conv_w,
           w_in_odd, c_norm_w, d_norm_w, d_norm_b, spatial_w, spatial_b, w_out, ln_mix_w, ln_mix_b,
           ln_ffn_w, ln_ffn_b, router_w, router_b, moe_w1, moe_b1, moe_w2, moe_b2):
    bn, t, d = x.shape
    tc = ctx.shape[1]
    n_lat = bn * t
    n_ctx = bn * tc
    ctx_row = bn
    lat_row = lambda b: b
    ctx_mod_row = lambda b: ctx_row

    lb_f_all = jnp.cumsum(jax.nn.softmax(lower_bounds_fwd.astype(F32), axis=0), axis=0)
    lb_b_all = jnp.cumsum(jax.nn.softmax(lower_bounds_bwd.astype(F32), axis=0), axis=0)

    cond = jnp.zeros((MOD_ROWS, d), F32).at[:bn].set(c).at[ctx_row].set(c_ctx)
    mods = _ada(cond, ada_w, ada_b).reshape(DEPTH, MOD_ROWS, 6, 1, d)

    h_ctx = ctx
    for l in range(DEPTH):
        last = l == DEPTH - 1
        sh1, sc1, g1, sh2, sc2, g2 = (mods[l, :, j] for j in range(6))
        wo = w_out[l].astype(BF16)
        rw_t = router_w[l].T.astype(BF16)
        ntok = n_lat if last else n_lat + n_ctx
        tok_state = (jnp.zeros((ntok, d), BF16), jnp.zeros((TOP_K, ntok), jnp.int32),
                     jnp.zeros((TOP_K, ntok), F32))
        post = dict(wo=wo, lnw=ln_mix_w[l], lnb=ln_mix_b[l], rw_t=rw_t, rb=router_b[l])
        bm_ctx = min(ROW_TILE, tc)
        if l % 2 == 0:
            e = l // 2
            w_in = w_in_even[e].astype(BF16)
            zin_c = _inproj(h_ctx, sc1, sh1, w_in, ctx_mod_row, bm=bm_ctx)
            zin = _inproj(x, sc1, sh1, w_in, lat_row, bm=ROW_TILE)
            zeros = jnp.zeros((bn, A_HEADS, A_DV, A_DK), F32)
            oc_f, sc_f = _scan(zin_c, lb_f_all[l], zeros, reverse=False)
            a_ctx, sc_b = _scan(zin_c, lb_b_all[l], zeros, reverse=True, o_prev=oc_f, norm_w=a_norm_w[e])
            o_f, _ = _scan(zin, lb_f_all[l], sc_f, reverse=False)
            a_lat, _ = _scan(zin, lb_b_all[l], sc_b, reverse=True, o_prev=o_f, norm_w=a_norm_w[e])
            x1, *tok_state = _outproj_even(a_lat, zin, conv_w[e], x=x, mods=(g1, sc2, sh2), mod_row=lat_row,
                                           tok_state=tok_state, tok_off=0, bm=ROW_TILE, **post)
            if not last:
                hc1, *tok_state = _outproj_even(a_ctx, zin_c, conv_w[e], x=h_ctx, mods=(g1, sc2, sh2),
                                                mod_row=ctx_mod_row, tok_state=tok_state, tok_off=n_lat,
                                                bm=bm_ctx, **post)
        else:
            o = l // 2
            w_in = w_in_odd[o].astype(BF16)
            odd = dict(d_norm_w=d_norm_w[o], d_norm_b=d_norm_b[o], sp_w=spatial_w[o], sp_b=spatial_b[o])
            zin = _inproj(x, sc1, sh1, w_in, lat_row, bm=ROW_TILE)
            c_mix = _fourier_mix(zin, c_norm_w[o])
            x1, *tok_state = _outproj_odd(c_mix, zin, x=x, mods=(g1, sc2, sh2), mod_row=lat_row,
                                          tok_state=tok_state, tok_off=0, bm=ROW_TILE, **odd, **post)
            if not last:
                raise NotImplementedError("an odd layer that is not the last needs the context Fourier mixer")
        h2_all, top_i, top_g = tok_state
        dest, counts, pad_starts, block_expert, blocks_used, nslots = _moe_layout(top_i)
        x_sorted = _dispatch(h2_all, dest, counts, pad_starts, nslots)
        w1g, w1l = _w1_split(moe_w1, l)
        y_sorted = _experts(
            x_sorted, block_expert, blocks_used, w1g, w1l, _w2_cast(moe_w2, l),
            moe_b1[l][:, None, 0::2], moe_b1[l][:, None, 1::2], moe_b2[l][:, None, :])
        gates_t = top_g.T
        x = _combine(x1, y_sorted, dest, gates_t, 0, g2, lat_row, ln_ffn_w[l], ln_ffn_b[l])
        if not last:
            h_ctx = _combine(hc1, y_sorted, dest, gates_t, n_lat, g2, ctx_mod_row, ln_ffn_w[l], ln_ffn_b[l])
    return x
```

```python
import functools
import math

import numpy as np
import jax
import jax.numpy as jnp
from jax import lax
from jax.experimental import pallas as pl
from jax.experimental.pallas import tpu as pltpu

F32 = jnp.float32
BF16 = jnp.bfloat16

D_MODEL = 1024
DEPTH = 2
GRID_W = 64
A_HEADS = 4
A_DK = 128
A_DV = 128
A_KW = A_HEADS * A_DK
A_WIDTH = A_HEADS * A_DV
B_WIDTH = D_MODEL - A_WIDTH
CONV_W = 3
C_GROUPS = 4
C_GROUP = 128
C_WIDTH = C_GROUPS * C_GROUP
D_WIDTH = D_MODEL - C_WIDTH
D_GROUPS = 4
D_GDIM = D_WIDTH // D_GROUPS
D_CHUNK = 128
SCAN_CHUNK = 32
EVEN_IN = 3 * A_KW + 2 * A_WIDTH + 3 * B_WIDTH
ODD_IN = C_WIDTH + 2 * D_WIDTH
N_EXPERTS = 32
TOP_K = 4
D_EXPERT = 1024
SWIGLU_ALPHA = 1.702
SWIGLU_LIMIT = 7.0
DN_ALPHA = (2 * DEPTH) ** 0.25
EPS = 1e-5

MOD_ROWS = 8
SCAN_ROWS = 256
SCAN_HEADS = 4
SCAN_SPAN = 256
ROW_TILE = 512
SUB_TILE = 256
MOE_ROWS = 512
ROUTE_TILE = 512
COMBINE_TILE = 128
HALO = 16
LANES = 128
ROW_TILES = D_MODEL // LANES
ROW_PITCH = ROW_TILES + 1

_NT = (((1,), (1,)), ((), ()))
_TN = (((0,), (0,)), ((), ()))


def _sigmoid(a):
    return 1.0 / (1.0 + jnp.exp(-a))


def _silu(a):
    return a * _sigmoid(a)


def _gelu(a):
    return 0.5 * a * (1.0 + lax.erf(a * (1.0 / math.sqrt(2.0))))


def _store_row_tiles(ref, val, pitch=ROW_TILES):
    rows = val.shape[0]
    for c in range(ROW_TILES):
        ref[pl.ds(c, rows, stride=pitch), :] = val[:, c * LANES:(c + 1) * LANES]


def _load_row_tiles(ref, rows, pitch=ROW_TILES):
    return jnp.concatenate([ref[pl.ds(c, rows, stride=pitch), :] for c in range(ROW_TILES)], axis=1)


def _params(*sem):
    return pltpu.CompilerParams(dimension_semantics=sem, vmem_limit_bytes=56 * 1024 * 1024)


def _ada_kernel(c_ref, w_ref, b_ref, o_ref):
    s = _silu(c_ref[...])
    o_ref[0] = jnp.dot(s, w_ref[0], preferred_element_type=F32, precision=lax.Precision.HIGHEST) + b_ref[0]


def _ada(cond, ada_w, ada_b):
    d = cond.shape[1]
    n6 = ada_w.shape[2]
    tn = 1536
    return pl.pallas_call(
        _ada_kernel,
        grid=(DEPTH, n6 // tn),
        in_specs=[pl.BlockSpec((MOD_ROWS, d), lambda l, n: (0, 0)),
                  pl.BlockSpec((1, d, tn), lambda l, n: (l, 0, n)),
                  pl.BlockSpec((1, 1, tn), lambda l, n: (l, 0, n))],
        out_specs=pl.BlockSpec((1, MOD_ROWS, tn), lambda l, n: (l, 0, n)),
        out_shape=jax.ShapeDtypeStruct((DEPTH, MOD_ROWS, n6), F32),
        compiler_params=_params("parallel", "parallel"),
        name="ada_mod",
    )(cond, ada_w, ada_b.reshape(DEPTH, 1, n6))


def _inproj_kernel(x_ref, sc_ref, sh_ref, w_ref, o_ref, *, tn):
    h = (x_ref[0] * (1.0 + sc_ref[0]) + sh_ref[0]).astype(BF16)
    for n in range(o_ref.shape[2] // tn):
        o_ref[0, :, n * tn:(n + 1) * tn] = jnp.dot(
            h, w_ref[:, n * tn:(n + 1) * tn], preferred_element_type=F32).astype(o_ref.dtype)


def _inproj(x, sc, sh, w, mod_row, bm):
    bn, t, d = x.shape
    nout = w.shape[1]
    mod_spec = pl.BlockSpec((1, 1, d), lambda b, i: (mod_row(b), 0, 0))
    return pl.pallas_call(
        functools.partial(_inproj_kernel, tn=512),
        grid=(bn, t // bm),
        in_specs=[pl.BlockSpec((1, bm, d), lambda b, i: (b, i, 0)), mod_spec, mod_spec,
                  pl.BlockSpec((d, nout), lambda b, i: (0, 0))],
        out_specs=pl.BlockSpec((1, bm, nout), lambda b, i: (b, i, 0)),
        out_shape=jax.ShapeDtypeStruct((bn, t, nout), BF16),
        compiler_params=_params("parallel", "parallel"),
        name="in_proj",
    )(x, sc, sh, w)


def _scan_kernel(*refs, reverse, finalize):
    if finalize:
        (q_ref, z_ref, v_ref, lb_ref, s0_ref, g_ref, op_ref, nw_ref, o_ref, sfin_ref, st_ref, oacc_ref) = refs
    else:
        (q_ref, z_ref, v_ref, lb_ref, s0_ref, o_ref, sfin_ref, st_ref, oacc_ref) = refs
    rows = q_ref.shape[1]
    heads = q_ref.shape[2] // A_DK
    nchunk = rows // SCAN_CHUNK

    @pl.when(pl.program_id(2) == 0)
    def _():
        st_ref[...] = s0_ref[0]

    q = _silu(q_ref[0].astype(F32))
    lb = lb_ref[...]
    f = lb + (1.0 - lb) * _sigmoid(z_ref[0].astype(F32))
    logf = jnp.log(f)
    k = 1.0 - f
    v = v_ref[0]

    ri = lax.broadcasted_iota(jnp.int32, (rows, rows), 0)
    ci = lax.broadcasted_iota(jnp.int32, (rows, rows), 1)
    same = (ri // SCAN_CHUNK) == (ci // SCAN_CHUNK)
    causal = same & ((ci >= ri) if reverse else (ci <= ri))
    tri = jnp.where(causal, 1.0, 0.0).astype(BF16)
    blk = jnp.where(same, 1.0, 0.0).astype(BF16)
    hi = logf.astype(BF16)
    lo = (logf - hi.astype(F32)).astype(BF16)
    b = jnp.dot(tri, hi, preferred_element_type=F32) + jnp.dot(tri, lo, preferred_element_type=F32)
    b_tot = jnp.dot(blk, hi, preferred_element_type=F32) + jnp.dot(blk, lo, preferred_element_type=F32)

    q_dec = (q * jnp.exp(b)).astype(BF16)
    k_inv = (k * jnp.exp(-b)).astype(BF16)
    k_end = (k * jnp.exp(b_tot - b)).astype(BF16)
    dec = jnp.exp(b_tot)

    span = min(SCAN_SPAN, rows)
    per = span // SCAN_CHUNK
    wide = per * A_DK
    spread = (lax.broadcasted_iota(jnp.int32, (span, wide), 0) // SCAN_CHUNK
              == lax.broadcasted_iota(jnp.int32, (span, wide), 1) // A_DK)
    causal_span = causal[0:span, 0:span]
    spans = range(rows // span - 1, -1, -1) if reverse else range(rows // span)
    order = range(per - 1, -1, -1) if reverse else range(per)
    for h in range(heads):
        hk = slice(h * A_DK, (h + 1) * A_DK)
        hv = slice(h * A_DV, (h + 1) * A_DV)
        st = st_ref[h]
        for sp in spans:
            rs = slice(sp * span, (sp + 1) * span)
            qd, vh = q_dec[rs, hk], v[rs, hv]
            scores = lax.dot_general(qd, k_inv[rs, hk], _NT, preferred_element_type=F32)
            scores = jnp.where(causal_span, scores, 0.0).astype(BF16)
            o_intra = jnp.dot(scores, vh, preferred_element_type=F32)
            ke_wide = jnp.where(spread, jnp.concatenate([k_end[rs, hk]] * per, axis=1), 0.0).astype(BF16)
            upd = lax.dot_general(vh, ke_wide, _TN, preferred_element_type=F32)
            seen = [None] * per
            for c in order:
                seen[c] = st.astype(BF16)
                r0 = sp * span + c * SCAN_CHUNK
                st = st * dec[r0:r0 + 1, hk] + upd[:, c * A_DK:(c + 1) * A_DK]
            qd_wide = jnp.where(spread, jnp.concatenate([qd] * per, axis=1), 0.0).astype(BF16)
            o_inter = lax.dot_general(qd_wide, jnp.concatenate(seen, axis=1), _NT, preferred_element_type=F32)
            oacc_ref[rs, hv] = o_intra + o_inter
        st_ref[h] = st
        sfin_ref[0, h] = st

    o = oacc_ref[...]
    if finalize:
        o = o + op_ref[0]
        gate = nw_ref[...] * _silu(g_ref[0].astype(F32))
        for h in range(heads):
            hv = slice(h * A_DV, (h + 1) * A_DV)
            oh = o[:, hv]
            oh = oh * lax.rsqrt(jnp.mean(oh * oh, axis=-1, keepdims=True) + EPS)
            o_ref[0, :, hv] = (oh * gate[:, hv]).astype(o_ref.dtype)
    else:
        o_ref[0] = o.astype(o_ref.dtype)


def _scan(zin, lb, s0, *, reverse, o_prev=None, norm_w=None):
    bn, t, _ = zin.shape
    finalize = o_prev is not None
    rows = min(SCAN_ROWS, t)
    nb = t // rows
    pos = (lambda n: nb - 1 - n) if reverse else (lambda n: n)
    hp = SCAN_HEADS
    groups = A_HEADS // hp
    hw = hp * A_DK

    def col(section):
        return pl.BlockSpec((1, rows, hw), lambda b, h, n: (b, pos(n), section * groups + h))

    state_spec = pl.BlockSpec((1, hp, A_DV, A_DK), lambda b, h, n: (b, h, 0, 0))
    in_specs = [col(0), col(2 if reverse else 1), col(3),
                pl.BlockSpec((1, hw), lambda b, h, n: (0, h)), state_spec]
    args = [zin, zin, zin, lb.reshape(1, A_KW), s0]
    if finalize:
        in_specs += [col(4),
                     pl.BlockSpec((1, rows, hw), lambda b, h, n: (b, pos(n), h)),
                     pl.BlockSpec((1, hw), lambda b, h, n: (0, h))]
        args += [zin, o_prev, norm_w.reshape(1, A_WIDTH)]
    return pl.pallas_call(
        functools.partial(_scan_kernel, reverse=reverse, finalize=finalize),
        grid=(bn, groups, nb),
        in_specs=in_specs,
        out_specs=[pl.BlockSpec((1, rows, hw), lambda b, h, n: (b, pos(n), h)), state_spec],
        out_shape=[jax.ShapeDtypeStruct((bn, t, A_WIDTH), BF16 if finalize else F32),
                   jax.ShapeDtypeStruct((bn, A_HEADS, A_DV, A_DK), F32)],
        scratch_shapes=[pltpu.VMEM((hp, A_DV, A_DK), F32), pltpu.VMEM((rows, hw), F32)],
        compiler_params=_params("parallel", "parallel", "arbitrary"),
        name="hgrn2_scan_bwd" if reverse else "hgrn2_scan_fwd",
    )(*args)


def _sub_tiles(bm):
    sub = min(bm, SUB_TILE)
    return [slice(s * sub, (s + 1) * sub) for s in range(bm // sub)]


def _post_mix(acc, rs, x_ref, g1_ref, lnw_ref, lnb_ref, sc2_ref, sh2_ref, rw_ref, rb_ref,
              x1_ref, h2_ref, ti_ref, tg_ref):
    r = DN_ALPHA * x_ref[0, rs, :] + g1_ref[0] * acc
    mu = jnp.mean(r, axis=-1, keepdims=True)
    rc = r - mu
    var = jnp.mean(rc * rc, axis=-1, keepdims=True)
    x1 = rc * lax.rsqrt(var + EPS) * lnw_ref[...] + lnb_ref[...]
    x1_ref[0, rs, :] = x1
    h2 = x1 * (1.0 + sc2_ref[0]) + sh2_ref[0]
    h2_ref[rs, :] = h2.astype(h2_ref.dtype)
    logits = lax.dot_general(rw_ref[...], h2.astype(BF16), _NT, preferred_element_type=F32) + rb_ref[...]
    iota = lax.broadcasted_iota(jnp.int32, logits.shape, 0)
    vals, idxs = [], []
    cur = logits
    for _ in range(TOP_K):
        m = jnp.max(cur, axis=0, keepdims=True)
        ik = jnp.min(jnp.where(cur == m, iota, N_EXPERTS), axis=0, keepdims=True)
        vals.append(m)
        idxs.append(ik)
        cur = jnp.where(iota == ik, -jnp.inf, cur)
    tv = jnp.concatenate(vals, axis=0)
    e = jnp.exp(tv - tv[0:1])
    tg_ref[:, rs] = e / jnp.sum(e, axis=0, keepdims=True)
    ti_ref[:, rs] = jnp.concatenate(idxs, axis=0)


def _outproj_even_kernel(a_ref, u_ref, gb_ref, gc_ref, up_ref, gcp_ref, un_ref, gcn_ref, cw_ref, wo_ref,
                         x_ref, g1_ref, lnw_ref, lnb_ref, sc2_ref, sh2_ref, rw_ref, rb_ref, alias_ref,
                         x1_ref, h2_ref, ti_ref, tg_ref):
    del alias_ref
    i = pl.program_id(1)
    bm = u_ref.shape[1]
    z = gc_ref[0].astype(F32) * u_ref[0].astype(F32)
    z_before = gcp_ref[0, HALO - 1:HALO, :].astype(F32) * up_ref[0, HALO - 1:HALO, :].astype(F32)
    z_after = gcn_ref[0, 0:1, :].astype(F32) * un_ref[0, 0:1, :].astype(F32)
    z_before = jnp.where(i == 0, 0.0, z_before)
    z_after = jnp.where(i == pl.num_programs(1) - 1, 0.0, z_after)
    row = lax.broadcasted_iota(jnp.int32, z.shape, 0)
    z_prev = jnp.where(row == 0, z_before, pltpu.roll(z, 1, axis=0))
    z_next = jnp.where(row == bm - 1, z_after, pltpu.roll(z, bm - 1, axis=0))
    zc = cw_ref[0:1, :] * z_prev + cw_ref[1:2, :] * z + cw_ref[2:3, :] * z_next
    b_mix = (gb_ref[0].astype(F32) * zc).astype(BF16)
    for rs in _sub_tiles(bm):
        acc = (jnp.dot(a_ref[0, rs, :], wo_ref[0:A_WIDTH, :], preferred_element_type=F32)
               + jnp.dot(b_mix[rs], wo_ref[A_WIDTH:, :], preferred_element_type=F32))
        _post_mix(acc, rs, x_ref, g1_ref, lnw_ref, lnb_ref, sc2_ref, sh2_ref, rw_ref, rb_ref,
                  x1_ref, h2_ref, ti_ref, tg_ref)


def _outproj_odd_kernel(c_ref, u_ref, v_ref, dw_ref, db_ref, spw_ref, spb_ref, wo_ref,
                        x_ref, g1_ref, lnw_ref, lnb_ref, sc2_ref, sh2_ref, rw_ref, rb_ref, alias_ref,
                        x1_ref, h2_ref, ti_ref, tg_ref):
    del alias_ref
    bm = u_ref.shape[1]
    u = _gelu(u_ref[0].astype(F32))
    v = _gelu(v_ref[0].astype(F32))
    cols = []
    for g in range(D_GROUPS):
        lo_c, hi_c = g * D_GDIM, (g + 1) * D_GDIM
        vg = v[:, lo_c:hi_c]
        mu = jnp.mean(vg, axis=-1, keepdims=True)
        vc = vg - mu
        var = jnp.mean(vc * vc, axis=-1, keepdims=True)
        vn = (vc * lax.rsqrt(var + EPS) * dw_ref[:, lo_c:hi_c] + db_ref[:, lo_c:hi_c]).astype(BF16)
        parts = []
        for ch in range(bm // D_CHUNK):
            sv = jnp.dot(spw_ref[g], vn[ch * D_CHUNK:(ch + 1) * D_CHUNK], preferred_element_type=F32)
            parts.append(sv + spb_ref[:, g:g + 1])
        cols.append(u[:, lo_c:hi_c] * jnp.concatenate(parts, axis=0))
    d_mix = jnp.concatenate(cols, axis=1).astype(BF16)
    for rs in _sub_tiles(bm):
        acc = (jnp.dot(c_ref[0, rs, :].astype(BF16), wo_ref[0:C_WIDTH, :], preferred_element_type=F32)
               + jnp.dot(d_mix[rs], wo_ref[C_WIDTH:, :], preferred_element_type=F32))
        _post_mix(acc, rs, x_ref, g1_ref, lnw_ref, lnb_ref, sc2_ref, sh2_ref, rw_ref, rb_ref,
                  x1_ref, h2_ref, ti_ref, tg_ref)


def _outproj(kind, mixer_args, mixer_specs, wo, x, mods, mod_row, lnw, lnb, rw_t, rb, tok_state, tok_off, bm):
    bn, t, d = x.shape
    nt = t // bm
    g1, sc2, sh2 = mods
    h2_all, ti_all, tg_all = tok_state
    ntok = h2_all.shape[0]
    off = tok_off // bm
    mod_spec = pl.BlockSpec((1, 1, d), lambda b, i: (mod_row(b), 0, 0))
    vec_spec = pl.BlockSpec((1, d), lambda b, i: (0, 0))
    in_specs = list(mixer_specs) + [
        pl.BlockSpec((d, d), lambda b, i: (0, 0)),
        pl.BlockSpec((1, bm, d), lambda b, i: (b, i, 0)),
        mod_spec, vec_spec, vec_spec, mod_spec, mod_spec,
        pl.BlockSpec((N_EXPERTS, d), lambda b, i: (0, 0)),
        pl.BlockSpec((N_EXPERTS, 1), lambda b, i: (0, 0)),
        pl.BlockSpec(memory_space=pl.ANY), pl.BlockSpec(memory_space=pl.ANY), pl.BlockSpec(memory_space=pl.ANY)]
    n_in = len(in_specs)
    kernel = _outproj_even_kernel if kind == "even" else _outproj_odd_kernel

    def body(*refs):
        ins, outs = refs[:n_in], refs[n_in:]
        kernel(*ins[:n_in - 3], ins[n_in - 3:], *outs)

    return pl.pallas_call(
        body,
        grid=(bn, nt),
        in_specs=in_specs,
        out_specs=[pl.BlockSpec((1, bm, d), lambda b, i: (b, i, 0)),
                   pl.BlockSpec((bm, d), lambda b, i: (off + b * nt + i, 0)),
                   pl.BlockSpec((TOP_K, bm), lambda b, i: (0, off + b * nt + i)),
                   pl.BlockSpec((TOP_K, bm), lambda b, i: (0, off + b * nt + i))],
        out_shape=[jax.ShapeDtypeStruct((bn, t, d), F32),
                   jax.ShapeDtypeStruct((ntok, d), BF16),
                   jax.ShapeDtypeStruct((TOP_K, ntok), jnp.int32),
                   jax.ShapeDtypeStruct((TOP_K, ntok), F32)],
        input_output_aliases={n_in - 3: 1, n_in - 2: 2, n_in - 1: 3},
        compiler_params=_params("parallel", "parallel"),
        name="out_proj_" + kind,
    )(*mixer_args, wo, x, g1, lnw.reshape(1, d), lnb.reshape(1, d), sc2, sh2, rw_t, rb.reshape(N_EXPERTS, 1),
      h2_all, ti_all, tg_all)


def _outproj_even(a_mix, zin, conv_w, **kw):
    bm = kw["bm"]
    t = zin.shape[1]
    hb = bm // HALO
    last = t // HALO - 1
    cur = lambda sec: pl.BlockSpec((1, bm, B_WIDTH), lambda b, i: (b, i, sec))
    before = lambda sec: pl.BlockSpec((1, HALO, B_WIDTH), lambda b, i: (b, jnp.maximum(i * hb - 1, 0), sec))
    after = lambda sec: pl.BlockSpec((1, HALO, B_WIDTH), lambda b, i: (b, jnp.minimum((i + 1) * hb, last), sec))
    specs = [pl.BlockSpec((1, bm, A_WIDTH), lambda b, i: (b, i, 0)), cur(5), cur(6), cur(7),
             before(5), before(7), after(5), after(7),
             pl.BlockSpec((CONV_W, B_WIDTH), lambda b, i: (0, 0))]
    return _outproj("even", [a_mix, zin, zin, zin, zin, zin, zin, zin, conv_w], specs, **kw)


def _outproj_odd(c_mix, zin, d_norm_w, d_norm_b, sp_w, sp_b, **kw):
    bm = kw["bm"]
    cur = lambda sec: pl.BlockSpec((1, bm, D_WIDTH), lambda b, i: (b, i, sec))
    specs = [pl.BlockSpec((1, bm, C_WIDTH), lambda b, i: (b, i, 0)), cur(1), cur(2),
             pl.BlockSpec((1, D_WIDTH), lambda b, i: (0, 0)), pl.BlockSpec((1, D_WIDTH), lambda b, i: (0, 0)),
             pl.BlockSpec((D_GROUPS, D_CHUNK, D_CHUNK), lambda b, i: (0, 0, 0)),
             pl.BlockSpec((D_CHUNK, D_GROUPS), lambda b, i: (0, 0))]
    return _outproj("odd", [c_mix, zin, zin, d_norm_w.reshape(1, D_WIDTH), d_norm_b.reshape(1, D_WIDTH),
                            sp_w.astype(BF16), sp_b.T], specs, **kw)


def _dft_mats(n):
    ang = 2.0 * np.pi * np.outer(np.arange(n), np.arange(n)) / n
    return np.cos(ang), np.sin(ang)


def _fourier_cw_kernel(z_ref, nw_ref, cs_ref, kc_ref, ks_ref, p_ref, q_ref, ps_ref, qs_ref):
    tile = z_ref.shape[1]
    half = 256
    grows = tile // GRID_W
    pitch = GRID_W + 1
    for hs in range(tile // half):
        z = z_ref[0, hs * half:(hs + 1) * half, :].astype(F32)
        a_parts, b_parts = [], []
        for g in range(C_GROUPS):
            zg = z[:, g * C_GROUP:(g + 1) * C_GROUP]
            zg = zg * lax.rsqrt(jnp.mean(zg * zg, axis=-1, keepdims=True) + EPS) * nw_ref[:, g * C_GROUP:(g + 1) * C_GROUP]
            ab = jnp.dot(zg.astype(BF16), cs_ref[...], preferred_element_type=F32)
            a_parts.append(ab[:, :C_GROUP])
            b_parts.append(ab[:, C_GROUP:])
        ab = jnp.concatenate(a_parts + b_parts, axis=1).astype(BF16)
        m1 = jnp.dot(kc_ref[...], ab, preferred_element_type=F32)
        m2 = jnp.dot(ks_ref[...], ab, preferred_element_type=F32)
        pv = m1[:, :C_WIDTH] - m2[:, C_WIDTH:]
        qv = m2[:, :C_WIDTH] + m1[:, C_WIDTH:]
        for r in range(half // GRID_W):
            at = (hs * (half // GRID_W) + r) * pitch
            for g in range(C_GROUPS):
                ps_ref[g, at:at + GRID_W, :] = pv[r * GRID_W:(r + 1) * GRID_W, g * C_GROUP:(g + 1) * C_GROUP]
                qs_ref[g, at:at + GRID_W, :] = qv[r * GRID_W:(r + 1) * GRID_W, g * C_GROUP:(g + 1) * C_GROUP]
    for k2 in range(GRID_W):
        for g in range(C_GROUPS):
            p_ref[0, k2, :, g * C_GROUP:(g + 1) * C_GROUP] = ps_ref[g, pl.ds(k2, grows, stride=pitch), :]
            q_ref[0, k2, :, g * C_GROUP:(g + 1) * C_GROUP] = qs_ref[g, pl.ds(k2, grows, stride=pitch), :]


def _fourier_r_kernel(p_ref, q_ref, c_ref, s_ref, o_ref, ys_ref):
    nj = p_ref.shape[1]
    rows = p_ref.shape[2]
    pitch = rows + 1
    for j in range(nj):
        y = (jnp.dot(c_ref[...], p_ref[0, j].astype(BF16), preferred_element_type=F32)
             + jnp.dot(s_ref[...], q_ref[0, j].astype(BF16), preferred_element_type=F32))
        for g in range(C_GROUPS):
            ys_ref[g, j * pitch:j * pitch + rows, :] = y[:, g * C_GROUP:(g + 1) * C_GROUP]

    def put(k1, carry):
        for g in range(C_GROUPS):
            o_ref[0, k1, :, g * C_GROUP:(g + 1) * C_GROUP] = ys_ref[g, pl.ds(k1, nj, stride=pitch), :]
        return carry

    lax.fori_loop(0, rows, put, 0)


def _fourier_mix(zin, c_norm_w):
    bn, t, _ = zin.shape
    rows = t // GRID_W
    tile = 512
    grows = tile // GRID_W
    c3, s3 = _dft_mats(C_GROUP)
    c2, s2 = _dft_mats(GRID_W)
    c1, s1 = _dft_mats(rows)
    scale = 1.0 / math.sqrt(rows * GRID_W * C_GROUP)
    cs3 = jnp.asarray(np.concatenate([c3, s3], axis=1), BF16)
    eye = np.eye(256 // GRID_W)
    kc2 = jnp.asarray(np.kron(eye, c2), BF16)
    ks2 = jnp.asarray(np.kron(eye, s2), BF16)
    c1s = jnp.asarray(c1 * scale, BF16)
    s1s = jnp.asarray(-s1 * scale, BF16)
    pq_shape = jax.ShapeDtypeStruct((bn, GRID_W, rows, C_WIDTH), F32)
    pq_spec = pl.BlockSpec((1, GRID_W, grows, C_WIDTH), lambda b, i: (b, 0, i, 0))
    p, q = pl.pallas_call(
        _fourier_cw_kernel,
        grid=(bn, t // tile),
        in_specs=[pl.BlockSpec((1, tile, C_WIDTH), lambda b, i: (b, i, 0)),
                  pl.BlockSpec((1, C_WIDTH), lambda b, i: (0, 0)),
                  pl.BlockSpec((C_GROUP, 2 * C_GROUP), lambda b, i: (0, 0)),
                  pl.BlockSpec((256, 256), lambda b, i: (0, 0)),
                  pl.BlockSpec((256, 256), lambda b, i: (0, 0))],
        out_specs=[pq_spec, pq_spec],
        out_shape=[pq_shape, pq_shape],
        scratch_shapes=[pltpu.VMEM((C_GROUPS, grows * (GRID_W + 1), C_GROUP), F32),
                        pltpu.VMEM((C_GROUPS, grows * (GRID_W + 1), C_GROUP), F32)],
        compiler_params=_params("parallel", "parallel"),
        name="fourier_chan_col",
    )(zin, c_norm_w.reshape(1, C_WIDTH), cs3, kc2, ks2)
    nj = 8
    in_spec = pl.BlockSpec((1, nj, rows, C_WIDTH), lambda b, j: (b, j, 0, 0))
    mat_spec = pl.BlockSpec((rows, rows), lambda b, j: (0, 0))
    y = pl.pallas_call(
        _fourier_r_kernel,
        grid=(bn, GRID_W // nj),
        in_specs=[in_spec, in_spec, mat_spec, mat_spec],
        out_specs=pl.BlockSpec((1, rows, nj, C_WIDTH), lambda b, j: (b, 0, j, 0)),
        out_shape=jax.ShapeDtypeStruct((bn, rows, GRID_W, C_WIDTH), F32),
        scratch_shapes=[pltpu.VMEM((C_GROUPS, nj * (rows + 1), C_GROUP), F32)],
        compiler_params=_params("parallel", "parallel"),
        name="fourier_rows",
    )(p, q, c1s, s1s)
    return y.reshape(bn, t, C_WIDTH)


def _rank_kernel(ti_ref, rank_ref, cnt_ref, run_ref):
    @pl.when(pl.program_id(0) == 0)
    def _():
        run_ref[...] = jnp.zeros_like(run_ref)

    ti = ti_ref[...]
    tb = ti.shape[1]
    eio = lax.broadcasted_iota(jnp.int32, (N_EXPERTS, tb), 0)
    hot = [eio == ti[k:k + 1, :] for k in range(TOP_K)]
    occ = sum(h.astype(F32) for h in hot)
    si = lax.broadcasted_iota(jnp.int32, (tb, tb), 0)
    ti_ = lax.broadcasted_iota(jnp.int32, (tb, tb), 1)
    before = jnp.where(si < ti_, 1.0, 0.0).astype(BF16)
    seen = jnp.dot(occ.astype(BF16), before, preferred_element_type=F32) + run_ref[:, 0:1]
    rank_ref[...] = jnp.concatenate(
        [jnp.sum(jnp.where(h, seen, 0.0), axis=0, keepdims=True) for h in hot], axis=0).astype(jnp.int32)
    run_ref[...] = run_ref[...] + jnp.sum(occ, axis=1, keepdims=True)
    cnt_ref[...] = run_ref[...].astype(jnp.int32)


def _ranks(top_i):
    ntok = top_i.shape[1]
    tb = ROUTE_TILE if ntok % ROUTE_TILE == 0 else 256
    rank, cnt = pl.pallas_call(
        _rank_kernel,
        grid=(ntok // tb,),
        in_specs=[pl.BlockSpec((TOP_K, tb), lambda i: (0, i))],
        out_specs=[pl.BlockSpec((TOP_K, tb), lambda i: (0, i)),
                   pl.BlockSpec((N_EXPERTS, 128), lambda i: (0, 0))],
        out_shape=[jax.ShapeDtypeStruct((TOP_K, ntok), jnp.int32),
                   jax.ShapeDtypeStruct((N_EXPERTS, 128), jnp.int32)],
        scratch_shapes=[pltpu.VMEM((N_EXPERTS, 128), F32)],
        compiler_params=_params("arbitrary"),
        name="route_rank",
    )(top_i)
    return rank, cnt[:, 0]


def _w1_split_kernel(w_ref, g_ref, l_ref, s_ref):
    kdim, cols = w_ref.shape[2], w_ref.shape[3]
    half = cols // 2
    nk = kdim // LANES
    t = w_ref[0, 0].T
    for kc in range(nk):
        s_ref[kc] = t[:, kc * LANES:(kc + 1) * LANES]
    even = jnp.concatenate([s_ref[kc, pl.ds(0, half, stride=2), :] for kc in range(nk)], axis=1)
    odd = jnp.concatenate([s_ref[kc, pl.ds(1, half, stride=2), :] for kc in range(nk)], axis=1)
    g_ref[0] = even.T.astype(BF16)
    l_ref[0] = odd.T.astype(BF16)


def _w1_split(w1_all, layer):
    _, e, d, f2 = w1_all.shape
    cols = 1024
    out = jax.ShapeDtypeStruct((e, d, f2 // 2), BF16)
    out_spec = pl.BlockSpec((1, d, cols // 2), lambda i, c: (i, 0, c))
    return pl.pallas_call(
        _w1_split_kernel,
        grid=(e, f2 // cols),
        in_specs=[pl.BlockSpec((1, 1, d, cols), lambda i, c: (layer, i, 0, c))],
        out_specs=[out_spec, out_spec],
        out_shape=[out, out],
        scratch_shapes=[pltpu.VMEM((d // LANES, cols, LANES), F32)],
        compiler_params=_params("parallel", "parallel"),
        name="moe_w1_split",
    )(w1_all)


def _cast_kernel(w_ref, o_ref):
    o_ref[...] = w_ref[0].astype(o_ref.dtype)


def _w2_cast(w2_all, layer):
    _, e, f, d = w2_all.shape
    return pl.pallas_call(
        _cast_kernel,
        grid=(e,),
        in_specs=[pl.BlockSpec((1, 1, f, d), lambda i: (layer, i, 0, 0))],
        out_specs=pl.BlockSpec((1, f, d), lambda i: (i, 0, 0)),
        out_shape=jax.ShapeDtypeStruct((e, f, d), BF16),
        compiler_params=_params("parallel"),
        name="moe_w2_cast",
    )(w2_all)


def _expert_kernel(be_ref, used_ref, x_ref, w1g_ref, w1l_ref, w2_ref, b1g_ref, b1l_ref, b2_ref, y_ref):
    del be_ref
    rows = x_ref.shape[0] // ROW_TILES
    in_use = pl.program_id(0) < used_ref[0]

    @pl.when(in_use)
    def _():
        x = _load_row_tiles(x_ref, rows).astype(BF16)
        glu = jnp.minimum(jnp.dot(x, w1g_ref[0], preferred_element_type=F32) + b1g_ref[0], SWIGLU_LIMIT)
        lin = jnp.clip(jnp.dot(x, w1l_ref[0], preferred_element_type=F32) + b1l_ref[0],
                       -SWIGLU_LIMIT, SWIGLU_LIMIT)
        act = glu * _sigmoid(SWIGLU_ALPHA * glu) * (lin + 1.0)
        _store_row_tiles(y_ref, jnp.dot(act.astype(BF16), w2_ref[0], preferred_element_type=F32) + b2_ref[0])

    @pl.when(jnp.logical_not(in_use))
    def _():
        y_ref[...] = jnp.zeros_like(y_ref)


def _experts(x_sorted, block_expert, blocks_used, w1g, w1l, w2, b1g, b1l, b2):
    d, f = w1g.shape[1], w1g.shape[2]
    nb = block_expert.shape[0]
    buf_rows = MOE_ROWS * ROW_TILES
    wspec = lambda k, n: pl.BlockSpec((1, k, n), lambda j, be, nu: (be[j], 0, 0))
    row_spec = pl.BlockSpec((buf_rows, LANES), lambda j, be, nu: (j, 0))
    grid_spec = pltpu.PrefetchScalarGridSpec(
        num_scalar_prefetch=2,
        grid=(nb,),
        in_specs=[row_spec, wspec(d, f), wspec(d, f), wspec(f, d), wspec(1, f), wspec(1, f), wspec(1, d)],
        out_specs=row_spec)
    return pl.pallas_call(
        _expert_kernel,
        grid_spec=grid_spec,
        out_shape=jax.ShapeDtypeStruct(x_sorted.shape, F32),
        compiler_params=_params("parallel"),
        name="moe_experts",
    )(block_expert, blocks_used, x_sorted, w1g, w1l, w2, b1g, b1l, b2)


def _dispatch_kernel(cnt_ref, start_ref, dst_ref, h_ref, xs_hbm, rowbuf, zero_ref, sem, zsem, *, nt, nslots):
    i = pl.program_id(0)
    tb = h_ref.shape[0]
    slot = i % 2

    def landed(buf):
        for _ in range(TOP_K):
            pltpu.make_async_copy(rowbuf.at[buf, pl.ds(0, tb * ROW_TILES), :],
                                  xs_hbm.at[pl.ds(0, tb * ROW_TILES), :], sem.at[buf]).wait()

    @pl.when(i >= 2)
    def _():
        landed(slot)

    _store_row_tiles(rowbuf.at[slot], h_ref[...].astype(F32), pitch=ROW_PITCH)
    for r in range(tb):
        for k in range(TOP_K):
            dst = pl.multiple_of(dst_ref[0, k, r] * ROW_TILES, ROW_TILES)
            pltpu.make_async_copy(rowbuf.at[slot, pl.ds(r * ROW_PITCH, ROW_TILES), :],
                                  xs_hbm.at[pl.ds(dst, ROW_TILES), :], sem.at[slot]).start(priority=k % 2)

    @pl.when(i == nt - 1)
    def _():
        landed(slot)
        if nt > 1:
            landed(1 - slot)
        zero_ref[...] = jnp.zeros_like(zero_ref)

        zrows = zero_ref.shape[0] // ROW_TILES

        def zero_rows(pos, n):
            cp = pltpu.make_async_copy(
                zero_ref.at[pl.ds(0, n * ROW_TILES), :],
                xs_hbm.at[pl.ds(pl.multiple_of(pos * ROW_TILES, ROW_TILES), n * ROW_TILES), :], zsem)
            cp.start()
            cp.wait()

        end = 0
        for e in range(N_EXPERTS):
            pos = start_ref[e] + cnt_ref[e]
            end = start_ref[e] + (cnt_ref[e] + MOE_ROWS - 1) // MOE_ROWS * MOE_ROWS
            gap = end - pos
            bit = zrows
            while bit >= 1:
                take = (gap & bit) != 0
                pl.when(take)(functools.partial(zero_rows, pos, bit))
                pos = pos + jnp.where(take, bit, 0)
                bit //= 2

        def tail(i, carry):
            zero_rows(end + i * zrows, zrows)
            return carry
        lax.fori_loop(0, (nslots - end) // zrows, tail, 0)


def _dispatch(h2, dest3, counts, pad_starts, nslots):
    ntok, d = h2.shape
    tb = COMBINE_TILE
    nt = ntok // tb
    assert MOE_ROWS & (MOE_ROWS - 1) == 0, "the padding fill decomposes gap lengths into powers of two"
    grid_spec = pltpu.PrefetchScalarGridSpec(
        num_scalar_prefetch=2,
        grid=(nt,),
        in_specs=[pl.BlockSpec((1, TOP_K, tb), lambda i, c, s: (i, 0, 0), memory_space=pltpu.SMEM),
                  pl.BlockSpec((tb, d), lambda i, c, s: (i, 0))],
        out_specs=pl.BlockSpec(memory_space=pl.ANY),
        scratch_shapes=[pltpu.VMEM((2, tb * ROW_PITCH, LANES), F32),
                        pltpu.VMEM((MOE_ROWS // 2 * ROW_TILES, LANES), F32),
                        pltpu.SemaphoreType.DMA((2,)), pltpu.SemaphoreType.DMA(())])
    return pl.pallas_call(
        functools.partial(_dispatch_kernel, nt=nt, nslots=nslots),
        grid_spec=grid_spec,
        out_shape=jax.ShapeDtypeStruct((nslots * ROW_TILES, LANES), F32),
        compiler_params=_params("arbitrary"),
        name="moe_dispatch",
    )(counts, pad_starts, dest3, h2)


def _combine_kernel(dst_ref, dstn_ref, y_hbm, x1_ref, gate_ref, g2_ref, lnw_ref, lnb_ref, o_ref, ybuf, sem, *, nt):
    i = pl.program_id(0)
    tb = x1_ref.shape[0]
    slot = i % 2

    def row_copy(src, buf, k, r):
        rows = y_hbm.at[pl.ds(pl.multiple_of(src * ROW_TILES, ROW_TILES), ROW_TILES), :]
        return pltpu.make_async_copy(rows, ybuf.at[buf, k, pl.ds(r * ROW_PITCH, ROW_TILES), :], sem.at[buf])

    def gather(idx_ref, buf):
        for r in range(tb):
            for k in range(TOP_K):
                row_copy(idx_ref[0, k, r], buf, k, r).start(priority=k % 2)

    @pl.when(i == 0)
    def _():
        gather(dst_ref, 0)

    @pl.when(i + 1 < nt)
    def _():
        gather(dstn_ref, 1 - slot)

    for k in range(TOP_K):
        pltpu.make_async_copy(y_hbm.at[pl.ds(0, tb * ROW_TILES), :],
                              ybuf.at[slot, k, pl.ds(0, tb * ROW_TILES), :], sem.at[slot]).wait()

    y = sum(gate_ref[:, k:k + 1] * _load_row_tiles(ybuf.at[slot, k], tb, pitch=ROW_PITCH) for k in range(TOP_K))
    r = DN_ALPHA * x1_ref[...] + g2_ref[0] * y
    mu = jnp.mean(r, axis=-1, keepdims=True)
    rc = r - mu
    var = jnp.mean(rc * rc, axis=-1, keepdims=True)
    o_ref[...] = rc * lax.rsqrt(var + EPS) * lnw_ref[...] + lnb_ref[...]


def _combine(x1, y_sorted, dest3, gates_t, tok_off, g2, mod_row, lnw, lnb):
    bn, t, d = x1.shape
    n = bn * t
    tb = COMBINE_TILE
    nt = n // tb
    off = tok_off // tb
    per_b = t // tb
    out = pl.pallas_call(
        functools.partial(_combine_kernel, nt=nt),
        grid=(nt,),
        in_specs=[pl.BlockSpec((1, TOP_K, tb), lambda i: (off + i, 0, 0), memory_space=pltpu.SMEM),
                  pl.BlockSpec((1, TOP_K, tb), lambda i: (off + jnp.minimum(i + 1, nt - 1), 0, 0),
                               memory_space=pltpu.SMEM),
                  pl.BlockSpec(memory_space=pl.ANY),
                  pl.BlockSpec((tb, d), lambda i: (i, 0)),
                  pl.BlockSpec((tb, TOP_K), lambda i: (off + i, 0)),
                  pl.BlockSpec((1, 1, d), lambda i: (mod_row(i // per_b), 0, 0)),
                  pl.BlockSpec((1, d), lambda i: (0, 0)),
                  pl.BlockSpec((1, d), lambda i: (0, 0))],
        out_specs=pl.BlockSpec((tb, d), lambda i: (i, 0)),
        out_shape=jax.ShapeDtypeStruct((n, d), F32),
        scratch_shapes=[pltpu.VMEM((2, TOP_K, tb * ROW_PITCH, LANES), F32), pltpu.SemaphoreType.DMA((2,))],
        compiler_params=_params("arbitrary"),
        name="moe_combine",
    )(dest3, dest3, y_sorted, x1.reshape(n, d), gates_t, g2, lnw.reshape(1, d), lnb.reshape(1, d))
    return out.reshape(bn, t, d)


def _dest_kernel(ti_ref, rank_ref, start_ref, dest_ref):
    ti = ti_ref[...]
    tb = ti.shape[1]
    eio = lax.broadcasted_iota(jnp.int32, (N_EXPERTS, tb), 0)
    start = start_ref[:, 0:1].astype(F32)
    base = jnp.concatenate(
        [jnp.sum(jnp.where(eio == ti[k:k + 1, :], start, 0.0), axis=0, keepdims=True) for k in range(TOP_K)], axis=0)
    dest = base.astype(jnp.int32) + rank_ref[...]
    for c in range(tb // COMBINE_TILE):
        dest_ref[c] = dest[:, c * COMBINE_TILE:(c + 1) * COMBINE_TILE]


def _moe_layout(top_i):
    ntok = top_i.shape[1]
    m = ntok * TOP_K
    rank, counts = _ranks(top_i)
    padded = (counts + MOE_ROWS - 1) // MOE_ROWS * MOE_ROWS
    pad_ends = jnp.cumsum(padded)
    pad_starts = pad_ends - padded
    tb = ROUTE_TILE if ntok % ROUTE_TILE == 0 else 256
    ntiles = ntok // COMBINE_TILE
    dest = pl.pallas_call(
        _dest_kernel,
        grid=(ntok // tb,),
        in_specs=[pl.BlockSpec((TOP_K, tb), lambda i: (0, i)),
                  pl.BlockSpec((TOP_K, tb), lambda i: (0, i)),
                  pl.BlockSpec((N_EXPERTS, LANES), lambda i: (0, 0))],
        out_specs=pl.BlockSpec((tb // COMBINE_TILE, TOP_K, COMBINE_TILE), lambda i: (i, 0, 0)),
        out_shape=jax.ShapeDtypeStruct((ntiles, TOP_K, COMBINE_TILE), jnp.int32),
        compiler_params=_params("parallel"),
        name="route_dest",
    )(top_i, rank, jnp.broadcast_to(pad_starts[:, None], (N_EXPERTS, LANES)))
    nb = (m + N_EXPERTS * (MOE_ROWS - 1) + MOE_ROWS - 1) // MOE_ROWS
    block_starts = jnp.arange(nb, dtype=jnp.int32) * MOE_ROWS
    block_expert = jnp.minimum(jnp.sum(block_starts[:, None] >= pad_ends[None, :], axis=1),
                               N_EXPERTS - 1).astype(jnp.int32)
    blocks_used = (pad_ends[-1:] // MOE_ROWS).astype(jnp.int32)
    return dest, counts, pad_starts.astype(jnp.int32), block_expert, blocks_used, nb * MOE_ROWS


def kernel(x, c, ctx, c_ctx, lower_bounds_fwd, lower_bounds_bwd, ada_w, ada_b, w_in_even, a_norm_w, conv_w,
           w_in_odd, c_norm_w, d_norm_w, d_norm_b, spatial_w, spatial_b, w_out, ln_mix_w, ln_mix_b,
           ln_ffn_w, ln_ffn_b, router_w, router_b, moe_w1, moe_b1, moe_w2, moe_b2):
    bn, t, d = x.shape
    tc = ctx.shape[1]
    n_lat = bn * t
    n_ctx = bn * tc
    ctx_row = bn
    lat_row = lambda b: b
    ctx_mod_row = lambda b: ctx_row

    lb_f_all = jnp.cumsum(jax.nn.softmax(lower_bounds_fwd.astype(F32), axis=0), axis=0)
    lb_b_all = jnp.cumsum(jax.nn.softmax(lower_bounds_bwd.astype(F32), axis=0), axis=0)

    cond = jnp.zeros((MOD_ROWS, d), F32).at[:bn].set(c).at[ctx_row].set(c_ctx)
    mods = _ada(cond, ada_w, ada_b).reshape(DEPTH, MOD_ROWS, 6, 1, d)

    h_ctx = ctx
    for l in range(DEPTH):
        last = l == DEPTH - 1
        sh1, sc1, g1, sh2, sc2, g2 = (mods[l, :, j] for j in range(6))
        wo = w_out[l].astype(BF16)
        rw_t = router_w[l].T.astype(BF16)
        ntok = n_lat if last else n_lat + n_ctx
        tok_state = (jnp.zeros((ntok, d), BF16), jnp.zeros((TOP_K, ntok), jnp.int32),
                     jnp.zeros((TOP_K, ntok), F32))
        post = dict(wo=wo, lnw=ln_mix_w[l], lnb=ln_mix_b[l], rw_t=rw_t, rb=router_b[l])
        bm_ctx = min(ROW_TILE, tc)
        if l % 2 == 0:
            e = l // 2
            w_in = w_in_even[e].astype(BF16)
            zin_c = _inproj(h_ctx, sc1, sh1, w_in, ctx_mod_row, bm=bm_ctx)
            zin = _inproj(x, sc1, sh1, w_in, lat_row, bm=ROW_TILE)
            zeros = jnp.zeros((bn, A_HEADS, A_DV, A_DK), F32)
            oc_f, sc_f = _scan(zin_c, lb_f_all[l], zeros, reverse=False)
            a_ctx, sc_b = _scan(zin_c, lb_b_all[l], zeros, reverse=True, o_prev=oc_f, norm_w=a_norm_w[e])
            o_f, _ = _scan(zin, lb_f_all[l], sc_f, reverse=False)
            a_lat, _ = _scan(zin, lb_b_all[l], sc_b, reverse=True, o_prev=o_f, norm_w=a_norm_w[e])
            x1, *tok_state = _outproj_even(a_lat, zin, conv_w[e], x=x, mods=(g1, sc2, sh2), mod_row=lat_row,
                                           tok_state=tok_state, tok_off=0, bm=ROW_TILE, **post)
            if not last:
                hc1, *tok_state = _outproj_even(a_ctx, zin_c, conv_w[e], x=h_ctx, mods=(g1, sc2, sh2),
                                                mod_row=ctx_mod_row, tok_state=tok_state, tok_off=n_lat,
                                                bm=bm_ctx, **post)
        else:
            o = l // 2
            w_in = w_in_odd[o].astype(BF16)
            odd = dict(d_norm_w=d_norm_w[o], d_norm_b=d_norm_b[o], sp_w=spatial_w[o], sp_b=spatial_b[o])
            zin = _inproj(x, sc1, sh1, w_in, lat_row, bm=ROW_TILE)
            c_mix = _fourier_mix(zin, c_norm_w[o])
            x1, *tok_state = _outproj_odd(c_mix, zin, x=x, mods=(g1, sc2, sh2), mod_row=lat_row,
                                          tok_state=tok_state, tok_off=0, bm=ROW_TILE, **odd, **post)
            if not last:
                raise NotImplementedError("an odd layer that is not the last needs the context Fourier mixer")
        h2_all, top_i, top_g = tok_state
        dest, counts, pad_starts, block_expert, blocks_used, nslots = _moe_layout(top_i)
        x_sorted = _dispatch(h2_all, dest, counts, pad_starts, nslots)
        w1g, w1l = _w1_split(moe_w1, l)
        y_sorted = _experts(
            x_sorted, block_expert, blocks_used, w1g, w1l, _w2_cast(moe_w2, l),
            moe_b1[l][:, None, 0::2], moe_b1[l][:, None, 1::2], moe_b2[l][:, None, :])
        gates_t = top_g.T
        x = _combine(x1, y_sorted, dest, gates_t, 0, g2, lat_row, ln_ffn_w[l], ln_ffn_b[l])
        if not last:
            h_ctx = _combine(hc1, y_sorted, dest, gates_t, n_lat, g2, ctx_mod_row, ln_ffn_w[l], ln_ffn_b[l])
    return x
```

```python
import functools
import math

import numpy as np
import jax
import jax.numpy as jnp
from jax import lax
from jax.experimental import pallas as pl
from jax.experimental.pallas import tpu as pltpu

F32 = jnp.float32
BF16 = jnp.bfloat16

D_MODEL = 1024
DEPTH = 2
GRID_W = 64
A_HEADS = 4
A_DK = 128
A_DV = 128
A_KW = A_HEADS * A_DK
A_WIDTH = A_HEADS * A_DV
B_WIDTH = D_MODEL - A_WIDTH
CONV_W = 3
C_GROUPS = 4
C_GROUP = 128
C_WIDTH = C_GROUPS * C_GROUP
D_WIDTH = D_MODEL - C_WIDTH
D_GROUPS = 4
D_GDIM = D_WIDTH // D_GROUPS
D_CHUNK = 128
SCAN_CHUNK = 32
EVEN_IN = 3 * A_KW + 2 * A_WIDTH + 3 * B_WIDTH
ODD_IN = C_WIDTH + 2 * D_WIDTH
N_EXPERTS = 32
TOP_K = 4
D_EXPERT = 1024
SWIGLU_ALPHA = 1.702
SWIGLU_LIMIT = 7.0
DN_ALPHA = (2 * DEPTH) ** 0.25
EPS = 1e-5

MOD_ROWS = 8
SCAN_ROWS = 256
SCAN_HEADS = 4
SCAN_SPAN = 256
ROW_TILE = 512
SUB_TILE = 256
MOE_ROWS = 512
ROUTE_TILE = 512
COMBINE_TILE = 256
HALO = 16
LANES = 128
ROW_TILES = D_MODEL // LANES
ROW_PITCH = ROW_TILES + 1

_NT = (((1,), (1,)), ((), ()))
_TN = (((0,), (0,)), ((), ()))


def _sigmoid(a):
    return 1.0 / (1.0 + jnp.exp(-a))


def _silu(a):
    return a * _sigmoid(a)


def _gelu(a):
    return 0.5 * a * (1.0 + lax.erf(a * (1.0 / math.sqrt(2.0))))


def _store_row_tiles(ref, val, pitch=ROW_TILES):
    rows = val.shape[0]
    for c in range(ROW_TILES):
        ref[pl.ds(c, rows, stride=pitch), :] = val[:, c * LANES:(c + 1) * LANES]


def _load_row_tiles(ref, rows, pitch=ROW_TILES):
    return jnp.concatenate([ref[pl.ds(c, rows, stride=pitch), :] for c in range(ROW_TILES)], axis=1)


V7X_VMEM_BYTES = 64 * 1024 * 1024
VMEM_LIMIT_BYTES = V7X_VMEM_BYTES - 8 * 1024 * 1024


def _params(*sem):
    return pltpu.CompilerParams(dimension_semantics=sem, vmem_limit_bytes=VMEM_LIMIT_BYTES)


def _ada_kernel(c_ref, w_ref, b_ref, o_ref):
    s = _silu(c_ref[...])
    o_ref[0] = jnp.dot(s, w_ref[0], preferred_element_type=F32, precision=lax.Precision.HIGHEST) + b_ref[0]


def _ada(cond, ada_w, ada_b):
    d = cond.shape[1]
    n6 = ada_w.shape[2]
    tn = 1536
    return pl.pallas_call(
        _ada_kernel,
        grid=(DEPTH, n6 // tn),
        in_specs=[pl.BlockSpec((MOD_ROWS, d), lambda l, n: (0, 0)),
                  pl.BlockSpec((1, d, tn), lambda l, n: (l, 0, n)),
                  pl.BlockSpec((1, 1, tn), lambda l, n: (l, 0, n))],
        out_specs=pl.BlockSpec((1, MOD_ROWS, tn), lambda l, n: (l, 0, n)),
        out_shape=jax.ShapeDtypeStruct((DEPTH, MOD_ROWS, n6), F32),
        compiler_params=_params("parallel", "parallel"),
        name="ada_mod",
    )(cond, ada_w, ada_b.reshape(DEPTH, 1, n6))


def _inproj_kernel(x_ref, sc_ref, sh_ref, w_ref, o_ref, *, tn):
    h = (x_ref[0] * (1.0 + sc_ref[0]) + sh_ref[0]).astype(BF16)
    for n in range(o_ref.shape[2] // tn):
        o_ref[0, :, n * tn:(n + 1) * tn] = jnp.dot(
            h, w_ref[:, n * tn:(n + 1) * tn], preferred_element_type=F32).astype(o_ref.dtype)


def _inproj(x, sc, sh, w, mod_row, bm):
    bn, t, d = x.shape
    nout = w.shape[1]
    mod_spec = pl.BlockSpec((1, 1, d), lambda b, i: (mod_row(b), 0, 0))
    return pl.pallas_call(
        functools.partial(_inproj_kernel, tn=512),
        grid=(bn, t // bm),
        in_specs=[pl.BlockSpec((1, bm, d), lambda b, i: (b, i, 0)), mod_spec, mod_spec,
                  pl.BlockSpec((d, nout), lambda b, i: (0, 0))],
        out_specs=pl.BlockSpec((1, bm, nout), lambda b, i: (b, i, 0)),
        out_shape=jax.ShapeDtypeStruct((bn, t, nout), BF16),
        compiler_params=_params("parallel", "parallel"),
        name="in_proj",
    )(x, sc, sh, w)


def _scan_kernel(*refs, reverse, finalize):
    if finalize:
        (q_ref, z_ref, v_ref, lb_ref, s0_ref, g_ref, op_ref, nw_ref, o_ref, sfin_ref, st_ref, oacc_ref) = refs
    else:
        (q_ref, z_ref, v_ref, lb_ref, s0_ref, o_ref, sfin_ref, st_ref, oacc_ref) = refs
    rows = q_ref.shape[1]
    heads = q_ref.shape[2] // A_DK
    nchunk = rows // SCAN_CHUNK

    @pl.when(pl.program_id(2) == 0)
    def _():
        st_ref[...] = s0_ref[0]

    q = _silu(q_ref[0].astype(F32))
    lb = lb_ref[...]
    f = lb + (1.0 - lb) * _sigmoid(z_ref[0].astype(F32))
    logf = jnp.log(f)
    k = 1.0 - f
    v = v_ref[0]

    ri = lax.broadcasted_iota(jnp.int32, (rows, rows), 0)
    ci = lax.broadcasted_iota(jnp.int32, (rows, rows), 1)
    same = (ri // SCAN_CHUNK) == (ci // SCAN_CHUNK)
    causal = same & ((ci >= ri) if reverse else (ci <= ri))
    tri = jnp.where(causal, 1.0, 0.0).astype(BF16)
    blk = jnp.where(same, 1.0, 0.0).astype(BF16)
    hi = logf.astype(BF16)
    lo = (logf - hi.astype(F32)).astype(BF16)
    b = jnp.dot(tri, hi, preferred_element_type=F32) + jnp.dot(tri, lo, preferred_element_type=F32)
    b_tot = jnp.dot(blk, hi, preferred_element_type=F32) + jnp.dot(blk, lo, preferred_element_type=F32)

    q_dec = (q * jnp.exp(b)).astype(BF16)
    k_inv = (k * jnp.exp(-b)).astype(BF16)
    k_end = (k * jnp.exp(b_tot - b)).astype(BF16)
    dec = jnp.exp(b_tot)

    span = min(SCAN_SPAN, rows)
    per = span // SCAN_CHUNK
    wide = per * A_DK
    spread = (lax.broadcasted_iota(jnp.int32, (span, wide), 0) // SCAN_CHUNK
              == lax.broadcasted_iota(jnp.int32, (span, wide), 1) // A_DK)
    causal_span = causal[0:span, 0:span]
    spans = range(rows // span - 1, -1, -1) if reverse else range(rows // span)
    order = range(per - 1, -1, -1) if reverse else range(per)
    for h in range(heads):
        hk = slice(h * A_DK, (h + 1) * A_DK)
        hv = slice(h * A_DV, (h + 1) * A_DV)
        st = st_ref[h]
        for sp in spans:
            rs = slice(sp * span, (sp + 1) * span)
            qd, vh = q_dec[rs, hk], v[rs, hv]
            scores = lax.dot_general(qd, k_inv[rs, hk], _NT, preferred_element_type=F32)
            scores = jnp.where(causal_span, scores, 0.0).astype(BF16)
            o_intra = jnp.dot(scores, vh, preferred_element_type=F32)
            ke_wide = jnp.where(spread, jnp.concatenate([k_end[rs, hk]] * per, axis=1), 0.0).astype(BF16)
            upd = lax.dot_general(vh, ke_wide, _TN, preferred_element_type=F32)
            seen = [None] * per
            for c in order:
                seen[c] = st.astype(BF16)
                r0 = sp * span + c * SCAN_CHUNK
                st = st * dec[r0:r0 + 1, hk] + upd[:, c * A_DK:(c + 1) * A_DK]
            qd_wide = jnp.where(spread, jnp.concatenate([qd] * per, axis=1), 0.0).astype(BF16)
            o_inter = lax.dot_general(qd_wide, jnp.concatenate(seen, axis=1), _NT, preferred_element_type=F32)
            oacc_ref[rs, hv] = o_intra + o_inter
        st_ref[h] = st
        sfin_ref[0, h] = st

    o = oacc_ref[...]
    if finalize:
        o = o + op_ref[0]
        gate = nw_ref[...] * _silu(g_ref[0].astype(F32))
        for h in range(heads):
            hv = slice(h * A_DV, (h + 1) * A_DV)
            oh = o[:, hv]
            oh = oh * lax.rsqrt(jnp.mean(oh * oh, axis=-1, keepdims=True) + EPS)
            o_ref[0, :, hv] = (oh * gate[:, hv]).astype(o_ref.dtype)
    else:
        o_ref[0] = o.astype(o_ref.dtype)


def _scan(zin, lb, s0, *, reverse, o_prev=None, norm_w=None):
    bn, t, _ = zin.shape
    finalize = o_prev is not None
    rows = min(SCAN_ROWS, t)
    nb = t // rows
    pos = (lambda n: nb - 1 - n) if reverse else (lambda n: n)
    hp = SCAN_HEADS
    groups = A_HEADS // hp
    hw = hp * A_DK

    def col(section):
        return pl.BlockSpec((1, rows, hw), lambda b, h, n: (b, pos(n), section * groups + h))

    state_spec = pl.BlockSpec((1, hp, A_DV, A_DK), lambda b, h, n: (b, h, 0, 0))
    in_specs = [col(0), col(2 if reverse else 1), col(3),
                pl.BlockSpec((1, hw), lambda b, h, n: (0, h)), state_spec]
    args = [zin, zin, zin, lb.reshape(1, A_KW), s0]
    if finalize:
        in_specs += [col(4),
                     pl.BlockSpec((1, rows, hw), lambda b, h, n: (b, pos(n), h)),
                     pl.BlockSpec((1, hw), lambda b, h, n: (0, h))]
        args += [zin, o_prev, norm_w.reshape(1, A_WIDTH)]
    return pl.pallas_call(
        functools.partial(_scan_kernel, reverse=reverse, finalize=finalize),
        grid=(bn, groups, nb),
        in_specs=in_specs,
        out_specs=[pl.BlockSpec((1, rows, hw), lambda b, h, n: (b, pos(n), h)), state_spec],
        out_shape=[jax.ShapeDtypeStruct((bn, t, A_WIDTH), BF16 if finalize else F32),
                   jax.ShapeDtypeStruct((bn, A_HEADS, A_DV, A_DK), F32)],
        scratch_shapes=[pltpu.VMEM((hp, A_DV, A_DK), F32), pltpu.VMEM((rows, hw), F32)],
        compiler_params=_params("parallel", "parallel", "arbitrary"),
        name="hgrn2_scan_bwd" if reverse else "hgrn2_scan_fwd",
    )(*args)


def _sub_tiles(bm):
    sub = min(bm, SUB_TILE)
    return [slice(s * sub, (s + 1) * sub) for s in range(bm // sub)]


def _post_mix(acc, rs, x_ref, g1_ref, lnw_ref, lnb_ref, sc2_ref, sh2_ref, rw_ref, rb_ref,
              x1_ref, h2_ref, ti_ref, tg_ref):
    r = DN_ALPHA * x_ref[0, rs, :] + g1_ref[0] * acc
    mu = jnp.mean(r, axis=-1, keepdims=True)
    rc = r - mu
    var = jnp.mean(rc * rc, axis=-1, keepdims=True)
    x1 = rc * lax.rsqrt(var + EPS) * lnw_ref[...] + lnb_ref[...]
    x1_ref[0, rs, :] = x1
    h2 = x1 * (1.0 + sc2_ref[0]) + sh2_ref[0]
    h2_ref[rs, :] = h2.astype(h2_ref.dtype)
    logits = lax.dot_general(rw_ref[...], h2.astype(BF16), _NT, preferred_element_type=F32) + rb_ref[...]
    iota = lax.broadcasted_iota(jnp.int32, logits.shape, 0)
    vals, idxs = [], []
    cur = logits
    for _ in range(TOP_K):
        m = jnp.max(cur, axis=0, keepdims=True)
        ik = jnp.min(jnp.where(cur == m, iota, N_EXPERTS), axis=0, keepdims=True)
        vals.append(m)
        idxs.append(ik)
        cur = jnp.where(iota == ik, -jnp.inf, cur)
    tv = jnp.concatenate(vals, axis=0)
    e = jnp.exp(tv - tv[0:1])
    tg_ref[:, rs] = e / jnp.sum(e, axis=0, keepdims=True)
    ti_ref[:, rs] = jnp.concatenate(idxs, axis=0)


def _outproj_even_kernel(a_ref, u_ref, gb_ref, gc_ref, up_ref, gcp_ref, un_ref, gcn_ref, cw_ref, wo_ref,
                         x_ref, g1_ref, lnw_ref, lnb_ref, sc2_ref, sh2_ref, rw_ref, rb_ref, alias_ref,
                         x1_ref, h2_ref, ti_ref, tg_ref):
    del alias_ref
    i = pl.program_id(1)
    bm = u_ref.shape[1]
    z = gc_ref[0].astype(F32) * u_ref[0].astype(F32)
    z_before = gcp_ref[0, HALO - 1:HALO, :].astype(F32) * up_ref[0, HALO - 1:HALO, :].astype(F32)
    z_after = gcn_ref[0, 0:1, :].astype(F32) * un_ref[0, 0:1, :].astype(F32)
    z_before = jnp.where(i == 0, 0.0, z_before)
    z_after = jnp.where(i == pl.num_programs(1) - 1, 0.0, z_after)
    row = lax.broadcasted_iota(jnp.int32, z.shape, 0)
    z_prev = jnp.where(row == 0, z_before, pltpu.roll(z, 1, axis=0))
    z_next = jnp.where(row == bm - 1, z_after, pltpu.roll(z, bm - 1, axis=0))
    zc = cw_ref[0:1, :] * z_prev + cw_ref[1:2, :] * z + cw_ref[2:3, :] * z_next
    b_mix = (gb_ref[0].astype(F32) * zc).astype(BF16)
    for rs in _sub_tiles(bm):
        acc = (jnp.dot(a_ref[0, rs, :], wo_ref[0:A_WIDTH, :], preferred_element_type=F32)
               + jnp.dot(b_mix[rs], wo_ref[A_WIDTH:, :], preferred_element_type=F32))
        _post_mix(acc, rs, x_ref, g1_ref, lnw_ref, lnb_ref, sc2_ref, sh2_ref, rw_ref, rb_ref,
                  x1_ref, h2_ref, ti_ref, tg_ref)


def _outproj_odd_kernel(c_ref, u_ref, v_ref, dw_ref, db_ref, spw_ref, spb_ref, wo_ref,
                        x_ref, g1_ref, lnw_ref, lnb_ref, sc2_ref, sh2_ref, rw_ref, rb_ref, alias_ref,
                        x1_ref, h2_ref, ti_ref, tg_ref):
    del alias_ref
    bm = u_ref.shape[1]
    u = _gelu(u_ref[0].astype(F32))
    v = _gelu(v_ref[0].astype(F32))
    cols = []
    for g in range(D_GROUPS):
        lo_c, hi_c = g * D_GDIM, (g + 1) * D_GDIM
        vg = v[:, lo_c:hi_c]
        mu = jnp.mean(vg, axis=-1, keepdims=True)
        vc = vg - mu
        var = jnp.mean(vc * vc, axis=-1, keepdims=True)
        vn = (vc * lax.rsqrt(var + EPS) * dw_ref[:, lo_c:hi_c] + db_ref[:, lo_c:hi_c]).astype(BF16)
        parts = []
        for ch in range(bm // D_CHUNK):
            sv = jnp.dot(spw_ref[g], vn[ch * D_CHUNK:(ch + 1) * D_CHUNK], preferred_element_type=F32)
            parts.append(sv + spb_ref[:, g:g + 1])
        cols.append(u[:, lo_c:hi_c] * jnp.concatenate(parts, axis=0))
    d_mix = jnp.concatenate(cols, axis=1).astype(BF16)
    for rs in _sub_tiles(bm):
        acc = (jnp.dot(c_ref[0, rs, :].astype(BF16), wo_ref[0:C_WIDTH, :], preferred_element_type=F32)
               + jnp.dot(d_mix[rs], wo_ref[C_WIDTH:, :], preferred_element_type=F32))
        _post_mix(acc, rs, x_ref, g1_ref, lnw_ref, lnb_ref, sc2_ref, sh2_ref, rw_ref, rb_ref,
                  x1_ref, h2_ref, ti_ref, tg_ref)


def _outproj(kind, mixer_args, mixer_specs, wo, x, mods, mod_row, lnw, lnb, rw_t, rb, tok_state, tok_off, bm):
    bn, t, d = x.shape
    nt = t // bm
    g1, sc2, sh2 = mods
    h2_all, ti_all, tg_all = tok_state
    ntok = h2_all.shape[0]
    off = tok_off // bm
    mod_spec = pl.BlockSpec((1, 1, d), lambda b, i: (mod_row(b), 0, 0))
    vec_spec = pl.BlockSpec((1, d), lambda b, i: (0, 0))
    in_specs = list(mixer_specs) + [
        pl.BlockSpec((d, d), lambda b, i: (0, 0)),
        pl.BlockSpec((1, bm, d), lambda b, i: (b, i, 0)),
        mod_spec, vec_spec, vec_spec, mod_spec, mod_spec,
        pl.BlockSpec((N_EXPERTS, d), lambda b, i: (0, 0)),
        pl.BlockSpec((N_EXPERTS, 1), lambda b, i: (0, 0)),
        pl.BlockSpec(memory_space=pl.ANY), pl.BlockSpec(memory_space=pl.ANY), pl.BlockSpec(memory_space=pl.ANY)]
    n_in = len(in_specs)
    kernel = _outproj_even_kernel if kind == "even" else _outproj_odd_kernel

    def body(*refs):
        ins, outs = refs[:n_in], refs[n_in:]
        kernel(*ins[:n_in - 3], ins[n_in - 3:], *outs)

    return pl.pallas_call(
        body,
        grid=(bn, nt),
        in_specs=in_specs,
        out_specs=[pl.BlockSpec((1, bm, d), lambda b, i: (b, i, 0)),
                   pl.BlockSpec((bm, d), lambda b, i: (off + b * nt + i, 0)),
                   pl.BlockSpec((TOP_K, bm), lambda b, i: (0, off + b * nt + i)),
                   pl.BlockSpec((TOP_K, bm), lambda b, i: (0, off + b * nt + i))],
        out_shape=[jax.ShapeDtypeStruct((bn, t, d), F32),
                   jax.ShapeDtypeStruct((ntok, d), BF16),
                   jax.ShapeDtypeStruct((TOP_K, ntok), jnp.int32),
                   jax.ShapeDtypeStruct((TOP_K, ntok), F32)],
        input_output_aliases={n_in - 3: 1, n_in - 2: 2, n_in - 1: 3},
        compiler_params=_params("parallel", "parallel"),
        name="out_proj_" + kind,
    )(*mixer_args, wo, x, g1, lnw.reshape(1, d), lnb.reshape(1, d), sc2, sh2, rw_t, rb.reshape(N_EXPERTS, 1),
      h2_all, ti_all, tg_all)


def _outproj_even(a_mix, zin, conv_w, **kw):
    bm = kw["bm"]
    t = zin.shape[1]
    hb = bm // HALO
    last = t // HALO - 1
    cur = lambda sec: pl.BlockSpec((1, bm, B_WIDTH), lambda b, i: (b, i, sec))
    before = lambda sec: pl.BlockSpec((1, HALO, B_WIDTH), lambda b, i: (b, jnp.maximum(i * hb - 1, 0), sec))
    after = lambda sec: pl.BlockSpec((1, HALO, B_WIDTH), lambda b, i: (b, jnp.minimum((i + 1) * hb, last), sec))
    specs = [pl.BlockSpec((1, bm, A_WIDTH), lambda b, i: (b, i, 0)), cur(5), cur(6), cur(7),
             before(5), before(7), after(5), after(7),
             pl.BlockSpec((CONV_W, B_WIDTH), lambda b, i: (0, 0))]
    return _outproj("even", [a_mix, zin, zin, zin, zin, zin, zin, zin, conv_w], specs, **kw)


def _outproj_odd(c_mix, zin, d_norm_w, d_norm_b, sp_w, sp_b, **kw):
    bm = kw["bm"]
    cur = lambda sec: pl.BlockSpec((1, bm, D_WIDTH), lambda b, i: (b, i, sec))
    specs = [pl.BlockSpec((1, bm, C_WIDTH), lambda b, i: (b, i, 0)), cur(1), cur(2),
             pl.BlockSpec((1, D_WIDTH), lambda b, i: (0, 0)), pl.BlockSpec((1, D_WIDTH), lambda b, i: (0, 0)),
             pl.BlockSpec((D_GROUPS, D_CHUNK, D_CHUNK), lambda b, i: (0, 0, 0)),
             pl.BlockSpec((D_CHUNK, D_GROUPS), lambda b, i: (0, 0))]
    return _outproj("odd", [c_mix, zin, zin, d_norm_w.reshape(1, D_WIDTH), d_norm_b.reshape(1, D_WIDTH),
                            sp_w.astype(BF16), sp_b.T], specs, **kw)


def _dft_mats(n):
    ang = 2.0 * np.pi * np.outer(np.arange(n), np.arange(n)) / n
    return np.cos(ang), np.sin(ang)


def _fourier_cw_kernel(z_ref, nw_ref, cs_ref, kc_ref, ks_ref, p_ref, q_ref, ps_ref, qs_ref):
    tile = z_ref.shape[1]
    half = 256
    grows = tile // GRID_W
    pitch = GRID_W + 1
    for hs in range(tile // half):
        z = z_ref[0, hs * half:(hs + 1) * half, :].astype(F32)
        a_parts, b_parts = [], []
        for g in range(C_GROUPS):
            zg = z[:, g * C_GROUP:(g + 1) * C_GROUP]
            zg = zg * lax.rsqrt(jnp.mean(zg * zg, axis=-1, keepdims=True) + EPS) * nw_ref[:, g * C_GROUP:(g + 1) * C_GROUP]
            ab = jnp.dot(zg.astype(BF16), cs_ref[...], preferred_element_type=F32)
            a_parts.append(ab[:, :C_GROUP])
            b_parts.append(ab[:, C_GROUP:])
        ab = jnp.concatenate(a_parts + b_parts, axis=1).astype(BF16)
        m1 = jnp.dot(kc_ref[...], ab, preferred_element_type=F32)
        m2 = jnp.dot(ks_ref[...], ab, preferred_element_type=F32)
        pv = m1[:, :C_WIDTH] - m2[:, C_WIDTH:]
        qv = m2[:, :C_WIDTH] + m1[:, C_WIDTH:]
        for r in range(half // GRID_W):
            at = (hs * (half // GRID_W) + r) * pitch
            for g in range(C_GROUPS):
                ps_ref[g, at:at + GRID_W, :] = pv[r * GRID_W:(r + 1) * GRID_W, g * C_GROUP:(g + 1) * C_GROUP]
                qs_ref[g, at:at + GRID_W, :] = qv[r * GRID_W:(r + 1) * GRID_W, g * C_GROUP:(g + 1) * C_GROUP]
    for k2 in range(GRID_W):
        for g in range(C_GROUPS):
            p_ref[0, k2, :, g * C_GROUP:(g + 1) * C_GROUP] = ps_ref[g, pl.ds(k2, grows, stride=pitch), :]
            q_ref[0, k2, :, g * C_GROUP:(g + 1) * C_GROUP] = qs_ref[g, pl.ds(k2, grows, stride=pitch), :]


def _fourier_r_kernel(p_ref, q_ref, c_ref, s_ref, o_ref, ys_ref):
    nj = p_ref.shape[1]
    rows = p_ref.shape[2]
    pitch = rows + 1
    for j in range(nj):
        y = (jnp.dot(c_ref[...], p_ref[0, j].astype(BF16), preferred_element_type=F32)
             + jnp.dot(s_ref[...], q_ref[0, j].astype(BF16), preferred_element_type=F32))
        for g in range(C_GROUPS):
            ys_ref[g, j * pitch:j * pitch + rows, :] = y[:, g * C_GROUP:(g + 1) * C_GROUP]

    def put(k1, carry):
        for g in range(C_GROUPS):
            o_ref[0, k1, :, g * C_GROUP:(g + 1) * C_GROUP] = ys_ref[g, pl.ds(k1, nj, stride=pitch), :]
        return carry

    lax.fori_loop(0, rows, put, 0)


def _fourier_mix(zin, c_norm_w):
    bn, t, _ = zin.shape
    rows = t // GRID_W
    tile = 512
    grows = tile // GRID_W
    c3, s3 = _dft_mats(C_GROUP)
    c2, s2 = _dft_mats(GRID_W)
    c1, s1 = _dft_mats(rows)
    scale = 1.0 / math.sqrt(rows * GRID_W * C_GROUP)
    cs3 = jnp.asarray(np.concatenate([c3, s3], axis=1), BF16)
    eye = np.eye(256 // GRID_W)
    kc2 = jnp.asarray(np.kron(eye, c2), BF16)
    ks2 = jnp.asarray(np.kron(eye, s2), BF16)
    c1s = jnp.asarray(c1 * scale, BF16)
    s1s = jnp.asarray(-s1 * scale, BF16)
    pq_shape = jax.ShapeDtypeStruct((bn, GRID_W, rows, C_WIDTH), F32)
    pq_spec = pl.BlockSpec((1, GRID_W, grows, C_WIDTH), lambda b, i: (b, 0, i, 0))
    p, q = pl.pallas_call(
        _fourier_cw_kernel,
        grid=(bn, t // tile),
        in_specs=[pl.BlockSpec((1, tile, C_WIDTH), lambda b, i: (b, i, 0)),
                  pl.BlockSpec((1, C_WIDTH), lambda b, i: (0, 0)),
                  pl.BlockSpec((C_GROUP, 2 * C_GROUP), lambda b, i: (0, 0)),
                  pl.BlockSpec((256, 256), lambda b, i: (0, 0)),
                  pl.BlockSpec((256, 256), lambda b, i: (0, 0))],
        out_specs=[pq_spec, pq_spec],
        out_shape=[pq_shape, pq_shape],
        scratch_shapes=[pltpu.VMEM((C_GROUPS, grows * (GRID_W + 1), C_GROUP), F32),
                        pltpu.VMEM((C_GROUPS, grows * (GRID_W + 1), C_GROUP), F32)],
        compiler_params=_params("parallel", "parallel"),
        name="fourier_chan_col",
    )(zin, c_norm_w.reshape(1, C_WIDTH), cs3, kc2, ks2)
    nj = 8
    in_spec = pl.BlockSpec((1, nj, rows, C_WIDTH), lambda b, j: (b, j, 0, 0))
    mat_spec = pl.BlockSpec((rows, rows), lambda b, j: (0, 0))
    y = pl.pallas_call(
        _fourier_r_kernel,
        grid=(bn, GRID_W // nj),
        in_specs=[in_spec, in_spec, mat_spec, mat_spec],
        out_specs=pl.BlockSpec((1, rows, nj, C_WIDTH), lambda b, j: (b, 0, j, 0)),
        out_shape=jax.ShapeDtypeStruct((bn, rows, GRID_W, C_WIDTH), F32),
        scratch_shapes=[pltpu.VMEM((C_GROUPS, nj * (rows + 1), C_GROUP), F32)],
        compiler_params=_params("parallel", "parallel"),
        name="fourier_rows",
    )(p, q, c1s, s1s)
    return y.reshape(bn, t, C_WIDTH)


def _rank_kernel(ti_ref, rank_ref, cnt_ref, run_ref):
    @pl.when(pl.program_id(0) == 0)
    def _():
        run_ref[...] = jnp.zeros_like(run_ref)

    ti = ti_ref[...]
    tb = ti.shape[1]
    eio = lax.broadcasted_iota(jnp.int32, (N_EXPERTS, tb), 0)
    hot = [eio == ti[k:k + 1, :] for k in range(TOP_K)]
    occ = sum(h.astype(F32) for h in hot)
    si = lax.broadcasted_iota(jnp.int32, (tb, tb), 0)
    ti_ = lax.broadcasted_iota(jnp.int32, (tb, tb), 1)
    before = jnp.where(si < ti_, 1.0, 0.0).astype(BF16)
    seen = jnp.dot(occ.astype(BF16), before, preferred_element_type=F32) + run_ref[:, 0:1]
    rank_ref[...] = jnp.concatenate(
        [jnp.sum(jnp.where(h, seen, 0.0), axis=0, keepdims=True) for h in hot], axis=0).astype(jnp.int32)
    run_ref[...] = run_ref[...] + jnp.sum(occ, axis=1, keepdims=True)
    cnt_ref[...] = run_ref[...].astype(jnp.int32)


def _ranks(top_i):
    ntok = top_i.shape[1]
    tb = ROUTE_TILE if ntok % ROUTE_TILE == 0 else 256
    rank, cnt = pl.pallas_call(
        _rank_kernel,
        grid=(ntok // tb,),
        in_specs=[pl.BlockSpec((TOP_K, tb), lambda i: (0, i))],
        out_specs=[pl.BlockSpec((TOP_K, tb), lambda i: (0, i)),
                   pl.BlockSpec((N_EXPERTS, 128), lambda i: (0, 0))],
        out_shape=[jax.ShapeDtypeStruct((TOP_K, ntok), jnp.int32),
                   jax.ShapeDtypeStruct((N_EXPERTS, 128), jnp.int32)],
        scratch_shapes=[pltpu.VMEM((N_EXPERTS, 128), F32)],
        compiler_params=_params("arbitrary"),
        name="route_rank",
    )(top_i)
    return rank, cnt[:, 0]


def _w1_split_kernel(w_ref, g_ref, l_ref, s_ref):
    kdim, cols = w_ref.shape[2], w_ref.shape[3]
    half = cols // 2
    nk = kdim // LANES
    t = w_ref[0, 0].T
    for kc in range(nk):
        s_ref[kc] = t[:, kc * LANES:(kc + 1) * LANES]
    even = jnp.concatenate([s_ref[kc, pl.ds(0, half, stride=2), :] for kc in range(nk)], axis=1)
    odd = jnp.concatenate([s_ref[kc, pl.ds(1, half, stride=2), :] for kc in range(nk)], axis=1)
    g_ref[0] = even.T.astype(BF16)
    l_ref[0] = odd.T.astype(BF16)


def _w1_split(w1_all, layer):
    _, e, d, f2 = w1_all.shape
    cols = 1024
    out = jax.ShapeDtypeStruct((e, d, f2 // 2), BF16)
    out_spec = pl.BlockSpec((1, d, cols // 2), lambda i, c: (i, 0, c))
    return pl.pallas_call(
        _w1_split_kernel,
        grid=(e, f2 // cols),
        in_specs=[pl.BlockSpec((1, 1, d, cols), lambda i, c: (layer, i, 0, c))],
        out_specs=[out_spec, out_spec],
        out_shape=[out, out],
        scratch_shapes=[pltpu.VMEM((d // LANES, cols, LANES), F32)],
        compiler_params=_params("parallel", "parallel"),
        name="moe_w1_split",
    )(w1_all)


def _cast_kernel(w_ref, o_ref):
    o_ref[...] = w_ref[0].astype(o_ref.dtype)


def _w2_cast(w2_all, layer):
    _, e, f, d = w2_all.shape
    return pl.pallas_call(
        _cast_kernel,
        grid=(e,),
        in_specs=[pl.BlockSpec((1, 1, f, d), lambda i: (layer, i, 0, 0))],
        out_specs=pl.BlockSpec((1, f, d), lambda i: (i, 0, 0)),
        out_shape=jax.ShapeDtypeStruct((e, f, d), BF16),
        compiler_params=_params("parallel"),
        name="moe_w2_cast",
    )(w2_all)


def _expert_kernel(be_ref, used_ref, x_ref, w1g_ref, w1l_ref, w2_ref, b1g_ref, b1l_ref, b2_ref, y_ref):
    del be_ref
    rows = x_ref.shape[0] // ROW_TILES
    in_use = pl.program_id(0) < used_ref[0]

    @pl.when(in_use)
    def _():
        x = _load_row_tiles(x_ref, rows).astype(BF16)
        glu = jnp.minimum(jnp.dot(x, w1g_ref[0], preferred_element_type=F32) + b1g_ref[0], SWIGLU_LIMIT)
        lin = jnp.clip(jnp.dot(x, w1l_ref[0], preferred_element_type=F32) + b1l_ref[0],
                       -SWIGLU_LIMIT, SWIGLU_LIMIT)
        act = glu * _sigmoid(SWIGLU_ALPHA * glu) * (lin + 1.0)
        _store_row_tiles(y_ref, jnp.dot(act.astype(BF16), w2_ref[0], preferred_element_type=F32) + b2_ref[0])

    @pl.when(jnp.logical_not(in_use))
    def _():
        y_ref[...] = jnp.zeros_like(y_ref)


def _experts(x_sorted, block_expert, blocks_used, w1g, w1l, w2, b1g, b1l, b2):
    d, f = w1g.shape[1], w1g.shape[2]
    nb = block_expert.shape[0]
    buf_rows = MOE_ROWS * ROW_TILES
    wspec = lambda k, n: pl.BlockSpec((1, k, n), lambda j, be, nu: (be[j], 0, 0))
    row_spec = pl.BlockSpec((buf_rows, LANES), lambda j, be, nu: (j, 0))
    grid_spec = pltpu.PrefetchScalarGridSpec(
        num_scalar_prefetch=2,
        grid=(nb,),
        in_specs=[row_spec, wspec(d, f), wspec(d, f), wspec(f, d), wspec(1, f), wspec(1, f), wspec(1, d)],
        out_specs=row_spec)
    return pl.pallas_call(
        _expert_kernel,
        grid_spec=grid_spec,
        out_shape=jax.ShapeDtypeStruct(x_sorted.shape, F32),
        compiler_params=_params("parallel"),
        name="moe_experts",
    )(block_expert, blocks_used, x_sorted, w1g, w1l, w2, b1g, b1l, b2)


def _dispatch_kernel(cnt_ref, start_ref, dst_ref, h_ref, xs_hbm, rowbuf, zero_ref, sem, zsem, *, nt, nslots):
    i = pl.program_id(0)
    tb = h_ref.shape[0]
    slot = i % 2

    def landed(buf):
        for _ in range(TOP_K):
            pltpu.make_async_copy(rowbuf.at[buf, pl.ds(0, tb * ROW_TILES), :],
                                  xs_hbm.at[pl.ds(0, tb * ROW_TILES), :], sem.at[buf]).wait()

    @pl.when(i >= 2)
    def _():
        landed(slot)

    _store_row_tiles(rowbuf.at[slot], h_ref[...].astype(F32), pitch=ROW_PITCH)
    for r in range(tb):
        for k in range(TOP_K):
            dst = pl.multiple_of(dst_ref[0, k, r] * ROW_TILES, ROW_TILES)
            pltpu.make_async_copy(rowbuf.at[slot, pl.ds(r * ROW_PITCH, ROW_TILES), :],
                                  xs_hbm.at[pl.ds(dst, ROW_TILES), :], sem.at[slot]).start(priority=k % 2)

    @pl.when(i == nt - 1)
    def _():
        landed(slot)
        if nt > 1:
            landed(1 - slot)
        zero_ref[...] = jnp.zeros_like(zero_ref)

        zrows = zero_ref.shape[0] // ROW_TILES

        def zero_rows(pos, n):
            cp = pltpu.make_async_copy(
                zero_ref.at[pl.ds(0, n * ROW_TILES), :],
                xs_hbm.at[pl.ds(pl.multiple_of(pos * ROW_TILES, ROW_TILES), n * ROW_TILES), :], zsem)
            cp.start()
            cp.wait()

        end = 0
        for e in range(N_EXPERTS):
            pos = start_ref[e] + cnt_ref[e]
            end = start_ref[e] + (cnt_ref[e] + MOE_ROWS - 1) // MOE_ROWS * MOE_ROWS
            gap = end - pos
            bit = zrows
            while bit >= 1:
                take = (gap & bit) != 0
                pl.when(take)(functools.partial(zero_rows, pos, bit))
                pos = pos + jnp.where(take, bit, 0)
                bit //= 2

        def tail(i, carry):
            zero_rows(end + i * zrows, zrows)
            return carry
        lax.fori_loop(0, (nslots - end) // zrows, tail, 0)


def _dispatch(h2, dest3, counts, pad_starts, nslots):
    ntok, d = h2.shape
    tb = COMBINE_TILE
    nt = ntok // tb
    assert MOE_ROWS & (MOE_ROWS - 1) == 0, "the padding fill decomposes gap lengths into powers of two"
    grid_spec = pltpu.PrefetchScalarGridSpec(
        num_scalar_prefetch=2,
        grid=(nt,),
        in_specs=[pl.BlockSpec((1, TOP_K, tb), lambda i, c, s: (i, 0, 0), memory_space=pltpu.SMEM),
                  pl.BlockSpec((tb, d), lambda i, c, s: (i, 0))],
        out_specs=pl.BlockSpec(memory_space=pl.ANY),
        scratch_shapes=[pltpu.VMEM((2, tb * ROW_PITCH, LANES), F32),
                        pltpu.VMEM((MOE_ROWS // 2 * ROW_TILES, LANES), F32),
                        pltpu.SemaphoreType.DMA((2,)), pltpu.SemaphoreType.DMA(())])
    return pl.pallas_call(
        functools.partial(_dispatch_kernel, nt=nt, nslots=nslots),
        grid_spec=grid_spec,
        out_shape=jax.ShapeDtypeStruct((nslots * ROW_TILES, LANES), F32),
        compiler_params=_params("arbitrary"),
        name="moe_dispatch",
    )(counts, pad_starts, dest3, h2)


def _combine_kernel(dst_ref, dstn_ref, y_hbm, x1_ref, gate_ref, g2_ref, lnw_ref, lnb_ref, o_ref, ybuf, sem, *, nt):
    i = pl.program_id(0)
    tb = x1_ref.shape[0]
    slot = i % 2

    def row_copy(src, buf, k, r):
        rows = y_hbm.at[pl.ds(pl.multiple_of(src * ROW_TILES, ROW_TILES), ROW_TILES), :]
        return pltpu.make_async_copy(rows, ybuf.at[buf, k, pl.ds(r * ROW_PITCH, ROW_TILES), :], sem.at[buf])

    def gather(idx_ref, buf):
        for r in range(tb):
            for k in range(TOP_K):
                row_copy(idx_ref[0, k, r], buf, k, r).start(priority=k % 2)

    @pl.when(i == 0)
    def _():
        gather(dst_ref, 0)

    @pl.when(i + 1 < nt)
    def _():
        gather(dstn_ref, 1 - slot)

    for k in range(TOP_K):
        pltpu.make_async_copy(y_hbm.at[pl.ds(0, tb * ROW_TILES), :],
                              ybuf.at[slot, k, pl.ds(0, tb * ROW_TILES), :], sem.at[slot]).wait()

    y = sum(gate_ref[:, k:k + 1] * _load_row_tiles(ybuf.at[slot, k], tb, pitch=ROW_PITCH) for k in range(TOP_K))
    r = DN_ALPHA * x1_ref[...] + g2_ref[0] * y
    mu = jnp.mean(r, axis=-1, keepdims=True)
    rc = r - mu
    var = jnp.mean(rc * rc, axis=-1, keepdims=True)
    o_ref[...] = rc * lax.rsqrt(var + EPS) * lnw_ref[...] + lnb_ref[...]


def _combine(x1, y_sorted, dest3, gates_t, tok_off, g2, mod_row, lnw, lnb):
    bn, t, d = x1.shape
    n = bn * t
    tb = COMBINE_TILE
    nt = n // tb
    off = tok_off // tb
    per_b = t // tb
    out = pl.pallas_call(
        functools.partial(_combine_kernel, nt=nt),
        grid=(nt,),
        in_specs=[pl.BlockSpec((1, TOP_K, tb), lambda i: (off + i, 0, 0), memory_space=pltpu.SMEM),
                  pl.BlockSpec((1, TOP_K, tb), lambda i: (off + jnp.minimum(i + 1, nt - 1), 0, 0),
                               memory_space=pltpu.SMEM),
                  pl.BlockSpec(memory_space=pl.ANY),
                  pl.BlockSpec((tb, d), lambda i: (i, 0)),
                  pl.BlockSpec((tb, TOP_K), lambda i: (off + i, 0)),
                  pl.BlockSpec((1, 1, d), lambda i: (mod_row(i // per_b), 0, 0)),
                  pl.BlockSpec((1, d), lambda i: (0, 0)),
                  pl.BlockSpec((1, d), lambda i: (0, 0))],
        out_specs=pl.BlockSpec((tb, d), lambda i: (i, 0)),
        out_shape=jax.ShapeDtypeStruct((n, d), F32),
        scratch_shapes=[pltpu.VMEM((2, TOP_K, tb * ROW_PITCH, LANES), F32), pltpu.SemaphoreType.DMA((2,))],
        compiler_params=_params("arbitrary"),
        name="moe_combine",
    )(dest3, dest3, y_sorted, x1.reshape(n, d), gates_t, g2, lnw.reshape(1, d), lnb.reshape(1, d))
    return out.reshape(bn, t, d)


def _dest_kernel(ti_ref, rank_ref, start_ref, dest_ref):
    ti = ti_ref[...]
    tb = ti.shape[1]
    eio = lax.broadcasted_iota(jnp.int32, (N_EXPERTS, tb), 0)
    start = start_ref[:, 0:1].astype(F32)
    base = jnp.concatenate(
        [jnp.sum(jnp.where(eio == ti[k:k + 1, :], start, 0.0), axis=0, keepdims=True) for k in range(TOP_K)], axis=0)
    dest = base.astype(jnp.int32) + rank_ref[...]
    for c in range(tb // COMBINE_TILE):
        dest_ref[c] = dest[:, c * COMBINE_TILE:(c + 1) * COMBINE_TILE]


def _moe_layout(top_i):
    ntok = top_i.shape[1]
    m = ntok * TOP_K
    rank, counts = _ranks(top_i)
    padded = (counts + MOE_ROWS - 1) // MOE_ROWS * MOE_ROWS
    pad_ends = jnp.cumsum(padded)
    pad_starts = pad_ends - padded
    tb = ROUTE_TILE if ntok % ROUTE_TILE == 0 else 256
    ntiles = ntok // COMBINE_TILE
    dest = pl.pallas_call(
        _dest_kernel,
        grid=(ntok // tb,),
        in_specs=[pl.BlockSpec((TOP_K, tb), lambda i: (0, i)),
                  pl.BlockSpec((TOP_K, tb), lambda i: (0, i)),
                  pl.BlockSpec((N_EXPERTS, LANES), lambda i: (0, 0))],
        out_specs=pl.BlockSpec((tb // COMBINE_TILE, TOP_K, COMBINE_TILE), lambda i: (i, 0, 0)),
        out_shape=jax.ShapeDtypeStruct((ntiles, TOP_K, COMBINE_TILE), jnp.int32),
        compiler_params=_params("parallel"),
        name="route_dest",
    )(top_i, rank, jnp.broadcast_to(pad_starts[:, None], (N_EXPERTS, LANES)))
    nb = (m + N_EXPERTS * (MOE_ROWS - 1) + MOE_ROWS - 1) // MOE_ROWS
    block_starts = jnp.arange(nb, dtype=jnp.int32) * MOE_ROWS
    block_expert = jnp.minimum(jnp.sum(block_starts[:, None] >= pad_ends[None, :], axis=1),
                               N_EXPERTS - 1).astype(jnp.int32)
    blocks_used = (pad_ends[-1:] // MOE_ROWS).astype(jnp.int32)
    return dest, counts, pad_starts.astype(jnp.int32), block_expert, blocks_used, nb * MOE_ROWS


def kernel(x, c, ctx, c_ctx, lower_bounds_fwd, lower_bounds_bwd, ada_w, ada_b, w_in_even, a_norm_w, conv_w,
           w_in_odd, c_norm_w, d_norm_w, d_norm_b, spatial_w, spatial_b, w_out, ln_mix_w, ln_mix_b,
           ln_ffn_w, ln_ffn_b, router_w, router_b, moe_w1, moe_b1, moe_w2, moe_b2):
    bn, t, d = x.shape
    tc = ctx.shape[1]
    n_lat = bn * t
    n_ctx = bn * tc
    ctx_row = bn
    lat_row = lambda b: b
    ctx_mod_row = lambda b: ctx_row

    lb_f_all = jnp.cumsum(jax.nn.softmax(lower_bounds_fwd.astype(F32), axis=0), axis=0)
    lb_b_all = jnp.cumsum(jax.nn.softmax(lower_bounds_bwd.astype(F32), axis=0), axis=0)

    cond = jnp.zeros((MOD_ROWS, d), F32).at[:bn].set(c).at[ctx_row].set(c_ctx)
    mods = _ada(cond, ada_w, ada_b).reshape(DEPTH, MOD_ROWS, 6, 1, d)

    h_ctx = ctx
    for l in range(DEPTH):
        last = l == DEPTH - 1
        sh1, sc1, g1, sh2, sc2, g2 = (mods[l, :, j] for j in range(6))
        wo = w_out[l].astype(BF16)
        rw_t = router_w[l].T.astype(BF16)
        ntok = n_lat if last else n_lat + n_ctx
        tok_state = (jnp.zeros((ntok, d), BF16), jnp.zeros((TOP_K, ntok), jnp.int32),
                     jnp.zeros((TOP_K, ntok), F32))
        post = dict(wo=wo, lnw=ln_mix_w[l], lnb=ln_mix_b[l], rw_t=rw_t, rb=router_b[l])
        bm_ctx = min(ROW_TILE, tc)
        if l % 2 == 0:
            e = l // 2
            w_in = w_in_even[e].astype(BF16)
            zin_c = _inproj(h_ctx, sc1, sh1, w_in, ctx_mod_row, bm=bm_ctx)
            zin = _inproj(x, sc1, sh1, w_in, lat_row, bm=ROW_TILE)
            zeros = jnp.zeros((bn, A_HEADS, A_DV, A_DK), F32)
            oc_f, sc_f = _scan(zin_c, lb_f_all[l], zeros, reverse=False)
            a_ctx, sc_b = _scan(zin_c, lb_b_all[l], zeros, reverse=True, o_prev=oc_f, norm_w=a_norm_w[e])
            o_f, _ = _scan(zin, lb_f_all[l], sc_f, reverse=False)
            a_lat, _ = _scan(zin, lb_b_all[l], sc_b, reverse=True, o_prev=o_f, norm_w=a_norm_w[e])
            x1, *tok_state = _outproj_even(a_lat, zin, conv_w[e], x=x, mods=(g1, sc2, sh2), mod_row=lat_row,
                                           tok_state=tok_state, tok_off=0, bm=ROW_TILE, **post)
            if not last:
                hc1, *tok_state = _outproj_even(a_ctx, zin_c, conv_w[e], x=h_ctx, mods=(g1, sc2, sh2),
                                                mod_row=ctx_mod_row, tok_state=tok_state, tok_off=n_lat,
                                                bm=bm_ctx, **post)
        else:
            o = l // 2
            w_in = w_in_odd[o].astype(BF16)
            odd = dict(d_norm_w=d_norm_w[o], d_norm_b=d_norm_b[o], sp_w=spatial_w[o], sp_b=spatial_b[o])
            zin = _inproj(x, sc1, sh1, w_in, lat_row, bm=ROW_TILE)
            c_mix = _fourier_mix(zin, c_norm_w[o])
            x1, *tok_state = _outproj_odd(c_mix, zin, x=x, mods=(g1, sc2, sh2), mod_row=lat_row,
                                          tok_state=tok_state, tok_off=0, bm=ROW_TILE, **odd, **post)
            if not last:
                raise NotImplementedError("an odd layer that is not the last needs the context Fourier mixer")
        h2_all, top_i, top_g = tok_state
        dest, counts, pad_starts, block_expert, blocks_used, nslots = _moe_layout(top_i)
        x_sorted = _dispatch(h2_all, dest, counts, pad_starts, nslots)
        w1g, w1l = _w1_split(moe_w1, l)
        y_sorted = _experts(
            x_sorted, block_expert, blocks_used, w1g, w1l, _w2_cast(moe_w2, l),
            moe_b1[l][:, None, 0::2], moe_b1[l][:, None, 1::2], moe_b2[l][:, None, :])
        gates_t = top_g.T
        x = _combine(x1, y_sorted, dest, gates_t, 0, g2, lat_row, ln_ffn_w[l], ln_ffn_b[l])
        if not last:
            h_ctx = _combine(hc1, y_sorted, dest, gates_t, n_lat, g2, ctx_mod_row, ln_ffn_w[l], ln_ffn_b[l])
    return x
```

```python
import functools
import math

import numpy as np
import jax
import jax.numpy as jnp
from jax import lax
from jax.experimental import pallas as pl
from jax.experimental.pallas import tpu as pltpu

F32 = jnp.float32
BF16 = jnp.bfloat16

D_MODEL = 1024
DEPTH = 2
GRID_W = 64
A_HEADS = 4
A_DK = 128
A_DV = 128
A_KW = A_HEADS * A_DK
A_WIDTH = A_HEADS * A_DV
B_WIDTH = D_MODEL - A_WIDTH
CONV_W = 3
C_GROUPS = 4
C_GROUP = 128
C_WIDTH = C_GROUPS * C_GROUP
D_WIDTH = D_MODEL - C_WIDTH
D_GROUPS = 4
D_GDIM = D_WIDTH // D_GROUPS
D_CHUNK = 128
SCAN_CHUNK = 32
EVEN_IN = 3 * A_KW + 2 * A_WIDTH + 3 * B_WIDTH
ODD_IN = C_WIDTH + 2 * D_WIDTH
N_EXPERTS = 32
TOP_K = 4
D_EXPERT = 1024
SWIGLU_ALPHA = 1.702
SWIGLU_LIMIT = 7.0
DN_ALPHA = (2 * DEPTH) ** 0.25
EPS = 1e-5

MOD_ROWS = 8
SCAN_ROWS = 256
SCAN_HEADS = 4
SCAN_SPAN = 256
ROW_TILE = 512
SUB_TILE = 256
MOE_ROWS = 512
ROUTE_TILE = 512
COMBINE_TILE = 512
HALO = 16
LANES = 128
ROW_TILES = D_MODEL // LANES
ROW_PITCH = ROW_TILES + 1

_NT = (((1,), (1,)), ((), ()))
_TN = (((0,), (0,)), ((), ()))


def _sigmoid(a):
    return 1.0 / (1.0 + jnp.exp(-a))


def _silu(a):
    return a * _sigmoid(a)


def _gelu(a):
    return 0.5 * a * (1.0 + lax.erf(a * (1.0 / math.sqrt(2.0))))


def _store_row_tiles(ref, val, pitch=ROW_TILES):
    rows = val.shape[0]
    for c in range(ROW_TILES):
        ref[pl.ds(c, rows, stride=pitch), :] = val[:, c * LANES:(c + 1) * LANES]


def _load_row_tiles(ref, rows, pitch=ROW_TILES):
    return jnp.concatenate([ref[pl.ds(c, rows, stride=pitch), :] for c in range(ROW_TILES)], axis=1)


V7X_VMEM_BYTES = 64 * 1024 * 1024
VMEM_LIMIT_BYTES = V7X_VMEM_BYTES - 8 * 1024 * 1024


def _params(*sem):
    return pltpu.CompilerParams(dimension_semantics=sem, vmem_limit_bytes=VMEM_LIMIT_BYTES)


def _ada_kernel(c_ref, w_ref, b_ref, o_ref):
    s = _silu(c_ref[...])
    o_ref[0] = jnp.dot(s, w_ref[0], preferred_element_type=F32, precision=lax.Precision.HIGHEST) + b_ref[0]


def _ada(cond, ada_w, ada_b):
    d = cond.shape[1]
    n6 = ada_w.shape[2]
    tn = 1536
    return pl.pallas_call(
        _ada_kernel,
        grid=(DEPTH, n6 // tn),
        in_specs=[pl.BlockSpec((MOD_ROWS, d), lambda l, n: (0, 0)),
                  pl.BlockSpec((1, d, tn), lambda l, n: (l, 0, n)),
                  pl.BlockSpec((1, 1, tn), lambda l, n: (l, 0, n))],
        out_specs=pl.BlockSpec((1, MOD_ROWS, tn), lambda l, n: (l, 0, n)),
        out_shape=jax.ShapeDtypeStruct((DEPTH, MOD_ROWS, n6), F32),
        compiler_params=_params("parallel", "parallel"),
        name="ada_mod",
    )(cond, ada_w, ada_b.reshape(DEPTH, 1, n6))


def _inproj_kernel(x_ref, sc_ref, sh_ref, w_ref, o_ref, *, tn):
    h = (x_ref[0] * (1.0 + sc_ref[0]) + sh_ref[0]).astype(BF16)
    for n in range(o_ref.shape[2] // tn):
        o_ref[0, :, n * tn:(n + 1) * tn] = jnp.dot(
            h, w_ref[:, n * tn:(n + 1) * tn], preferred_element_type=F32).astype(o_ref.dtype)


def _inproj(x, sc, sh, w, mod_row, bm):
    bn, t, d = x.shape
    nout = w.shape[1]
    mod_spec = pl.BlockSpec((1, 1, d), lambda b, i: (mod_row(b), 0, 0))
    return pl.pallas_call(
        functools.partial(_inproj_kernel, tn=512),
        grid=(bn, t // bm),
        in_specs=[pl.BlockSpec((1, bm, d), lambda b, i: (b, i, 0)), mod_spec, mod_spec,
                  pl.BlockSpec((d, nout), lambda b, i: (0, 0))],
        out_specs=pl.BlockSpec((1, bm, nout), lambda b, i: (b, i, 0)),
        out_shape=jax.ShapeDtypeStruct((bn, t, nout), BF16),
        compiler_params=_params("parallel", "parallel"),
        name="in_proj",
    )(x, sc, sh, w)


def _scan_kernel(*refs, reverse, finalize):
    if finalize:
        (q_ref, z_ref, v_ref, lb_ref, s0_ref, g_ref, op_ref, nw_ref, o_ref, sfin_ref, st_ref, oacc_ref) = refs
    else:
        (q_ref, z_ref, v_ref, lb_ref, s0_ref, o_ref, sfin_ref, st_ref, oacc_ref) = refs
    rows = q_ref.shape[1]
    heads = q_ref.shape[2] // A_DK
    nchunk = rows // SCAN_CHUNK

    @pl.when(pl.program_id(2) == 0)
    def _():
        st_ref[...] = s0_ref[0]

    q = _silu(q_ref[0].astype(F32))
    lb = lb_ref[...]
    f = lb + (1.0 - lb) * _sigmoid(z_ref[0].astype(F32))
    logf = jnp.log(f)
    k = 1.0 - f
    v = v_ref[0]

    ri = lax.broadcasted_iota(jnp.int32, (rows, rows), 0)
    ci = lax.broadcasted_iota(jnp.int32, (rows, rows), 1)
    same = (ri // SCAN_CHUNK) == (ci // SCAN_CHUNK)
    causal = same & ((ci >= ri) if reverse else (ci <= ri))
    tri = jnp.where(causal, 1.0, 0.0).astype(BF16)
    blk = jnp.where(same, 1.0, 0.0).astype(BF16)
    hi = logf.astype(BF16)
    lo = (logf - hi.astype(F32)).astype(BF16)
    b = jnp.dot(tri, hi, preferred_element_type=F32) + jnp.dot(tri, lo, preferred_element_type=F32)
    b_tot = jnp.dot(blk, hi, preferred_element_type=F32) + jnp.dot(blk, lo, preferred_element_type=F32)

    q_dec = (q * jnp.exp(b)).astype(BF16)
    k_inv = (k * jnp.exp(-b)).astype(BF16)
    k_end = (k * jnp.exp(b_tot - b)).astype(BF16)
    dec = jnp.exp(b_tot)

    span = min(SCAN_SPAN, rows)
    per = span // SCAN_CHUNK
    wide = per * A_DK
    spread = (lax.broadcasted_iota(jnp.int32, (span, wide), 0) // SCAN_CHUNK
              == lax.broadcasted_iota(jnp.int32, (span, wide), 1) // A_DK)
    causal_span = causal[0:span, 0:span]
    spans = range(rows // span - 1, -1, -1) if reverse else range(rows // span)
    order = range(per - 1, -1, -1) if reverse else range(per)
    for h in range(heads):
        hk = slice(h * A_DK, (h + 1) * A_DK)
        hv = slice(h * A_DV, (h + 1) * A_DV)
        st = st_ref[h]
        for sp in spans:
            rs = slice(sp * span, (sp + 1) * span)
            qd, vh = q_dec[rs, hk], v[rs, hv]
            scores = lax.dot_general(qd, k_inv[rs, hk], _NT, preferred_element_type=F32)
            scores = jnp.where(causal_span, scores, 0.0).astype(BF16)
            o_intra = jnp.dot(scores, vh, preferred_element_type=F32)
            ke_wide = jnp.where(spread, jnp.concatenate([k_end[rs, hk]] * per, axis=1), 0.0).astype(BF16)
            upd = lax.dot_general(vh, ke_wide, _TN, preferred_element_type=F32)
            seen = [None] * per
            for c in order:
                seen[c] = st.astype(BF16)
                r0 = sp * span + c * SCAN_CHUNK
                st = st * dec[r0:r0 + 1, hk] + upd[:, c * A_DK:(c + 1) * A_DK]
            qd_wide = jnp.where(spread, jnp.concatenate([qd] * per, axis=1), 0.0).astype(BF16)
            o_inter = lax.dot_general(qd_wide, jnp.concatenate(seen, axis=1), _NT, preferred_element_type=F32)
            oacc_ref[rs, hv] = o_intra + o_inter
        st_ref[h] = st
        sfin_ref[0, h] = st

    o = oacc_ref[...]
    if finalize:
        o = o + op_ref[0]
        gate = nw_ref[...] * _silu(g_ref[0].astype(F32))
        for h in range(heads):
            hv = slice(h * A_DV, (h + 1) * A_DV)
            oh = o[:, hv]
            oh = oh * lax.rsqrt(jnp.mean(oh * oh, axis=-1, keepdims=True) + EPS)
            o_ref[0, :, hv] = (oh * gate[:, hv]).astype(o_ref.dtype)
    else:
        o_ref[0] = o.astype(o_ref.dtype)


def _scan(zin, lb, s0, *, reverse, o_prev=None, norm_w=None):
    bn, t, _ = zin.shape
    finalize = o_prev is not None
    rows = min(SCAN_ROWS, t)
    nb = t // rows
    pos = (lambda n: nb - 1 - n) if reverse else (lambda n: n)
    hp = SCAN_HEADS
    groups = A_HEADS // hp
    hw = hp * A_DK

    def col(section):
        return pl.BlockSpec((1, rows, hw), lambda b, h, n: (b, pos(n), section * groups + h))

    state_spec = pl.BlockSpec((1, hp, A_DV, A_DK), lambda b, h, n: (b, h, 0, 0))
    in_specs = [col(0), col(2 if reverse else 1), col(3),
                pl.BlockSpec((1, hw), lambda b, h, n: (0, h)), state_spec]
    args = [zin, zin, zin, lb.reshape(1, A_KW), s0]
    if finalize:
        in_specs += [col(4),
                     pl.BlockSpec((1, rows, hw), lambda b, h, n: (b, pos(n), h)),
                     pl.BlockSpec((1, hw), lambda b, h, n: (0, h))]
        args += [zin, o_prev, norm_w.reshape(1, A_WIDTH)]
    return pl.pallas_call(
        functools.partial(_scan_kernel, reverse=reverse, finalize=finalize),
        grid=(bn, groups, nb),
        in_specs=in_specs,
        out_specs=[pl.BlockSpec((1, rows, hw), lambda b, h, n: (b, pos(n), h)), state_spec],
        out_shape=[jax.ShapeDtypeStruct((bn, t, A_WIDTH), BF16 if finalize else F32),
                   jax.ShapeDtypeStruct((bn, A_HEADS, A_DV, A_DK), F32)],
        scratch_shapes=[pltpu.VMEM((hp, A_DV, A_DK), F32), pltpu.VMEM((rows, hw), F32)],
        compiler_params=_params("parallel", "parallel", "arbitrary"),
        name="hgrn2_scan_bwd" if reverse else "hgrn2_scan_fwd",
    )(*args)


def _sub_tiles(bm):
    sub = min(bm, SUB_TILE)
    return [slice(s * sub, (s + 1) * sub) for s in range(bm // sub)]


def _post_mix(acc, rs, x_ref, g1_ref, lnw_ref, lnb_ref, sc2_ref, sh2_ref, rw_ref, rb_ref,
              x1_ref, h2_ref, ti_ref, tg_ref):
    r = DN_ALPHA * x_ref[0, rs, :] + g1_ref[0] * acc
    mu = jnp.mean(r, axis=-1, keepdims=True)
    rc = r - mu
    var = jnp.mean(rc * rc, axis=-1, keepdims=True)
    x1 = rc * lax.rsqrt(var + EPS) * lnw_ref[...] + lnb_ref[...]
    x1_ref[0, rs, :] = x1
    h2 = x1 * (1.0 + sc2_ref[0]) + sh2_ref[0]
    h2_ref[rs, :] = h2.astype(h2_ref.dtype)
    logits = lax.dot_general(rw_ref[...], h2.astype(BF16), _NT, preferred_element_type=F32) + rb_ref[...]
    iota = lax.broadcasted_iota(jnp.int32, logits.shape, 0)
    vals, idxs = [], []
    cur = logits
    for _ in range(TOP_K):
        m = jnp.max(cur, axis=0, keepdims=True)
        ik = jnp.min(jnp.where(cur == m, iota, N_EXPERTS), axis=0, keepdims=True)
        vals.append(m)
        idxs.append(ik)
        cur = jnp.where(iota == ik, -jnp.inf, cur)
    tv = jnp.concatenate(vals, axis=0)
    e = jnp.exp(tv - tv[0:1])
    tg_ref[:, rs] = e / jnp.sum(e, axis=0, keepdims=True)
    ti_ref[:, rs] = jnp.concatenate(idxs, axis=0)


def _outproj_even_kernel(a_ref, u_ref, gb_ref, gc_ref, up_ref, gcp_ref, un_ref, gcn_ref, cw_ref, wo_ref,
                         x_ref, g1_ref, lnw_ref, lnb_ref, sc2_ref, sh2_ref, rw_ref, rb_ref, alias_ref,
                         x1_ref, h2_ref, ti_ref, tg_ref):
    del alias_ref
    i = pl.program_id(1)
    bm = u_ref.shape[1]
    z = gc_ref[0].astype(F32) * u_ref[0].astype(F32)
    z_before = gcp_ref[0, HALO - 1:HALO, :].astype(F32) * up_ref[0, HALO - 1:HALO, :].astype(F32)
    z_after = gcn_ref[0, 0:1, :].astype(F32) * un_ref[0, 0:1, :].astype(F32)
    z_before = jnp.where(i == 0, 0.0, z_before)
    z_after = jnp.where(i == pl.num_programs(1) - 1, 0.0, z_after)
    row = lax.broadcasted_iota(jnp.int32, z.shape, 0)
    z_prev = jnp.where(row == 0, z_before, pltpu.roll(z, 1, axis=0))
    z_next = jnp.where(row == bm - 1, z_after, pltpu.roll(z, bm - 1, axis=0))
    zc = cw_ref[0:1, :] * z_prev + cw_ref[1:2, :] * z + cw_ref[2:3, :] * z_next
    b_mix = (gb_ref[0].astype(F32) * zc).astype(BF16)
    for rs in _sub_tiles(bm):
        acc = (jnp.dot(a_ref[0, rs, :], wo_ref[0:A_WIDTH, :], preferred_element_type=F32)
               + jnp.dot(b_mix[rs], wo_ref[A_WIDTH:, :], preferred_element_type=F32))
        _post_mix(acc, rs, x_ref, g1_ref, lnw_ref, lnb_ref, sc2_ref, sh2_ref, rw_ref, rb_ref,
                  x1_ref, h2_ref, ti_ref, tg_ref)


def _outproj_odd_kernel(c_ref, u_ref, v_ref, dw_ref, db_ref, spw_ref, spb_ref, wo_ref,
                        x_ref, g1_ref, lnw_ref, lnb_ref, sc2_ref, sh2_ref, rw_ref, rb_ref, alias_ref,
                        x1_ref, h2_ref, ti_ref, tg_ref):
    del alias_ref
    bm = u_ref.shape[1]
    u = _gelu(u_ref[0].astype(F32))
    v = _gelu(v_ref[0].astype(F32))
    cols = []
    for g in range(D_GROUPS):
        lo_c, hi_c = g * D_GDIM, (g + 1) * D_GDIM
        vg = v[:, lo_c:hi_c]
        mu = jnp.mean(vg, axis=-1, keepdims=True)
        vc = vg - mu
        var = jnp.mean(vc * vc, axis=-1, keepdims=True)
        vn = (vc * lax.rsqrt(var + EPS) * dw_ref[:, lo_c:hi_c] + db_ref[:, lo_c:hi_c]).astype(BF16)
        parts = []
        for ch in range(bm // D_CHUNK):
            sv = jnp.dot(spw_ref[g], vn[ch * D_CHUNK:(ch + 1) * D_CHUNK], preferred_element_type=F32)
            parts.append(sv + spb_ref[:, g:g + 1])
        cols.append(u[:, lo_c:hi_c] * jnp.concatenate(parts, axis=0))
    d_mix = jnp.concatenate(cols, axis=1).astype(BF16)
    for rs in _sub_tiles(bm):
        acc = (jnp.dot(c_ref[0, rs, :].astype(BF16), wo_ref[0:C_WIDTH, :], preferred_element_type=F32)
               + jnp.dot(d_mix[rs], wo_ref[C_WIDTH:, :], preferred_element_type=F32))
        _post_mix(acc, rs, x_ref, g1_ref, lnw_ref, lnb_ref, sc2_ref, sh2_ref, rw_ref, rb_ref,
                  x1_ref, h2_ref, ti_ref, tg_ref)


def _outproj(kind, mixer_args, mixer_specs, wo, x, mods, mod_row, lnw, lnb, rw_t, rb, tok_state, tok_off, bm):
    bn, t, d = x.shape
    nt = t // bm
    g1, sc2, sh2 = mods
    h2_all, ti_all, tg_all = tok_state
    ntok = h2_all.shape[0]
    off = tok_off // bm
    mod_spec = pl.BlockSpec((1, 1, d), lambda b, i: (mod_row(b), 0, 0))
    vec_spec = pl.BlockSpec((1, d), lambda b, i: (0, 0))
    in_specs = list(mixer_specs) + [
        pl.BlockSpec((d, d), lambda b, i: (0, 0)),
        pl.BlockSpec((1, bm, d), lambda b, i: (b, i, 0)),
        mod_spec, vec_spec, vec_spec, mod_spec, mod_spec,
        pl.BlockSpec((N_EXPERTS, d), lambda b, i: (0, 0)),
        pl.BlockSpec((N_EXPERTS, 1), lambda b, i: (0, 0)),
        pl.BlockSpec(memory_space=pl.ANY), pl.BlockSpec(memory_space=pl.ANY), pl.BlockSpec(memory_space=pl.ANY)]
    n_in = len(in_specs)
    kernel = _outproj_even_kernel if kind == "even" else _outproj_odd_kernel

    def body(*refs):
        ins, outs = refs[:n_in], refs[n_in:]
        kernel(*ins[:n_in - 3], ins[n_in - 3:], *outs)

    return pl.pallas_call(
        body,
        grid=(bn, nt),
        in_specs=in_specs,
        out_specs=[pl.BlockSpec((1, bm, d), lambda b, i: (b, i, 0)),
                   pl.BlockSpec((bm, d), lambda b, i: (off + b * nt + i, 0)),
                   pl.BlockSpec((TOP_K, bm), lambda b, i: (0, off + b * nt + i)),
                   pl.BlockSpec((TOP_K, bm), lambda b, i: (0, off + b * nt + i))],
        out_shape=[jax.ShapeDtypeStruct((bn, t, d), F32),
                   jax.ShapeDtypeStruct((ntok, d), BF16),
                   jax.ShapeDtypeStruct((TOP_K, ntok), jnp.int32),
                   jax.ShapeDtypeStruct((TOP_K, ntok), F32)],
        input_output_aliases={n_in - 3: 1, n_in - 2: 2, n_in - 1: 3},
        compiler_params=_params("parallel", "parallel"),
        name="out_proj_" + kind,
    )(*mixer_args, wo, x, g1, lnw.reshape(1, d), lnb.reshape(1, d), sc2, sh2, rw_t, rb.reshape(N_EXPERTS, 1),
      h2_all, ti_all, tg_all)


def _outproj_even(a_mix, zin, conv_w, **kw):
    bm = kw["bm"]
    t = zin.shape[1]
    hb = bm // HALO
    last = t // HALO - 1
    cur = lambda sec: pl.BlockSpec((1, bm, B_WIDTH), lambda b, i: (b, i, sec))
    before = lambda sec: pl.BlockSpec((1, HALO, B_WIDTH), lambda b, i: (b, jnp.maximum(i * hb - 1, 0), sec))
    after = lambda sec: pl.BlockSpec((1, HALO, B_WIDTH), lambda b, i: (b, jnp.minimum((i + 1) * hb, last), sec))
    specs = [pl.BlockSpec((1, bm, A_WIDTH), lambda b, i: (b, i, 0)), cur(5), cur(6), cur(7),
             before(5), before(7), after(5), after(7),
             pl.BlockSpec((CONV_W, B_WIDTH), lambda b, i: (0, 0))]
    return _outproj("even", [a_mix, zin, zin, zin, zin, zin, zin, zin, conv_w], specs, **kw)


def _outproj_odd(c_mix, zin, d_norm_w, d_norm_b, sp_w, sp_b, **kw):
    bm = kw["bm"]
    cur = lambda sec: pl.BlockSpec((1, bm, D_WIDTH), lambda b, i: (b, i, sec))
    specs = [pl.BlockSpec((1, bm, C_WIDTH), lambda b, i: (b, i, 0)), cur(1), cur(2),
             pl.BlockSpec((1, D_WIDTH), lambda b, i: (0, 0)), pl.BlockSpec((1, D_WIDTH), lambda b, i: (0, 0)),
             pl.BlockSpec((D_GROUPS, D_CHUNK, D_CHUNK), lambda b, i: (0, 0, 0)),
             pl.BlockSpec((D_CHUNK, D_GROUPS), lambda b, i: (0, 0))]
    return _outproj("odd", [c_mix, zin, zin, d_norm_w.reshape(1, D_WIDTH), d_norm_b.reshape(1, D_WIDTH),
                            sp_w.astype(BF16), sp_b.T], specs, **kw)


def _dft_mats(n):
    ang = 2.0 * np.pi * np.outer(np.arange(n), np.arange(n)) / n
    return np.cos(ang), np.sin(ang)


def _fourier_cw_kernel(z_ref, nw_ref, cs_ref, kc_ref, ks_ref, p_ref, q_ref, ps_ref, qs_ref):
    tile = z_ref.shape[1]
    half = 256
    grows = tile // GRID_W
    pitch = GRID_W + 1
    for hs in range(tile // half):
        z = z_ref[0, hs * half:(hs + 1) * half, :].astype(F32)
        a_parts, b_parts = [], []
        for g in range(C_GROUPS):
            zg = z[:, g * C_GROUP:(g + 1) * C_GROUP]
            zg = zg * lax.rsqrt(jnp.mean(zg * zg, axis=-1, keepdims=True) + EPS) * nw_ref[:, g * C_GROUP:(g + 1) * C_GROUP]
            ab = jnp.dot(zg.astype(BF16), cs_ref[...], preferred_element_type=F32)
            a_parts.append(ab[:, :C_GROUP])
            b_parts.append(ab[:, C_GROUP:])
        ab = jnp.concatenate(a_parts + b_parts, axis=1).astype(BF16)
        m1 = jnp.dot(kc_ref[...], ab, preferred_element_type=F32)
        m2 = jnp.dot(ks_ref[...], ab, preferred_element_type=F32)
        pv = m1[:, :C_WIDTH] - m2[:, C_WIDTH:]
        qv = m2[:, :C_WIDTH] + m1[:, C_WIDTH:]
        for r in range(half // GRID_W):
            at = (hs * (half // GRID_W) + r) * pitch
            for g in range(C_GROUPS):
                ps_ref[g, at:at + GRID_W, :] = pv[r * GRID_W:(r + 1) * GRID_W, g * C_GROUP:(g + 1) * C_GROUP]
                qs_ref[g, at:at + GRID_W, :] = qv[r * GRID_W:(r + 1) * GRID_W, g * C_GROUP:(g + 1) * C_GROUP]
    for k2 in range(GRID_W):
        for g in range(C_GROUPS):
            p_ref[0, k2, :, g * C_GROUP:(g + 1) * C_GROUP] = ps_ref[g, pl.ds(k2, grows, stride=pitch), :]
            q_ref[0, k2, :, g * C_GROUP:(g + 1) * C_GROUP] = qs_ref[g, pl.ds(k2, grows, stride=pitch), :]


def _fourier_r_kernel(p_ref, q_ref, c_ref, s_ref, o_ref, ys_ref):
    nj = p_ref.shape[1]
    rows = p_ref.shape[2]
    pitch = rows + 1
    for j in range(nj):
        y = (jnp.dot(c_ref[...], p_ref[0, j].astype(BF16), preferred_element_type=F32)
             + jnp.dot(s_ref[...], q_ref[0, j].astype(BF16), preferred_element_type=F32))
        for g in range(C_GROUPS):
            ys_ref[g, j * pitch:j * pitch + rows, :] = y[:, g * C_GROUP:(g + 1) * C_GROUP]

    def put(k1, carry):
        for g in range(C_GROUPS):
            o_ref[0, k1, :, g * C_GROUP:(g + 1) * C_GROUP] = ys_ref[g, pl.ds(k1, nj, stride=pitch), :]
        return carry

    lax.fori_loop(0, rows, put, 0)


def _fourier_mix(zin, c_norm_w):
    bn, t, _ = zin.shape
    rows = t // GRID_W
    tile = 512
    grows = tile // GRID_W
    c3, s3 = _dft_mats(C_GROUP)
    c2, s2 = _dft_mats(GRID_W)
    c1, s1 = _dft_mats(rows)
    scale = 1.0 / math.sqrt(rows * GRID_W * C_GROUP)
    cs3 = jnp.asarray(np.concatenate([c3, s3], axis=1), BF16)
    eye = np.eye(256 // GRID_W)
    kc2 = jnp.asarray(np.kron(eye, c2), BF16)
    ks2 = jnp.asarray(np.kron(eye, s2), BF16)
    c1s = jnp.asarray(c1 * scale, BF16)
    s1s = jnp.asarray(-s1 * scale, BF16)
    pq_shape = jax.ShapeDtypeStruct((bn, GRID_W, rows, C_WIDTH), F32)
    pq_spec = pl.BlockSpec((1, GRID_W, grows, C_WIDTH), lambda b, i: (b, 0, i, 0))
    p, q = pl.pallas_call(
        _fourier_cw_kernel,
        grid=(bn, t // tile),
        in_specs=[pl.BlockSpec((1, tile, C_WIDTH), lambda b, i: (b, i, 0)),
                  pl.BlockSpec((1, C_WIDTH), lambda b, i: (0, 0)),
                  pl.BlockSpec((C_GROUP, 2 * C_GROUP), lambda b, i: (0, 0)),
                  pl.BlockSpec((256, 256), lambda b, i: (0, 0)),
                  pl.BlockSpec((256, 256), lambda b, i: (0, 0))],
        out_specs=[pq_spec, pq_spec],
        out_shape=[pq_shape, pq_shape],
        scratch_shapes=[pltpu.VMEM((C_GROUPS, grows * (GRID_W + 1), C_GROUP), F32),
                        pltpu.VMEM((C_GROUPS, grows * (GRID_W + 1), C_GROUP), F32)],
        compiler_params=_params("parallel", "parallel"),
        name="fourier_chan_col",
    )(zin, c_norm_w.reshape(1, C_WIDTH), cs3, kc2, ks2)
    nj = 8
    in_spec = pl.BlockSpec((1, nj, rows, C_WIDTH), lambda b, j: (b, j, 0, 0))
    mat_spec = pl.BlockSpec((rows, rows), lambda b, j: (0, 0))
    y = pl.pallas_call(
        _fourier_r_kernel,
        grid=(bn, GRID_W // nj),
        in_specs=[in_spec, in_spec, mat_spec, mat_spec],
        out_specs=pl.BlockSpec((1, rows, nj, C_WIDTH), lambda b, j: (b, 0, j, 0)),
        out_shape=jax.ShapeDtypeStruct((bn, rows, GRID_W, C_WIDTH), F32),
        scratch_shapes=[pltpu.VMEM((C_GROUPS, nj * (rows + 1), C_GROUP), F32)],
        compiler_params=_params("parallel", "parallel"),
        name="fourier_rows",
    )(p, q, c1s, s1s)
    return y.reshape(bn, t, C_WIDTH)


def _rank_kernel(ti_ref, rank_ref, cnt_ref, run_ref):
    @pl.when(pl.program_id(0) == 0)
    def _():
        run_ref[...] = jnp.zeros_like(run_ref)

    ti = ti_ref[...]
    tb = ti.shape[1]
    eio = lax.broadcasted_iota(jnp.int32, (N_EXPERTS, tb), 0)
    hot = [eio == ti[k:k + 1, :] for k in range(TOP_K)]
    occ = sum(h.astype(F32) for h in hot)
    si = lax.broadcasted_iota(jnp.int32, (tb, tb), 0)
    ti_ = lax.broadcasted_iota(jnp.int32, (tb, tb), 1)
    before = jnp.where(si < ti_, 1.0, 0.0).astype(BF16)
    seen = jnp.dot(occ.astype(BF16), before, preferred_element_type=F32) + run_ref[:, 0:1]
    rank_ref[...] = jnp.concatenate(
        [jnp.sum(jnp.where(h, seen, 0.0), axis=0, keepdims=True) for h in hot], axis=0).astype(jnp.int32)
    run_ref[...] = run_ref[...] + jnp.sum(occ, axis=1, keepdims=True)
    cnt_ref[...] = run_ref[...].astype(jnp.int32)


def _ranks(top_i):
    ntok = top_i.shape[1]
    tb = ROUTE_TILE if ntok % ROUTE_TILE == 0 else 256
    rank, cnt = pl.pallas_call(
        _rank_kernel,
        grid=(ntok // tb,),
        in_specs=[pl.BlockSpec((TOP_K, tb), lambda i: (0, i))],
        out_specs=[pl.BlockSpec((TOP_K, tb), lambda i: (0, i)),
                   pl.BlockSpec((N_EXPERTS, 128), lambda i: (0, 0))],
        out_shape=[jax.ShapeDtypeStruct((TOP_K, ntok), jnp.int32),
                   jax.ShapeDtypeStruct((N_EXPERTS, 128), jnp.int32)],
        scratch_shapes=[pltpu.VMEM((N_EXPERTS, 128), F32)],
        compiler_params=_params("arbitrary"),
        name="route_rank",
    )(top_i)
    return rank, cnt[:, 0]


def _w1_split_kernel(w_ref, g_ref, l_ref, s_ref):
    kdim, cols = w_ref.shape[2], w_ref.shape[3]
    half = cols // 2
    nk = kdim // LANES
    t = w_ref[0, 0].T
    for kc in range(nk):
        s_ref[kc] = t[:, kc * LANES:(kc + 1) * LANES]
    even = jnp.concatenate([s_ref[kc, pl.ds(0, half, stride=2), :] for kc in range(nk)], axis=1)
    odd = jnp.concatenate([s_ref[kc, pl.ds(1, half, stride=2), :] for kc in range(nk)], axis=1)
    g_ref[0] = even.T.astype(BF16)
    l_ref[0] = odd.T.astype(BF16)


def _w1_split(w1_all, layer):
    _, e, d, f2 = w1_all.shape
    cols = 1024
    out = jax.ShapeDtypeStruct((e, d, f2 // 2), BF16)
    out_spec = pl.BlockSpec((1, d, cols // 2), lambda i, c: (i, 0, c))
    return pl.pallas_call(
        _w1_split_kernel,
        grid=(e, f2 // cols),
        in_specs=[pl.BlockSpec((1, 1, d, cols), lambda i, c: (layer, i, 0, c))],
        out_specs=[out_spec, out_spec],
        out_shape=[out, out],
        scratch_shapes=[pltpu.VMEM((d // LANES, cols, LANES), F32)],
        compiler_params=_params("parallel", "parallel"),
        name="moe_w1_split",
    )(w1_all)


def _cast_kernel(w_ref, o_ref):
    o_ref[...] = w_ref[0].astype(o_ref.dtype)


def _w2_cast(w2_all, layer):
    _, e, f, d = w2_all.shape
    return pl.pallas_call(
        _cast_kernel,
        grid=(e,),
        in_specs=[pl.BlockSpec((1, 1, f, d), lambda i: (layer, i, 0, 0))],
        out_specs=pl.BlockSpec((1, f, d), lambda i: (i, 0, 0)),
        out_shape=jax.ShapeDtypeStruct((e, f, d), BF16),
        compiler_params=_params("parallel"),
        name="moe_w2_cast",
    )(w2_all)


def _expert_kernel(be_ref, used_ref, x_ref, w1g_ref, w1l_ref, w2_ref, b1g_ref, b1l_ref, b2_ref, y_ref):
    del be_ref
    rows = x_ref.shape[0] // ROW_TILES
    in_use = pl.program_id(0) < used_ref[0]

    @pl.when(in_use)
    def _():
        x = _load_row_tiles(x_ref, rows).astype(BF16)
        glu = jnp.minimum(jnp.dot(x, w1g_ref[0], preferred_element_type=F32) + b1g_ref[0], SWIGLU_LIMIT)
        lin = jnp.clip(jnp.dot(x, w1l_ref[0], preferred_element_type=F32) + b1l_ref[0],
                       -SWIGLU_LIMIT, SWIGLU_LIMIT)
        act = glu * _sigmoid(SWIGLU_ALPHA * glu) * (lin + 1.0)
        _store_row_tiles(y_ref, jnp.dot(act.astype(BF16), w2_ref[0], preferred_element_type=F32) + b2_ref[0])

    @pl.when(jnp.logical_not(in_use))
    def _():
        y_ref[...] = jnp.zeros_like(y_ref)


def _experts(x_sorted, block_expert, blocks_used, w1g, w1l, w2, b1g, b1l, b2):
    d, f = w1g.shape[1], w1g.shape[2]
    nb = block_expert.shape[0]
    buf_rows = MOE_ROWS * ROW_TILES
    wspec = lambda k, n: pl.BlockSpec((1, k, n), lambda j, be, nu: (be[j], 0, 0))
    row_spec = pl.BlockSpec((buf_rows, LANES), lambda j, be, nu: (j, 0))
    grid_spec = pltpu.PrefetchScalarGridSpec(
        num_scalar_prefetch=2,
        grid=(nb,),
        in_specs=[row_spec, wspec(d, f), wspec(d, f), wspec(f, d), wspec(1, f), wspec(1, f), wspec(1, d)],
        out_specs=row_spec)
    return pl.pallas_call(
        _expert_kernel,
        grid_spec=grid_spec,
        out_shape=jax.ShapeDtypeStruct(x_sorted.shape, F32),
        compiler_params=_params("parallel"),
        name="moe_experts",
    )(block_expert, blocks_used, x_sorted, w1g, w1l, w2, b1g, b1l, b2)


def _dispatch_kernel(cnt_ref, start_ref, dst_ref, h_ref, xs_hbm, rowbuf, zero_ref, sem, zsem, *, nt, nslots):
    i = pl.program_id(0)
    tb = h_ref.shape[0]
    slot = i % 2

    def landed(buf):
        for _ in range(TOP_K):
            pltpu.make_async_copy(rowbuf.at[buf, pl.ds(0, tb * ROW_TILES), :],
                                  xs_hbm.at[pl.ds(0, tb * ROW_TILES), :], sem.at[buf]).wait()

    @pl.when(i >= 2)
    def _():
        landed(slot)

    _store_row_tiles(rowbuf.at[slot], h_ref[...].astype(F32), pitch=ROW_PITCH)
    for r in range(tb):
        for k in range(TOP_K):
            dst = pl.multiple_of(dst_ref[0, k, r] * ROW_TILES, ROW_TILES)
            pltpu.make_async_copy(rowbuf.at[slot, pl.ds(r * ROW_PITCH, ROW_TILES), :],
                                  xs_hbm.at[pl.ds(dst, ROW_TILES), :], sem.at[slot]).start(priority=k % 2)

    @pl.when(i == nt - 1)
    def _():
        landed(slot)
        if nt > 1:
            landed(1 - slot)
        zero_ref[...] = jnp.zeros_like(zero_ref)

        zrows = zero_ref.shape[0] // ROW_TILES

        def zero_rows(pos, n):
            cp = pltpu.make_async_copy(
                zero_ref.at[pl.ds(0, n * ROW_TILES), :],
                xs_hbm.at[pl.ds(pl.multiple_of(pos * ROW_TILES, ROW_TILES), n * ROW_TILES), :], zsem)
            cp.start()
            cp.wait()

        end = 0
        for e in range(N_EXPERTS):
            pos = start_ref[e] + cnt_ref[e]
            end = start_ref[e] + (cnt_ref[e] + MOE_ROWS - 1) // MOE_ROWS * MOE_ROWS
            gap = end - pos
            bit = zrows
            while bit >= 1:
                take = (gap & bit) != 0
                pl.when(take)(functools.partial(zero_rows, pos, bit))
                pos = pos + jnp.where(take, bit, 0)
                bit //= 2

        def tail(i, carry):
            zero_rows(end + i * zrows, zrows)
            return carry
        lax.fori_loop(0, (nslots - end) // zrows, tail, 0)


def _dispatch(h2, dest3, counts, pad_starts, nslots):
    ntok, d = h2.shape
    tb = COMBINE_TILE
    nt = ntok // tb
    assert MOE_ROWS & (MOE_ROWS - 1) == 0, "the padding fill decomposes gap lengths into powers of two"
    grid_spec = pltpu.PrefetchScalarGridSpec(
        num_scalar_prefetch=2,
        grid=(nt,),
        in_specs=[pl.BlockSpec((1, TOP_K, tb), lambda i, c, s: (i, 0, 0), memory_space=pltpu.SMEM),
                  pl.BlockSpec((tb, d), lambda i, c, s: (i, 0))],
        out_specs=pl.BlockSpec(memory_space=pl.ANY),
        scratch_shapes=[pltpu.VMEM((2, tb * ROW_PITCH, LANES), F32),
                        pltpu.VMEM((MOE_ROWS // 2 * ROW_TILES, LANES), F32),
                        pltpu.SemaphoreType.DMA((2,)), pltpu.SemaphoreType.DMA(())])
    return pl.pallas_call(
        functools.partial(_dispatch_kernel, nt=nt, nslots=nslots),
        grid_spec=grid_spec,
        out_shape=jax.ShapeDtypeStruct((nslots * ROW_TILES, LANES), F32),
        compiler_params=_params("arbitrary"),
        name="moe_dispatch",
    )(counts, pad_starts, dest3, h2)


def _combine_kernel(dst_ref, dstn_ref, y_hbm, x1_ref, gate_ref, g2_ref, lnw_ref, lnb_ref, o_ref, ybuf, sem, *, nt):
    i = pl.program_id(0)
    tb = x1_ref.shape[0]
    slot = i % 2

    def row_copy(src, buf, k, r):
        rows = y_hbm.at[pl.ds(pl.multiple_of(src * ROW_TILES, ROW_TILES), ROW_TILES), :]
        return pltpu.make_async_copy(rows, ybuf.at[buf, k, pl.ds(r * ROW_PITCH, ROW_TILES), :], sem.at[buf])

    def gather(idx_ref, buf):
        for r in range(tb):
            for k in range(TOP_K):
                row_copy(idx_ref[0, k, r], buf, k, r).start(priority=k % 2)

    @pl.when(i == 0)
    def _():
        gather(dst_ref, 0)

    @pl.when(i + 1 < nt)
    def _():
        gather(dstn_ref, 1 - slot)

    for k in range(TOP_K):
        pltpu.make_async_copy(y_hbm.at[pl.ds(0, tb * ROW_TILES), :],
                              ybuf.at[slot, k, pl.ds(0, tb * ROW_TILES), :], sem.at[slot]).wait()

    y = sum(gate_ref[:, k:k + 1] * _load_row_tiles(ybuf.at[slot, k], tb, pitch=ROW_PITCH) for k in range(TOP_K))
    r = DN_ALPHA * x1_ref[...] + g2_ref[0] * y
    mu = jnp.mean(r, axis=-1, keepdims=True)
    rc = r - mu
    var = jnp.mean(rc * rc, axis=-1, keepdims=True)
    o_ref[...] = rc * lax.rsqrt(var + EPS) * lnw_ref[...] + lnb_ref[...]


def _combine(x1, y_sorted, dest3, gates_t, tok_off, g2, mod_row, lnw, lnb):
    bn, t, d = x1.shape
    n = bn * t
    tb = COMBINE_TILE
    nt = n // tb
    off = tok_off // tb
    per_b = t // tb
    out = pl.pallas_call(
        functools.partial(_combine_kernel, nt=nt),
        grid=(nt,),
        in_specs=[pl.BlockSpec((1, TOP_K, tb), lambda i: (off + i, 0, 0), memory_space=pltpu.SMEM),
                  pl.BlockSpec((1, TOP_K, tb), lambda i: (off + jnp.minimum(i + 1, nt - 1), 0, 0),
                               memory_space=pltpu.SMEM),
                  pl.BlockSpec(memory_space=pl.ANY),
                  pl.BlockSpec((tb, d), lambda i: (i, 0)),
                  pl.BlockSpec((tb, TOP_K), lambda i: (off + i, 0)),
                  pl.BlockSpec((1, 1, d), lambda i: (mod_row(i // per_b), 0, 0)),
                  pl.BlockSpec((1, d), lambda i: (0, 0)),
                  pl.BlockSpec((1, d), lambda i: (0, 0))],
        out_specs=pl.BlockSpec((tb, d), lambda i: (i, 0)),
        out_shape=jax.ShapeDtypeStruct((n, d), F32),
        scratch_shapes=[pltpu.VMEM((2, TOP_K, tb * ROW_PITCH, LANES), F32), pltpu.SemaphoreType.DMA((2,))],
        compiler_params=_params("arbitrary"),
        name="moe_combine",
    )(dest3, dest3, y_sorted, x1.reshape(n, d), gates_t, g2, lnw.reshape(1, d), lnb.reshape(1, d))
    return out.reshape(bn, t, d)


def _dest_kernel(ti_ref, rank_ref, start_ref, dest_ref):
    ti = ti_ref[...]
    tb = ti.shape[1]
    eio = lax.broadcasted_iota(jnp.int32, (N_EXPERTS, tb), 0)
    start = start_ref[:, 0:1].astype(F32)
    base = jnp.concatenate(
        [jnp.sum(jnp.where(eio == ti[k:k + 1, :], start, 0.0), axis=0, keepdims=True) for k in range(TOP_K)], axis=0)
    dest = base.astype(jnp.int32) + rank_ref[...]
    for c in range(tb // COMBINE_TILE):
        dest_ref[c] = dest[:, c * COMBINE_TILE:(c + 1) * COMBINE_TILE]


def _moe_layout(top_i):
    ntok = top_i.shape[1]
    m = ntok * TOP_K
    rank, counts = _ranks(top_i)
    padded = (counts + MOE_ROWS - 1) // MOE_ROWS * MOE_ROWS
    pad_ends = jnp.cumsum(padded)
    pad_starts = pad_ends - padded
    tb = ROUTE_TILE if ntok % ROUTE_TILE == 0 else 256
    ntiles = ntok // COMBINE_TILE
    dest = pl.pallas_call(
        _dest_kernel,
        grid=(ntok // tb,),
        in_specs=[pl.BlockSpec((TOP_K, tb), lambda i: (0, i)),
                  pl.BlockSpec((TOP_K, tb), lambda i: (0, i)),
                  pl.BlockSpec((N_EXPERTS, LANES), lambda i: (0, 0))],
        out_specs=pl.BlockSpec((tb // COMBINE_TILE, TOP_K, COMBINE_TILE), lambda i: (i, 0, 0)),
        out_shape=jax.ShapeDtypeStruct((ntiles, TOP_K, COMBINE_TILE), jnp.int32),
        compiler_params=_params("parallel"),
        name="route_dest",
    )(top_i, rank, jnp.broadcast_to(pad_starts[:, None], (N_EXPERTS, LANES)))
    nb = (m + N_EXPERTS * (MOE_ROWS - 1) + MOE_ROWS - 1) // MOE_ROWS
    block_starts = jnp.arange(nb, dtype=jnp.int32) * MOE_ROWS
    block_expert = jnp.minimum(jnp.sum(block_starts[:, None] >= pad_ends[None, :], axis=1),
                               N_EXPERTS - 1).astype(jnp.int32)
    blocks_used = (pad_ends[-1:] // MOE_ROWS).astype(jnp.int32)
    return dest, counts, pad_starts.astype(jnp.int32), block_expert, blocks_used, nb * MOE_ROWS


def kernel(x, c, ctx, c_ctx, lower_bounds_fwd, lower_bounds_bwd, ada_w, ada_b, w_in_even, a_norm_w, conv_w,
           w_in_odd, c_norm_w, d_norm_w, d_norm_b, spatial_w, spatial_b, w_out, ln_mix_w, ln_mix_b,
           ln_ffn_w, ln_ffn_b, router_w, router_b, moe_w1, moe_b1, moe_w2, moe_b2):
    bn, t, d = x.shape
    tc = ctx.shape[1]
    n_lat = bn * t
    n_ctx = bn * tc
    ctx_row = bn
    lat_row = lambda b: b
    ctx_mod_row = lambda b: ctx_row

    lb_f_all = jnp.cumsum(jax.nn.softmax(lower_bounds_fwd.astype(F32), axis=0), axis=0)
    lb_b_all = jnp.cumsum(jax.nn.softmax(lower_bounds_bwd.astype(F32), axis=0), axis=0)

    cond = jnp.zeros((MOD_ROWS, d), F32).at[:bn].set(c).at[ctx_row].set(c_ctx)
    mods = _ada(cond, ada_w, ada_b).reshape(DEPTH, MOD_ROWS, 6, 1, d)

    h_ctx = ctx
    for l in range(DEPTH):
        last = l == DEPTH - 1
        sh1, sc1, g1, sh2, sc2, g2 = (mods[l, :, j] for j in range(6))
        wo = w_out[l].astype(BF16)
        rw_t = router_w[l].T.astype(BF16)
        ntok = n_lat if last else n_lat + n_ctx
        tok_state = (jnp.zeros((ntok, d), BF16), jnp.zeros((TOP_K, ntok), jnp.int32),
                     jnp.zeros((TOP_K, ntok), F32))
        post = dict(wo=wo, lnw=ln_mix_w[l], lnb=ln_mix_b[l], rw_t=rw_t, rb=router_b[l])
        bm_ctx = min(ROW_TILE, tc)
        if l % 2 == 0:
            e = l // 2
            w_in = w_in_even[e].astype(BF16)
            zin_c = _inproj(h_ctx, sc1, sh1, w_in, ctx_mod_row, bm=bm_ctx)
            zin = _inproj(x, sc1, sh1, w_in, lat_row, bm=ROW_TILE)
            zeros = jnp.zeros((bn, A_HEADS, A_DV, A_DK), F32)
            oc_f, sc_f = _scan(zin_c, lb_f_all[l], zeros, reverse=False)
            a_ctx, sc_b = _scan(zin_c, lb_b_all[l], zeros, reverse=True, o_prev=oc_f, norm_w=a_norm_w[e])
            o_f, _ = _scan(zin, lb_f_all[l], sc_f, reverse=False)
            a_lat, _ = _scan(zin, lb_b_all[l], sc_b, reverse=True, o_prev=o_f, norm_w=a_norm_w[e])
            x1, *tok_state = _outproj_even(a_lat, zin, conv_w[e], x=x, mods=(g1, sc2, sh2), mod_row=lat_row,
                                           tok_state=tok_state, tok_off=0, bm=ROW_TILE, **post)
            if not last:
                hc1, *tok_state = _outproj_even(a_ctx, zin_c, conv_w[e], x=h_ctx, mods=(g1, sc2, sh2),
                                                mod_row=ctx_mod_row, tok_state=tok_state, tok_off=n_lat,
                                                bm=bm_ctx, **post)
        else:
            o = l // 2
            w_in = w_in_odd[o].astype(BF16)
            odd = dict(d_norm_w=d_norm_w[o], d_norm_b=d_norm_b[o], sp_w=spatial_w[o], sp_b=spatial_b[o])
            zin = _inproj(x, sc1, sh1, w_in, lat_row, bm=ROW_TILE)
            c_mix = _fourier_mix(zin, c_norm_w[o])
            x1, *tok_state = _outproj_odd(c_mix, zin, x=x, mods=(g1, sc2, sh2), mod_row=lat_row,
                                          tok_state=tok_state, tok_off=0, bm=ROW_TILE, **odd, **post)
            if not last:
                raise NotImplementedError("an odd layer that is not the last needs the context Fourier mixer")
        h2_all, top_i, top_g = tok_state
        dest, counts, pad_starts, block_expert, blocks_used, nslots = _moe_layout(top_i)
        x_sorted = _dispatch(h2_all, dest, counts, pad_starts, nslots)
        w1g, w1l = _w1_split(moe_w1, l)
        y_sorted = _experts(
            x_sorted, block_expert, blocks_used, w1g, w1l, _w2_cast(moe_w2, l),
            moe_b1[l][:, None, 0::2], moe_b1[l][:, None, 1::2], moe_b2[l][:, None, :])
        gates_t = top_g.T
        x = _combine(x1, y_sorted, dest, gates_t, 0, g2, lat_row, ln_ffn_w[l], ln_ffn_b[l])
        if not last:
            h_ctx = _combine(hc1, y_sorted, dest, gates_t, n_lat, g2, ctx_mod_row, ln_ffn_w[l], ln_ffn_b[l])
    return x
```

```python
import functools
import math

import numpy as np
import jax
import jax.numpy as jnp
from jax import lax
from jax.experimental import pallas as pl
from jax.experimental.pallas import tpu as pltpu

F32 = jnp.float32
BF16 = jnp.bfloat16

D_MODEL = 1024
DEPTH = 2
GRID_W = 64
A_HEADS = 4
A_DK = 128
A_DV = 128
A_KW = A_HEADS * A_DK
A_WIDTH = A_HEADS * A_DV
B_WIDTH = D_MODEL - A_WIDTH
CONV_W = 3
C_GROUPS = 4
C_GROUP = 128
C_WIDTH = C_GROUPS * C_GROUP
D_WIDTH = D_MODEL - C_WIDTH
D_GROUPS = 4
D_GDIM = D_WIDTH // D_GROUPS
D_CHUNK = 128
SCAN_CHUNK = 32
EVEN_IN = 3 * A_KW + 2 * A_WIDTH + 3 * B_WIDTH
ODD_IN = C_WIDTH + 2 * D_WIDTH
N_EXPERTS = 32
TOP_K = 4
D_EXPERT = 1024
SWIGLU_ALPHA = 1.702
SWIGLU_LIMIT = 7.0
DN_ALPHA = (2 * DEPTH) ** 0.25
EPS = 1e-5

MOD_ROWS = 8
SCAN_ROWS = 256
SCAN_HEADS = 4
SCAN_SPAN = 256
ROW_TILE = 512
SUB_TILE = 256
MOE_ROWS = 512
ROUTE_TILE = 512
COMBINE_TILE = 256
HALO = 16
LANES = 128
ROW_TILES = D_MODEL // LANES
ROW_PITCH = ROW_TILES + 1

_NT = (((1,), (1,)), ((), ()))
_TN = (((0,), (0,)), ((), ()))


def _sigmoid(a):
    return 1.0 / (1.0 + jnp.exp(-a))


def _silu(a):
    return a * _sigmoid(a)


def _gelu(a):
    return 0.5 * a * (1.0 + lax.erf(a * (1.0 / math.sqrt(2.0))))


def _store_row_tiles(ref, val, pitch=ROW_TILES):
    rows = val.shape[0]
    for c in range(ROW_TILES):
        ref[pl.ds(c, rows, stride=pitch), :] = val[:, c * LANES:(c + 1) * LANES]


def _load_row_tiles(ref, rows, pitch=ROW_TILES):
    return jnp.concatenate([ref[pl.ds(c, rows, stride=pitch), :] for c in range(ROW_TILES)], axis=1)


V7X_VMEM_BYTES = 64 * 1024 * 1024
VMEM_LIMIT_BYTES = V7X_VMEM_BYTES - 8 * 1024 * 1024


def _params(*sem):
    return pltpu.CompilerParams(dimension_semantics=sem, vmem_limit_bytes=VMEM_LIMIT_BYTES)


def _ada_kernel(c_ref, w_ref, b_ref, o_ref):
    s = _silu(c_ref[...])
    o_ref[0] = jnp.dot(s, w_ref[0], preferred_element_type=F32, precision=lax.Precision.HIGHEST) + b_ref[0]


def _ada(cond, ada_w, ada_b):
    d = cond.shape[1]
    n6 = ada_w.shape[2]
    tn = 1536
    return pl.pallas_call(
        _ada_kernel,
        grid=(DEPTH, n6 // tn),
        in_specs=[pl.BlockSpec((MOD_ROWS, d), lambda l, n: (0, 0)),
                  pl.BlockSpec((1, d, tn), lambda l, n: (l, 0, n)),
                  pl.BlockSpec((1, 1, tn), lambda l, n: (l, 0, n))],
        out_specs=pl.BlockSpec((1, MOD_ROWS, tn), lambda l, n: (l, 0, n)),
        out_shape=jax.ShapeDtypeStruct((DEPTH, MOD_ROWS, n6), F32),
        compiler_params=_params("parallel", "parallel"),
        name="ada_mod",
    )(cond, ada_w, ada_b.reshape(DEPTH, 1, n6))


def _inproj_kernel(x_ref, sc_ref, sh_ref, w_ref, o_ref, *, tn):
    h = (x_ref[0] * (1.0 + sc_ref[0]) + sh_ref[0]).astype(BF16)
    for n in range(o_ref.shape[2] // tn):
        o_ref[0, :, n * tn:(n + 1) * tn] = jnp.dot(
            h, w_ref[:, n * tn:(n + 1) * tn], preferred_element_type=F32).astype(o_ref.dtype)


def _inproj(x, sc, sh, w, mod_row, bm):
    bn, t, d = x.shape
    nout = w.shape[1]
    mod_spec = pl.BlockSpec((1, 1, d), lambda b, i: (mod_row(b), 0, 0))
    return pl.pallas_call(
        functools.partial(_inproj_kernel, tn=512),
        grid=(bn, t // bm),
        in_specs=[pl.BlockSpec((1, bm, d), lambda b, i: (b, i, 0)), mod_spec, mod_spec,
                  pl.BlockSpec((d, nout), lambda b, i: (0, 0))],
        out_specs=pl.BlockSpec((1, bm, nout), lambda b, i: (b, i, 0)),
        out_shape=jax.ShapeDtypeStruct((bn, t, nout), BF16),
        compiler_params=_params("parallel", "parallel"),
        name="in_proj",
    )(x, sc, sh, w)


def _scan_kernel(*refs, reverse, finalize):
    if finalize:
        (q_ref, z_ref, v_ref, lb_ref, s0_ref, g_ref, op_ref, nw_ref, o_ref, sfin_ref, st_ref, oacc_ref) = refs
    else:
        (q_ref, z_ref, v_ref, lb_ref, s0_ref, o_ref, sfin_ref, st_ref, oacc_ref) = refs
    rows = q_ref.shape[1]
    heads = q_ref.shape[2] // A_DK
    nchunk = rows // SCAN_CHUNK

    @pl.when(pl.program_id(2) == 0)
    def _():
        st_ref[...] = s0_ref[0]

    q = _silu(q_ref[0].astype(F32))
    lb = lb_ref[...]
    f = lb + (1.0 - lb) * _sigmoid(z_ref[0].astype(F32))
    logf = jnp.log(f)
    k = 1.0 - f
    v = v_ref[0]

    ri = lax.broadcasted_iota(jnp.int32, (rows, rows), 0)
    ci = lax.broadcasted_iota(jnp.int32, (rows, rows), 1)
    same = (ri // SCAN_CHUNK) == (ci // SCAN_CHUNK)
    causal = same & ((ci >= ri) if reverse else (ci <= ri))
    tri = jnp.where(causal, 1.0, 0.0).astype(BF16)
    blk = jnp.where(same, 1.0, 0.0).astype(BF16)
    hi = logf.astype(BF16)
    lo = (logf - hi.astype(F32)).astype(BF16)
    b = jnp.dot(tri, hi, preferred_element_type=F32) + jnp.dot(tri, lo, preferred_element_type=F32)
    b_tot = jnp.dot(blk, hi, preferred_element_type=F32) + jnp.dot(blk, lo, preferred_element_type=F32)

    q_dec = (q * jnp.exp(b)).astype(BF16)
    k_inv = (k * jnp.exp(-b)).astype(BF16)
    k_end = (k * jnp.exp(b_tot - b)).astype(BF16)
    dec = jnp.exp(b_tot)

    span = min(SCAN_SPAN, rows)
    per = span // SCAN_CHUNK
    wide = per * A_DK
    spread = (lax.broadcasted_iota(jnp.int32, (span, wide), 0) // SCAN_CHUNK
              == lax.broadcasted_iota(jnp.int32, (span, wide), 1) // A_DK)
    causal_span = causal[0:span, 0:span]
    spans = range(rows // span - 1, -1, -1) if reverse else range(rows // span)
    order = range(per - 1, -1, -1) if reverse else range(per)
    for h in range(heads):
        hk = slice(h * A_DK, (h + 1) * A_DK)
        hv = slice(h * A_DV, (h + 1) * A_DV)
        st = st_ref[h]
        for sp in spans:
            rs = slice(sp * span, (sp + 1) * span)
            qd, vh = q_dec[rs, hk], v[rs, hv]
            scores = lax.dot_general(qd, k_inv[rs, hk], _NT, preferred_element_type=F32)
            scores = jnp.where(causal_span, scores, 0.0).astype(BF16)
            o_intra = jnp.dot(scores, vh, preferred_element_type=F32)
            ke_wide = jnp.where(spread, jnp.concatenate([k_end[rs, hk]] * per, axis=1), 0.0).astype(BF16)
            upd = lax.dot_general(vh, ke_wide, _TN, preferred_element_type=F32)
            seen = [None] * per
            for c in order:
                seen[c] = st.astype(BF16)
                r0 = sp * span + c * SCAN_CHUNK
                st = st * dec[r0:r0 + 1, hk] + upd[:, c * A_DK:(c + 1) * A_DK]
            qd_wide = jnp.where(spread, jnp.concatenate([qd] * per, axis=1), 0.0).astype(BF16)
            o_inter = lax.dot_general(qd_wide, jnp.concatenate(seen, axis=1), _NT, preferred_element_type=F32)
            oacc_ref[rs, hv] = o_intra + o_inter
        st_ref[h] = st
        sfin_ref[0, h] = st

    o = oacc_ref[...]
    if finalize:
        o = o + op_ref[0]
        gate = nw_ref[...] * _silu(g_ref[0].astype(F32))
        for h in range(heads):
            hv = slice(h * A_DV, (h + 1) * A_DV)
            oh = o[:, hv]
            oh = oh * lax.rsqrt(jnp.mean(oh * oh, axis=-1, keepdims=True) + EPS)
            o_ref[0, :, hv] = (oh * gate[:, hv]).astype(o_ref.dtype)
    else:
        o_ref[0] = o.astype(o_ref.dtype)


def _scan(zin, lb, s0, *, reverse, o_prev=None, norm_w=None):
    bn, t, _ = zin.shape
    finalize = o_prev is not None
    rows = min(SCAN_ROWS, t)
    nb = t // rows
    pos = (lambda n: nb - 1 - n) if reverse else (lambda n: n)
    hp = SCAN_HEADS
    groups = A_HEADS // hp
    hw = hp * A_DK

    def col(section):
        return pl.BlockSpec((1, rows, hw), lambda b, h, n: (b, pos(n), section * groups + h))

    state_spec = pl.BlockSpec((1, hp, A_DV, A_DK), lambda b, h, n: (b, h, 0, 0))
    in_specs = [col(0), col(2 if reverse else 1), col(3),
                pl.BlockSpec((1, hw), lambda b, h, n: (0, h)), state_spec]
    args = [zin, zin, zin, lb.reshape(1, A_KW), s0]
    if finalize:
        in_specs += [col(4),
                     pl.BlockSpec((1, rows, hw), lambda b, h, n: (b, pos(n), h)),
                     pl.BlockSpec((1, hw), lambda b, h, n: (0, h))]
        args += [zin, o_prev, norm_w.reshape(1, A_WIDTH)]
    return pl.pallas_call(
        functools.partial(_scan_kernel, reverse=reverse, finalize=finalize),
        grid=(bn, groups, nb),
        in_specs=in_specs,
        out_specs=[pl.BlockSpec((1, rows, hw), lambda b, h, n: (b, pos(n), h)), state_spec],
        out_shape=[jax.ShapeDtypeStruct((bn, t, A_WIDTH), BF16 if finalize else F32),
                   jax.ShapeDtypeStruct((bn, A_HEADS, A_DV, A_DK), F32)],
        scratch_shapes=[pltpu.VMEM((hp, A_DV, A_DK), F32), pltpu.VMEM((rows, hw), F32)],
        compiler_params=_params("parallel", "parallel", "arbitrary"),
        name="hgrn2_scan_bwd" if reverse else "hgrn2_scan_fwd",
    )(*args)


def _sub_tiles(bm):
    sub = min(bm, SUB_TILE)
    return [slice(s * sub, (s + 1) * sub) for s in range(bm // sub)]


def _post_mix(acc, rs, x_ref, g1_ref, lnw_ref, lnb_ref, sc2_ref, sh2_ref, rw_ref, rb_ref,
              x1_ref, h2_ref, ti_ref, tg_ref):
    r = DN_ALPHA * x_ref[0, rs, :] + g1_ref[0] * acc
    mu = jnp.mean(r, axis=-1, keepdims=True)
    rc = r - mu
    var = jnp.mean(rc * rc, axis=-1, keepdims=True)
    x1 = rc * lax.rsqrt(var + EPS) * lnw_ref[...] + lnb_ref[...]
    x1_ref[0, rs, :] = x1
    h2 = x1 * (1.0 + sc2_ref[0]) + sh2_ref[0]
    h2_ref[rs, :] = h2.astype(h2_ref.dtype)
    logits = lax.dot_general(rw_ref[...], h2.astype(BF16), _NT, preferred_element_type=F32) + rb_ref[...]
    iota = lax.broadcasted_iota(jnp.int32, logits.shape, 0)
    vals, idxs = [], []
    cur = logits
    for _ in range(TOP_K):
        m = jnp.max(cur, axis=0, keepdims=True)
        ik = jnp.min(jnp.where(cur == m, iota, N_EXPERTS), axis=0, keepdims=True)
        vals.append(m)
        idxs.append(ik)
        cur = jnp.where(iota == ik, -jnp.inf, cur)
    tv = jnp.concatenate(vals, axis=0)
    e = jnp.exp(tv - tv[0:1])
    tg_ref[:, rs] = e / jnp.sum(e, axis=0, keepdims=True)
    ti_ref[:, rs] = jnp.concatenate(idxs, axis=0)


def _outproj_even_kernel(a_ref, u_ref, gb_ref, gc_ref, up_ref, gcp_ref, un_ref, gcn_ref, cw_ref, wo_ref,
                         x_ref, g1_ref, lnw_ref, lnb_ref, sc2_ref, sh2_ref, rw_ref, rb_ref, alias_ref,
                         x1_ref, h2_ref, ti_ref, tg_ref):
    del alias_ref
    i = pl.program_id(1)
    bm = u_ref.shape[1]
    z = gc_ref[0].astype(F32) * u_ref[0].astype(F32)
    z_before = gcp_ref[0, HALO - 1:HALO, :].astype(F32) * up_ref[0, HALO - 1:HALO, :].astype(F32)
    z_after = gcn_ref[0, 0:1, :].astype(F32) * un_ref[0, 0:1, :].astype(F32)
    z_before = jnp.where(i == 0, 0.0, z_before)
    z_after = jnp.where(i == pl.num_programs(1) - 1, 0.0, z_after)
    row = lax.broadcasted_iota(jnp.int32, z.shape, 0)
    z_prev = jnp.where(row == 0, z_before, pltpu.roll(z, 1, axis=0))
    z_next = jnp.where(row == bm - 1, z_after, pltpu.roll(z, bm - 1, axis=0))
    zc = cw_ref[0:1, :] * z_prev + cw_ref[1:2, :] * z + cw_ref[2:3, :] * z_next
    b_mix = (gb_ref[0].astype(F32) * zc).astype(BF16)
    for rs in _sub_tiles(bm):
        acc = (jnp.dot(a_ref[0, rs, :], wo_ref[0:A_WIDTH, :], preferred_element_type=F32)
               + jnp.dot(b_mix[rs], wo_ref[A_WIDTH:, :], preferred_element_type=F32))
        _post_mix(acc, rs, x_ref, g1_ref, lnw_ref, lnb_ref, sc2_ref, sh2_ref, rw_ref, rb_ref,
                  x1_ref, h2_ref, ti_ref, tg_ref)


def _outproj_odd_kernel(c_ref, u_ref, v_ref, dw_ref, db_ref, spw_ref, spb_ref, wo_ref,
                        x_ref, g1_ref, lnw_ref, lnb_ref, sc2_ref, sh2_ref, rw_ref, rb_ref, alias_ref,
                        x1_ref, h2_ref, ti_ref, tg_ref):
    del alias_ref
    bm = u_ref.shape[1]
    u = _gelu(u_ref[0].astype(F32))
    v = _gelu(v_ref[0].astype(F32))
    cols = []
    for g in range(D_GROUPS):
        lo_c, hi_c = g * D_GDIM, (g + 1) * D_GDIM
        vg = v[:, lo_c:hi_c]
        mu = jnp.mean(vg, axis=-1, keepdims=True)
        vc = vg - mu
        var = jnp.mean(vc * vc, axis=-1, keepdims=True)
        vn = (vc * lax.rsqrt(var + EPS) * dw_ref[:, lo_c:hi_c] + db_ref[:, lo_c:hi_c]).astype(BF16)
        parts = []
        for ch in range(bm // D_CHUNK):
            sv = jnp.dot(spw_ref[g], vn[ch * D_CHUNK:(ch + 1) * D_CHUNK], preferred_element_type=F32)
            parts.append(sv + spb_ref[:, g:g + 1])
        cols.append(u[:, lo_c:hi_c] * jnp.concatenate(parts, axis=0))
    d_mix = jnp.concatenate(cols, axis=1).astype(BF16)
    for rs in _sub_tiles(bm):
        acc = (jnp.dot(c_ref[0, rs, :].astype(BF16), wo_ref[0:C_WIDTH, :], preferred_element_type=F32)
               + jnp.dot(d_mix[rs], wo_ref[C_WIDTH:, :], preferred_element_type=F32))
        _post_mix(acc, rs, x_ref, g1_ref, lnw_ref, lnb_ref, sc2_ref, sh2_ref, rw_ref, rb_ref,
                  x1_ref, h2_ref, ti_ref, tg_ref)


def _outproj(kind, mixer_args, mixer_specs, wo, x, mods, mod_row, lnw, lnb, rw_t, rb, tok_state, tok_off, bm):
    bn, t, d = x.shape
    nt = t // bm
    g1, sc2, sh2 = mods
    h2_all, ti_all, tg_all = tok_state
    ntok = h2_all.shape[0]
    off = tok_off // bm
    carried = [] if bn * t == ntok else [h2_all, ti_all, tg_all]
    mod_spec = pl.BlockSpec((1, 1, d), lambda b, i: (mod_row(b), 0, 0))
    vec_spec = pl.BlockSpec((1, d), lambda b, i: (0, 0))
    in_specs = list(mixer_specs) + [
        pl.BlockSpec((d, d), lambda b, i: (0, 0)),
        pl.BlockSpec((1, bm, d), lambda b, i: (b, i, 0)),
        mod_spec, vec_spec, vec_spec, mod_spec, mod_spec,
        pl.BlockSpec((N_EXPERTS, d), lambda b, i: (0, 0)),
        pl.BlockSpec((N_EXPERTS, 1), lambda b, i: (0, 0))] + [pl.BlockSpec(memory_space=pl.ANY)] * len(carried)
    n_in = len(in_specs)
    n_own = n_in - len(carried)
    kernel = _outproj_even_kernel if kind == "even" else _outproj_odd_kernel

    def body(*refs):
        ins, outs = refs[:n_in], refs[n_in:]
        kernel(*ins[:n_own], ins[n_own:], *outs)

    return pl.pallas_call(
        body,
        grid=(bn, nt),
        in_specs=in_specs,
        out_specs=[pl.BlockSpec((1, bm, d), lambda b, i: (b, i, 0)),
                   pl.BlockSpec((bm, d), lambda b, i: (off + b * nt + i, 0)),
                   pl.BlockSpec((TOP_K, bm), lambda b, i: (0, off + b * nt + i)),
                   pl.BlockSpec((TOP_K, bm), lambda b, i: (0, off + b * nt + i))],
        out_shape=[jax.ShapeDtypeStruct((bn, t, d), F32),
                   jax.ShapeDtypeStruct((ntok, d), BF16),
                   jax.ShapeDtypeStruct((TOP_K, ntok), jnp.int32),
                   jax.ShapeDtypeStruct((TOP_K, ntok), F32)],
        input_output_aliases={n_own + k: 1 + k for k in range(len(carried))},
        compiler_params=_params("parallel", "parallel"),
        name="out_proj_" + kind,
    )(*mixer_args, wo, x, g1, lnw.reshape(1, d), lnb.reshape(1, d), sc2, sh2, rw_t, rb.reshape(N_EXPERTS, 1),
      *carried)


def _outproj_even(a_mix, zin, conv_w, **kw):
    bm = kw["bm"]
    t = zin.shape[1]
    hb = bm // HALO
    last = t // HALO - 1
    cur = lambda sec: pl.BlockSpec((1, bm, B_WIDTH), lambda b, i: (b, i, sec))
    before = lambda sec: pl.BlockSpec((1, HALO, B_WIDTH), lambda b, i: (b, jnp.maximum(i * hb - 1, 0), sec))
    after = lambda sec: pl.BlockSpec((1, HALO, B_WIDTH), lambda b, i: (b, jnp.minimum((i + 1) * hb, last), sec))
    specs = [pl.BlockSpec((1, bm, A_WIDTH), lambda b, i: (b, i, 0)), cur(5), cur(6), cur(7),
             before(5), before(7), after(5), after(7),
             pl.BlockSpec((CONV_W, B_WIDTH), lambda b, i: (0, 0))]
    return _outproj("even", [a_mix, zin, zin, zin, zin, zin, zin, zin, conv_w], specs, **kw)


def _outproj_odd(c_mix, zin, d_norm_w, d_norm_b, sp_w, sp_b, **kw):
    bm = kw["bm"]
    cur = lambda sec: pl.BlockSpec((1, bm, D_WIDTH), lambda b, i: (b, i, sec))
    specs = [pl.BlockSpec((1, bm, C_WIDTH), lambda b, i: (b, i, 0)), cur(1), cur(2),
             pl.BlockSpec((1, D_WIDTH), lambda b, i: (0, 0)), pl.BlockSpec((1, D_WIDTH), lambda b, i: (0, 0)),
             pl.BlockSpec((D_GROUPS, D_CHUNK, D_CHUNK), lambda b, i: (0, 0, 0)),
             pl.BlockSpec((D_CHUNK, D_GROUPS), lambda b, i: (0, 0))]
    return _outproj("odd", [c_mix, zin, zin, d_norm_w.reshape(1, D_WIDTH), d_norm_b.reshape(1, D_WIDTH),
                            sp_w.astype(BF16), sp_b.T], specs, **kw)


def _dft_mats(n):
    ang = 2.0 * np.pi * np.outer(np.arange(n), np.arange(n)) / n
    return np.cos(ang), np.sin(ang)


def _fourier_cw_kernel(z_ref, nw_ref, cs_ref, kc_ref, ks_ref, p_ref, q_ref, ps_ref, qs_ref):
    tile = z_ref.shape[1]
    half = 256
    grows = tile // GRID_W
    pitch = GRID_W + 1
    for hs in range(tile // half):
        z = z_ref[0, hs * half:(hs + 1) * half, :].astype(F32)
        a_parts, b_parts = [], []
        for g in range(C_GROUPS):
            zg = z[:, g * C_GROUP:(g + 1) * C_GROUP]
            zg = zg * lax.rsqrt(jnp.mean(zg * zg, axis=-1, keepdims=True) + EPS) * nw_ref[:, g * C_GROUP:(g + 1) * C_GROUP]
            ab = jnp.dot(zg.astype(BF16), cs_ref[...], preferred_element_type=F32)
            a_parts.append(ab[:, :C_GROUP])
            b_parts.append(ab[:, C_GROUP:])
        ab = jnp.concatenate(a_parts + b_parts, axis=1).astype(BF16)
        m1 = jnp.dot(kc_ref[...], ab, preferred_element_type=F32)
        m2 = jnp.dot(ks_ref[...], ab, preferred_element_type=F32)
        pv = m1[:, :C_WIDTH] - m2[:, C_WIDTH:]
        qv = m2[:, :C_WIDTH] + m1[:, C_WIDTH:]
        for r in range(half // GRID_W):
            at = (hs * (half // GRID_W) + r) * pitch
            for g in range(C_GROUPS):
                ps_ref[g, at:at + GRID_W, :] = pv[r * GRID_W:(r + 1) * GRID_W, g * C_GROUP:(g + 1) * C_GROUP]
                qs_ref[g, at:at + GRID_W, :] = qv[r * GRID_W:(r + 1) * GRID_W, g * C_GROUP:(g + 1) * C_GROUP]
    for k2 in range(GRID_W):
        for g in range(C_GROUPS):
            p_ref[0, k2, :, g * C_GROUP:(g + 1) * C_GROUP] = ps_ref[g, pl.ds(k2, grows, stride=pitch), :]
            q_ref[0, k2, :, g * C_GROUP:(g + 1) * C_GROUP] = qs_ref[g, pl.ds(k2, grows, stride=pitch), :]


def _fourier_r_kernel(p_ref, q_ref, c_ref, s_ref, o_ref, ys_ref):
    nj = p_ref.shape[1]
    rows = p_ref.shape[2]
    pitch = rows + 1
    for j in range(nj):
        y = (jnp.dot(c_ref[...], p_ref[0, j].astype(BF16), preferred_element_type=F32)
             + jnp.dot(s_ref[...], q_ref[0, j].astype(BF16), preferred_element_type=F32))
        for g in range(C_GROUPS):
            ys_ref[g, j * pitch:j * pitch + rows, :] = y[:, g * C_GROUP:(g + 1) * C_GROUP]

    def put(k1, carry):
        for g in range(C_GROUPS):
            o_ref[0, k1, :, g * C_GROUP:(g + 1) * C_GROUP] = ys_ref[g, pl.ds(k1, nj, stride=pitch), :]
        return carry

    lax.fori_loop(0, rows, put, 0)


def _fourier_mix(zin, c_norm_w):
    bn, t, _ = zin.shape
    rows = t // GRID_W
    tile = 512
    grows = tile // GRID_W
    c3, s3 = _dft_mats(C_GROUP)
    c2, s2 = _dft_mats(GRID_W)
    c1, s1 = _dft_mats(rows)
    scale = 1.0 / math.sqrt(rows * GRID_W * C_GROUP)
    cs3 = jnp.asarray(np.concatenate([c3, s3], axis=1), BF16)
    eye = np.eye(256 // GRID_W)
    kc2 = jnp.asarray(np.kron(eye, c2), BF16)
    ks2 = jnp.asarray(np.kron(eye, s2), BF16)
    c1s = jnp.asarray(c1 * scale, BF16)
    s1s = jnp.asarray(-s1 * scale, BF16)
    pq_shape = jax.ShapeDtypeStruct((bn, GRID_W, rows, C_WIDTH), F32)
    pq_spec = pl.BlockSpec((1, GRID_W, grows, C_WIDTH), lambda b, i: (b, 0, i, 0))
    p, q = pl.pallas_call(
        _fourier_cw_kernel,
        grid=(bn, t // tile),
        in_specs=[pl.BlockSpec((1, tile, C_WIDTH), lambda b, i: (b, i, 0)),
                  pl.BlockSpec((1, C_WIDTH), lambda b, i: (0, 0)),
                  pl.BlockSpec((C_GROUP, 2 * C_GROUP), lambda b, i: (0, 0)),
                  pl.BlockSpec((256, 256), lambda b, i: (0, 0)),
                  pl.BlockSpec((256, 256), lambda b, i: (0, 0))],
        out_specs=[pq_spec, pq_spec],
        out_shape=[pq_shape, pq_shape],
        scratch_shapes=[pltpu.VMEM((C_GROUPS, grows * (GRID_W + 1), C_GROUP), F32),
                        pltpu.VMEM((C_GROUPS, grows * (GRID_W + 1), C_GROUP), F32)],
        compiler_params=_params("parallel", "parallel"),
        name="fourier_chan_col",
    )(zin, c_norm_w.reshape(1, C_WIDTH), cs3, kc2, ks2)
    nj = 8
    in_spec = pl.BlockSpec((1, nj, rows, C_WIDTH), lambda b, j: (b, j, 0, 0))
    mat_spec = pl.BlockSpec((rows, rows), lambda b, j: (0, 0))
    y = pl.pallas_call(
        _fourier_r_kernel,
        grid=(bn, GRID_W // nj),
        in_specs=[in_spec, in_spec, mat_spec, mat_spec],
        out_specs=pl.BlockSpec((1, rows, nj, C_WIDTH), lambda b, j: (b, 0, j, 0)),
        out_shape=jax.ShapeDtypeStruct((bn, rows, GRID_W, C_WIDTH), F32),
        scratch_shapes=[pltpu.VMEM((C_GROUPS, nj * (rows + 1), C_GROUP), F32)],
        compiler_params=_params("parallel", "parallel"),
        name="fourier_rows",
    )(p, q, c1s, s1s)
    return y.reshape(bn, t, C_WIDTH)


def _rank_kernel(ti_ref, rank_ref, cnt_ref, run_ref):
    @pl.when(pl.program_id(0) == 0)
    def _():
        run_ref[...] = jnp.zeros_like(run_ref)

    ti = ti_ref[...]
    tb = ti.shape[1]
    eio = lax.broadcasted_iota(jnp.int32, (N_EXPERTS, tb), 0)
    hot = [eio == ti[k:k + 1, :] for k in range(TOP_K)]
    occ = sum(h.astype(F32) for h in hot)
    si = lax.broadcasted_iota(jnp.int32, (tb, tb), 0)
    ti_ = lax.broadcasted_iota(jnp.int32, (tb, tb), 1)
    before = jnp.where(si < ti_, 1.0, 0.0).astype(BF16)
    seen = jnp.dot(occ.astype(BF16), before, preferred_element_type=F32) + run_ref[:, 0:1]
    rank_ref[...] = jnp.concatenate(
        [jnp.sum(jnp.where(h, seen, 0.0), axis=0, keepdims=True) for h in hot], axis=0).astype(jnp.int32)
    run_ref[...] = run_ref[...] + jnp.sum(occ, axis=1, keepdims=True)
    cnt_ref[...] = run_ref[...].astype(jnp.int32)


def _ranks(top_i):
    ntok = top_i.shape[1]
    tb = ROUTE_TILE if ntok % ROUTE_TILE == 0 else 256
    rank, cnt = pl.pallas_call(
        _rank_kernel,
        grid=(ntok // tb,),
        in_specs=[pl.BlockSpec((TOP_K, tb), lambda i: (0, i))],
        out_specs=[pl.BlockSpec((TOP_K, tb), lambda i: (0, i)),
                   pl.BlockSpec((N_EXPERTS, 128), lambda i: (0, 0))],
        out_shape=[jax.ShapeDtypeStruct((TOP_K, ntok), jnp.int32),
                   jax.ShapeDtypeStruct((N_EXPERTS, 128), jnp.int32)],
        scratch_shapes=[pltpu.VMEM((N_EXPERTS, 128), F32)],
        compiler_params=_params("arbitrary"),
        name="route_rank",
    )(top_i)
    return rank, cnt[:, 0]


def _w1_split_kernel(w_ref, g_ref, l_ref, s_ref):
    kdim, cols = w_ref.shape[2], w_ref.shape[3]
    half = cols // 2
    nk = kdim // LANES
    t = w_ref[0, 0].T
    for kc in range(nk):
        s_ref[kc] = t[:, kc * LANES:(kc + 1) * LANES]
    even = jnp.concatenate([s_ref[kc, pl.ds(0, half, stride=2), :] for kc in range(nk)], axis=1)
    odd = jnp.concatenate([s_ref[kc, pl.ds(1, half, stride=2), :] for kc in range(nk)], axis=1)
    g_ref[0] = even.T.astype(BF16)
    l_ref[0] = odd.T.astype(BF16)


def _w1_split(w1_all, layer):
    _, e, d, f2 = w1_all.shape
    cols = 1024
    out = jax.ShapeDtypeStruct((e, d, f2 // 2), BF16)
    out_spec = pl.BlockSpec((1, d, cols // 2), lambda i, c: (i, 0, c))
    return pl.pallas_call(
        _w1_split_kernel,
        grid=(e, f2 // cols),
        in_specs=[pl.BlockSpec((1, 1, d, cols), lambda i, c: (layer, i, 0, c))],
        out_specs=[out_spec, out_spec],
        out_shape=[out, out],
        scratch_shapes=[pltpu.VMEM((d // LANES, cols, LANES), F32)],
        compiler_params=_params("parallel", "parallel"),
        name="moe_w1_split",
    )(w1_all)


def _cast_kernel(w_ref, o_ref):
    o_ref[...] = w_ref[0].astype(o_ref.dtype)


def _w2_cast(w2_all, layer):
    _, e, f, d = w2_all.shape
    return pl.pallas_call(
        _cast_kernel,
        grid=(e,),
        in_specs=[pl.BlockSpec((1, 1, f, d), lambda i: (layer, i, 0, 0))],
        out_specs=pl.BlockSpec((1, f, d), lambda i: (i, 0, 0)),
        out_shape=jax.ShapeDtypeStruct((e, f, d), BF16),
        compiler_params=_params("parallel"),
        name="moe_w2_cast",
    )(w2_all)


def _expert_kernel(be_ref, used_ref, x_ref, w1g_ref, w1l_ref, w2_ref, b1g_ref, b1l_ref, b2_ref, y_ref):
    del be_ref
    rows = x_ref.shape[0] // ROW_TILES
    in_use = pl.program_id(0) < used_ref[0]

    @pl.when(in_use)
    def _():
        x = _load_row_tiles(x_ref, rows).astype(BF16)
        glu = jnp.minimum(jnp.dot(x, w1g_ref[0], preferred_element_type=F32) + b1g_ref[0], SWIGLU_LIMIT)
        lin = jnp.clip(jnp.dot(x, w1l_ref[0], preferred_element_type=F32) + b1l_ref[0],
                       -SWIGLU_LIMIT, SWIGLU_LIMIT)
        act = glu * _sigmoid(SWIGLU_ALPHA * glu) * (lin + 1.0)
        _store_row_tiles(y_ref, jnp.dot(act.astype(BF16), w2_ref[0], preferred_element_type=F32) + b2_ref[0])

    @pl.when(jnp.logical_not(in_use))
    def _():
        y_ref[...] = jnp.zeros_like(y_ref)


def _experts(x_sorted, block_expert, blocks_used, w1g, w1l, w2, b1g, b1l, b2):
    d, f = w1g.shape[1], w1g.shape[2]
    nb = block_expert.shape[0]
    buf_rows = MOE_ROWS * ROW_TILES
    wspec = lambda k, n: pl.BlockSpec((1, k, n), lambda j, be, nu: (be[j], 0, 0))
    row_spec = pl.BlockSpec((buf_rows, LANES), lambda j, be, nu: (j, 0))
    grid_spec = pltpu.PrefetchScalarGridSpec(
        num_scalar_prefetch=2,
        grid=(nb,),
        in_specs=[row_spec, wspec(d, f), wspec(d, f), wspec(f, d), wspec(1, f), wspec(1, f), wspec(1, d)],
        out_specs=row_spec)
    return pl.pallas_call(
        _expert_kernel,
        grid_spec=grid_spec,
        out_shape=jax.ShapeDtypeStruct(x_sorted.shape, F32),
        compiler_params=_params("parallel"),
        name="moe_experts",
    )(block_expert, blocks_used, x_sorted, w1g, w1l, w2, b1g, b1l, b2)


def _dispatch_kernel(cnt_ref, start_ref, dst_ref, h_ref, xs_hbm, rowbuf, zero_ref, sem, zsem, *, nt, nslots):
    i = pl.program_id(0)
    tb = h_ref.shape[0]
    slot = i % 2

    def landed(buf):
        for _ in range(TOP_K):
            pltpu.make_async_copy(rowbuf.at[buf, pl.ds(0, tb * ROW_TILES), :],
                                  xs_hbm.at[pl.ds(0, tb * ROW_TILES), :], sem.at[buf]).wait()

    @pl.when(i >= 2)
    def _():
        landed(slot)

    _store_row_tiles(rowbuf.at[slot], h_ref[...].astype(F32), pitch=ROW_PITCH)
    for r in range(tb):
        for k in range(TOP_K):
            dst = pl.multiple_of(dst_ref[0, k, r] * ROW_TILES, ROW_TILES)
            pltpu.make_async_copy(rowbuf.at[slot, pl.ds(r * ROW_PITCH, ROW_TILES), :],
                                  xs_hbm.at[pl.ds(dst, ROW_TILES), :], sem.at[slot]).start(priority=k % 2)

    @pl.when(i == nt - 1)
    def _():
        landed(slot)
        if nt > 1:
            landed(1 - slot)
        zero_ref[...] = jnp.zeros_like(zero_ref)

        zrows = zero_ref.shape[0] // ROW_TILES

        def zero_rows(pos, n):
            cp = pltpu.make_async_copy(
                zero_ref.at[pl.ds(0, n * ROW_TILES), :],
                xs_hbm.at[pl.ds(pl.multiple_of(pos * ROW_TILES, ROW_TILES), n * ROW_TILES), :], zsem)
            cp.start()
            cp.wait()

        end = 0
        for e in range(N_EXPERTS):
            pos = start_ref[e] + cnt_ref[e]
            end = start_ref[e] + (cnt_ref[e] + MOE_ROWS - 1) // MOE_ROWS * MOE_ROWS
            gap = end - pos
            bit = zrows
            while bit >= 1:
                take = (gap & bit) != 0
                pl.when(take)(functools.partial(zero_rows, pos, bit))
                pos = pos + jnp.where(take, bit, 0)
                bit //= 2

        def tail(i, carry):
            zero_rows(end + i * zrows, zrows)
            return carry
        lax.fori_loop(0, (nslots - end) // zrows, tail, 0)


def _dispatch(h2, dest3, counts, pad_starts, nslots):
    ntok, d = h2.shape
    tb = COMBINE_TILE
    nt = ntok // tb
    assert MOE_ROWS & (MOE_ROWS - 1) == 0, "the padding fill decomposes gap lengths into powers of two"
    grid_spec = pltpu.PrefetchScalarGridSpec(
        num_scalar_prefetch=2,
        grid=(nt,),
        in_specs=[pl.BlockSpec((1, TOP_K, tb), lambda i, c, s: (i, 0, 0), memory_space=pltpu.SMEM),
                  pl.BlockSpec((tb, d), lambda i, c, s: (i, 0))],
        out_specs=pl.BlockSpec(memory_space=pl.ANY),
        scratch_shapes=[pltpu.VMEM((2, tb * ROW_PITCH, LANES), F32),
                        pltpu.VMEM((MOE_ROWS // 2 * ROW_TILES, LANES), F32),
                        pltpu.SemaphoreType.DMA((2,)), pltpu.SemaphoreType.DMA(())])
    return pl.pallas_call(
        functools.partial(_dispatch_kernel, nt=nt, nslots=nslots),
        grid_spec=grid_spec,
        out_shape=jax.ShapeDtypeStruct((nslots * ROW_TILES, LANES), F32),
        compiler_params=_params("arbitrary"),
        name="moe_dispatch",
    )(counts, pad_starts, dest3, h2)


def _combine_kernel(dst_ref, dstn_ref, y_hbm, x1_ref, gate_ref, g2_ref, lnw_ref, lnb_ref, o_ref, ybuf, sem, *, nt):
    i = pl.program_id(0)
    tb = x1_ref.shape[0]
    slot = i % 2

    def row_copy(src, buf, k, r):
        rows = y_hbm.at[pl.ds(pl.multiple_of(src * ROW_TILES, ROW_TILES), ROW_TILES), :]
        return pltpu.make_async_copy(rows, ybuf.at[buf, k, pl.ds(r * ROW_PITCH, ROW_TILES), :], sem.at[buf])

    def gather(idx_ref, buf):
        for r in range(tb):
            for k in range(TOP_K):
                row_copy(idx_ref[0, k, r], buf, k, r).start(priority=k % 2)

    @pl.when(i == 0)
    def _():
        gather(dst_ref, 0)

    @pl.when(i + 1 < nt)
    def _():
        gather(dstn_ref, 1 - slot)

    for k in range(TOP_K):
        pltpu.make_async_copy(y_hbm.at[pl.ds(0, tb * ROW_TILES), :],
                              ybuf.at[slot, k, pl.ds(0, tb * ROW_TILES), :], sem.at[slot]).wait()

    y = sum(gate_ref[:, k:k + 1] * _load_row_tiles(ybuf.at[slot, k], tb, pitch=ROW_PITCH) for k in range(TOP_K))
    r = DN_ALPHA * x1_ref[...] + g2_ref[0] * y
    mu = jnp.mean(r, axis=-1, keepdims=True)
    rc = r - mu
    var = jnp.mean(rc * rc, axis=-1, keepdims=True)
    o_ref[...] = rc * lax.rsqrt(var + EPS) * lnw_ref[...] + lnb_ref[...]


def _combine(x1, y_sorted, dest3, gates_t, tok_off, g2, mod_row, lnw, lnb):
    bn, t, d = x1.shape
    n = bn * t
    tb = COMBINE_TILE
    nt = n // tb
    off = tok_off // tb
    per_b = t // tb
    out = pl.pallas_call(
        functools.partial(_combine_kernel, nt=nt),
        grid=(nt,),
        in_specs=[pl.BlockSpec((1, TOP_K, tb), lambda i: (off + i, 0, 0), memory_space=pltpu.SMEM),
                  pl.BlockSpec((1, TOP_K, tb), lambda i: (off + jnp.minimum(i + 1, nt - 1), 0, 0),
                               memory_space=pltpu.SMEM),
                  pl.BlockSpec(memory_space=pl.ANY),
                  pl.BlockSpec((tb, d), lambda i: (i, 0)),
                  pl.BlockSpec((tb, TOP_K), lambda i: (off + i, 0)),
                  pl.BlockSpec((1, 1, d), lambda i: (mod_row(i // per_b), 0, 0)),
                  pl.BlockSpec((1, d), lambda i: (0, 0)),
                  pl.BlockSpec((1, d), lambda i: (0, 0))],
        out_specs=pl.BlockSpec((tb, d), lambda i: (i, 0)),
        out_shape=jax.ShapeDtypeStruct((n, d), F32),
        scratch_shapes=[pltpu.VMEM((2, TOP_K, tb * ROW_PITCH, LANES), F32), pltpu.SemaphoreType.DMA((2,))],
        compiler_params=_params("arbitrary"),
        name="moe_combine",
    )(dest3, dest3, y_sorted, x1.reshape(n, d), gates_t, g2, lnw.reshape(1, d), lnb.reshape(1, d))
    return out.reshape(bn, t, d)


def _dest_kernel(ti_ref, rank_ref, start_ref, dest_ref):
    ti = ti_ref[...]
    tb = ti.shape[1]
    eio = lax.broadcasted_iota(jnp.int32, (N_EXPERTS, tb), 0)
    start = start_ref[:, 0:1].astype(F32)
    base = jnp.concatenate(
        [jnp.sum(jnp.where(eio == ti[k:k + 1, :], start, 0.0), axis=0, keepdims=True) for k in range(TOP_K)], axis=0)
    dest = base.astype(jnp.int32) + rank_ref[...]
    for c in range(tb // COMBINE_TILE):
        dest_ref[c] = dest[:, c * COMBINE_TILE:(c + 1) * COMBINE_TILE]


def _moe_layout(top_i):
    ntok = top_i.shape[1]
    m = ntok * TOP_K
    rank, counts = _ranks(top_i)
    padded = (counts + MOE_ROWS - 1) // MOE_ROWS * MOE_ROWS
    pad_ends = jnp.cumsum(padded)
    pad_starts = pad_ends - padded
    tb = ROUTE_TILE if ntok % ROUTE_TILE == 0 else 256
    ntiles = ntok // COMBINE_TILE
    dest = pl.pallas_call(
        _dest_kernel,
        grid=(ntok // tb,),
        in_specs=[pl.BlockSpec((TOP_K, tb), lambda i: (0, i)),
                  pl.BlockSpec((TOP_K, tb), lambda i: (0, i)),
                  pl.BlockSpec((N_EXPERTS, LANES), lambda i: (0, 0))],
        out_specs=pl.BlockSpec((tb // COMBINE_TILE, TOP_K, COMBINE_TILE), lambda i: (i, 0, 0)),
        out_shape=jax.ShapeDtypeStruct((ntiles, TOP_K, COMBINE_TILE), jnp.int32),
        compiler_params=_params("parallel"),
        name="route_dest",
    )(top_i, rank, jnp.broadcast_to(pad_starts[:, None], (N_EXPERTS, LANES)))
    nb = (m + N_EXPERTS * (MOE_ROWS - 1) + MOE_ROWS - 1) // MOE_ROWS
    block_starts = jnp.arange(nb, dtype=jnp.int32) * MOE_ROWS
    block_expert = jnp.minimum(jnp.sum(block_starts[:, None] >= pad_ends[None, :], axis=1),
                               N_EXPERTS - 1).astype(jnp.int32)
    blocks_used = (pad_ends[-1:] // MOE_ROWS).astype(jnp.int32)
    return dest, counts, pad_starts.astype(jnp.int32), block_expert, blocks_used, nb * MOE_ROWS


def kernel(x, c, ctx, c_ctx, lower_bounds_fwd, lower_bounds_bwd, ada_w, ada_b, w_in_even, a_norm_w, conv_w,
           w_in_odd, c_norm_w, d_norm_w, d_norm_b, spatial_w, spatial_b, w_out, ln_mix_w, ln_mix_b,
           ln_ffn_w, ln_ffn_b, router_w, router_b, moe_w1, moe_b1, moe_w2, moe_b2):
    bn, t, d = x.shape
    tc = ctx.shape[1]
    n_lat = bn * t
    n_ctx = bn * tc
    ctx_row = bn
    lat_row = lambda b: b
    ctx_mod_row = lambda b: ctx_row

    lb_f_all = jnp.cumsum(jax.nn.softmax(lower_bounds_fwd.astype(F32), axis=0), axis=0)
    lb_b_all = jnp.cumsum(jax.nn.softmax(lower_bounds_bwd.astype(F32), axis=0), axis=0)

    cond = jnp.zeros((MOD_ROWS, d), F32).at[:bn].set(c).at[ctx_row].set(c_ctx)
    mods = _ada(cond, ada_w, ada_b).reshape(DEPTH, MOD_ROWS, 6, 1, d)

    h_ctx = ctx
    for l in range(DEPTH):
        last = l == DEPTH - 1
        sh1, sc1, g1, sh2, sc2, g2 = (mods[l, :, j] for j in range(6))
        wo = w_out[l].astype(BF16)
        rw_t = router_w[l].T.astype(BF16)
        ntok = n_lat if last else n_lat + n_ctx
        make = jax.ShapeDtypeStruct if last else jnp.zeros
        tok_state = (make((ntok, d), BF16), make((TOP_K, ntok), jnp.int32), make((TOP_K, ntok), F32))
        post = dict(wo=wo, lnw=ln_mix_w[l], lnb=ln_mix_b[l], rw_t=rw_t, rb=router_b[l])
        bm_ctx = min(ROW_TILE, tc)
        if l % 2 == 0:
            e = l // 2
            w_in = w_in_even[e].astype(BF16)
            zin_c = _inproj(h_ctx, sc1, sh1, w_in, ctx_mod_row, bm=bm_ctx)
            zin = _inproj(x, sc1, sh1, w_in, lat_row, bm=ROW_TILE)
            zeros = jnp.zeros((bn, A_HEADS, A_DV, A_DK), F32)
            oc_f, sc_f = _scan(zin_c, lb_f_all[l], zeros, reverse=False)
            a_ctx, sc_b = _scan(zin_c, lb_b_all[l], zeros, reverse=True, o_prev=oc_f, norm_w=a_norm_w[e])
            o_f, _ = _scan(zin, lb_f_all[l], sc_f, reverse=False)
            a_lat, _ = _scan(zin, lb_b_all[l], sc_b, reverse=True, o_prev=o_f, norm_w=a_norm_w[e])
            x1, *tok_state = _outproj_even(a_lat, zin, conv_w[e], x=x, mods=(g1, sc2, sh2), mod_row=lat_row,
                                           tok_state=tok_state, tok_off=0, bm=ROW_TILE, **post)
            if not last:
                hc1, *tok_state = _outproj_even(a_ctx, zin_c, conv_w[e], x=h_ctx, mods=(g1, sc2, sh2),
                                                mod_row=ctx_mod_row, tok_state=tok_state, tok_off=n_lat,
                                                bm=bm_ctx, **post)
        else:
            o = l // 2
            w_in = w_in_odd[o].astype(BF16)
            odd = dict(d_norm_w=d_norm_w[o], d_norm_b=d_norm_b[o], sp_w=spatial_w[o], sp_b=spatial_b[o])
            zin = _inproj(x, sc1, sh1, w_in, lat_row, bm=ROW_TILE)
            c_mix = _fourier_mix(zin, c_norm_w[o])
            x1, *tok_state = _outproj_odd(c_mix, zin, x=x, mods=(g1, sc2, sh2), mod_row=lat_row,
                                          tok_state=tok_state, tok_off=0, bm=ROW_TILE, **odd, **post)
            if not last:
                raise NotImplementedError("an odd layer that is not the last needs the context Fourier mixer")
        h2_all, top_i, top_g = tok_state
        dest, counts, pad_starts, block_expert, blocks_used, nslots = _moe_layout(top_i)
        x_sorted = _dispatch(h2_all, dest, counts, pad_starts, nslots)
        w1g, w1l = _w1_split(moe_w1, l)
        y_sorted = _experts(
            x_sorted, block_expert, blocks_used, w1g, w1l, _w2_cast(moe_w2, l),
            moe_b1[l][:, None, 0::2], moe_b1[l][:, None, 1::2], moe_b2[l][:, None, :])
        gates_t = top_g.T
        x = _combine(x1, y_sorted, dest, gates_t, 0, g2, lat_row, ln_ffn_w[l], ln_ffn_b[l])
        if not last:
            h_ctx = _combine(hc1, y_sorted, dest, gates_t, n_lat, g2, ctx_mod_row, ln_ffn_w[l], ln_ffn_b[l])
    return x
```
